```python
import jax, jax.numpy as jnp
from jax import lax
import numpy as np

D_MODEL = 1024
BATCH = 8
SEQ = 8192
DEPTH = 4

CTX_LEN = 256
GRID_W = 64
D_MIX = D_MODEL
MLA_HEADS = 8
MLA_NOPE = 64
MLA_ROPE = 32
MLA_V = 64
MLA_Q_RANK = 384
MLA_KV_RANK = 256
MLA_OUT = MLA_HEADS * MLA_V
ATTN_SCALE = (MLA_NOPE + MLA_ROPE) ** -0.5
Q_BLOCK = 128
ROPE_BASE = 10000.0
CONV_CH = 256
CONV_WIDTH = 31
SGU_HEADS = 4
SGU_HEAD_DIM = 64
SGU_CH = SGU_HEADS * SGU_HEAD_DIM
CHUNK = 128
D_FF = 4 * D_MODEL
EPS = 1e-6
IN_Q = MLA_Q_RANK
IN_KV = MLA_KV_RANK
IN_KR = MLA_ROPE
IN_CONV = 2 * CONV_CH
IN_SGU = 2 * SGU_CH
D_IN = IN_Q + IN_KV + IN_KR + IN_CONV + IN_SGU
IN_SPLITS = (IN_Q, IN_Q + IN_KV, IN_Q + IN_KV + IN_KR, IN_Q + IN_KV + IN_KR + IN_CONV)

kernel_name = "hybrid_mla_conv_sgu_dit_block"


def rmsnorm(x, g):
    xf = x.astype(jnp.float32)
    y = xf * lax.rsqrt(jnp.mean(xf * xf, -1, keepdims=True) + EPS)
    return (y * g.astype(jnp.float32)).astype(x.dtype)


def layernorm(x, g, b):
    xf = x.astype(jnp.float32)
    mu = jnp.mean(xf, -1, keepdims=True)
    var = jnp.mean(jnp.square(xf - mu), -1, keepdims=True)
    y = (xf - mu) * lax.rsqrt(var + EPS) * g.astype(jnp.float32) + b.astype(jnp.float32)
    return y.astype(x.dtype)


def modulate(h, shift, scale):
    return h * (1 + scale) + shift


def _rot(x, pos):
    n = x.shape[-1] // 2
    inv = 1.0 / (ROPE_BASE ** (jnp.arange(n, dtype=jnp.float32) / n))
    ang = pos.astype(jnp.float32)[:, None] * inv
    cos = jnp.cos(ang)[:, None, :]
    sin = jnp.sin(ang)[:, None, :]
    x1, x2 = x[..., :n], x[..., n:]
    return jnp.concatenate([x1 * cos - x2 * sin, x2 * cos + x1 * sin], -1).astype(x.dtype)


def rope_2d(x, row, col):
    h = x.shape[-1] // 2
    return jnp.concatenate([_rot(x[..., :h], row), _rot(x[..., h:], col)], -1)


def mla_q(z_q, qn_g, w_uq, row, col):
    b, l, _ = z_q.shape
    q = (rmsnorm(z_q, qn_g) @ w_uq).reshape(b, l, MLA_HEADS, MLA_NOPE + MLA_ROPE)
    q_nope, q_rope = q[..., :MLA_NOPE], q[..., MLA_NOPE:]
    if row is not None:
        q_rope = rope_2d(q_rope, row, col)
    return jnp.concatenate([q_nope, q_rope], -1)


def mla_kv(z_kv, z_kr, kvn_g, w_ukv, row, col):
    b, l, _ = z_kv.shape
    kv = (rmsnorm(z_kv, kvn_g) @ w_ukv).reshape(b, l, MLA_HEADS, MLA_NOPE + MLA_V)
    k_nope, v = kv[..., :MLA_NOPE], kv[..., MLA_NOPE:]
    k_rope = z_kr[:, :, None, :]
    if row is not None:
        k_rope = rope_2d(k_rope, row, col)
    k_rope = jnp.broadcast_to(k_rope, (b, l, MLA_HEADS, MLA_ROPE))
    return jnp.concatenate([k_nope, k_rope], -1), v


def attend(q, k, v):
    s = jnp.einsum('bqhd,bkhd->bhqk', q, k).astype(jnp.float32) * ATTN_SCALE
    p = jax.nn.softmax(s, axis=-1).astype(v.dtype)
    return jnp.einsum('bhqk,bkhd->bqhd', p, v)


def attend_blocked(q, k, v):
    b, l, h, dq = q.shape
    qb = q.reshape(b, l // Q_BLOCK, Q_BLOCK, h, dq).transpose(1, 0, 2, 3, 4)
    out = lax.map(lambda qi: attend(qi, k, v), qb)
    return out.transpose(1, 0, 2, 3, 4).reshape(b, l, h * v.shape[-1])


def conv_module(z, w, b, ln_g, ln_b):
    a, gate = jnp.split(z, 2, axis=-1)
    y = a * jax.nn.sigmoid(gate)
    y = lax.conv_general_dilated(
        y, w[:, None, :].astype(y.dtype), window_strides=(1,), padding='SAME',
        dimension_numbers=('NWC', 'WIO', 'NWC'), feature_group_count=CONV_CH) + b
    return jax.nn.silu(layernorm(y, ln_g, ln_b))


def sgu_module(z, ln_g, ln_b, w_s, b_s):
    z = jax.nn.gelu(z)
    u, v = jnp.split(z, 2, axis=-1)
    v = layernorm(v, ln_g, ln_b)
    bsz, l, _ = v.shape
    v = v.reshape(bsz, l // CHUNK, CHUNK, SGU_HEADS, SGU_HEAD_DIM)
    mixed = jnp.einsum('hpq,bnqhd->bnphd', w_s, v) + b_s.T[None, None, :, :, None]
    return u * mixed.reshape(bsz, l, SGU_CH)


def sq_relu_mlp(h, w1, w2):
    return jnp.square(jax.nn.relu(h @ w1)) @ w2


def setup_inputs(seed: int = 0) -> dict:
    key = jax.random.key(seed)
    ks = jax.random.split(key, 32)
    f32 = jnp.float32
    L = DEPTH

    def nrm(k, shape, scale):
        return jax.random.normal(k, shape, f32) * scale

    def gain(k, shape):
        return 1.0 + 0.05 * jax.random.normal(k, shape, f32)

    return {
        "x": nrm(ks[0], (BATCH, SEQ, D_MODEL), 1.0),
        "c": nrm(ks[1], (BATCH, D_MODEL), 1.0),
        "ctx": nrm(ks[2], (BATCH, CTX_LEN, D_MODEL), 1.0),
        "c_ctx": nrm(ks[3], (D_MODEL,), 1.0),
        "ada_w": nrm(ks[4], (L, D_MODEL, 6 * D_MODEL), 0.5 * D_MODEL ** -0.5),
        "ada_b": nrm(ks[5], (L, 6 * D_MODEL), 0.02),
        "norm1_g": gain(ks[6], (L, D_MODEL)),
        "norm2_g": gain(ks[7], (L, D_MODEL)),
        "w_in": nrm(ks[8], (L, D_MODEL, D_IN), D_MODEL ** -0.5),
        "q_norm_g": gain(ks[9], (L, MLA_Q_RANK)),
        "w_uq": nrm(ks[10], (L, MLA_Q_RANK, MLA_HEADS * (MLA_NOPE + MLA_ROPE)), MLA_Q_RANK ** -0.5),
        "kv_norm_g": gain(ks[11], (L, MLA_KV_RANK)),
        "w_ukv": nrm(ks[12], (L, MLA_KV_RANK, MLA_HEADS * (MLA_NOPE + MLA_V)), MLA_KV_RANK ** -0.5),
        "conv_w": nrm(ks[13], (L, CONV_WIDTH, CONV_CH), CONV_WIDTH ** -0.5),
        "conv_b": nrm(ks[14], (L, CONV_CH), 0.02),
        "conv_ln_g": gain(ks[15], (L, CONV_CH)),
        "conv_ln_b": nrm(ks[16], (L, CONV_CH), 0.02),
        "sgu_ln_g": gain(ks[17], (L, SGU_CH)),
        "sgu_ln_b": nrm(ks[18], (L, SGU_CH), 0.02),
        "sgu_w": nrm(ks[19], (L, SGU_HEADS, CHUNK, CHUNK), CHUNK ** -0.5),
        "sgu_b": gain(ks[20], (L, SGU_HEADS, CHUNK)),
        "w_out": nrm(ks[21], (L, D_MIX, D_MODEL), D_MIX ** -0.5),
        "w_ff1": nrm(ks[22], (L, D_MODEL, D_FF), D_MODEL ** -0.5),
        "w_ff2": nrm(ks[23], (L, D_FF, D_MODEL), D_FF ** -0.5),
        "final_g": gain(ks[24], (D_MODEL,)),
    }


def reference(x, c, ctx, c_ctx, ada_w, ada_b, norm1_g, norm2_g, w_in, q_norm_g, w_uq,
              kv_norm_g, w_ukv, conv_w, conv_b, conv_ln_g, conv_ln_b, sgu_ln_g, sgu_ln_b,
              sgu_w, sgu_b, w_out, w_ff1, w_ff2, final_g):
    bsz, seq, _ = x.shape
    n_ctx = ctx.shape[1]
    rows = seq // GRID_W
    row = jnp.repeat(jnp.arange(rows, dtype=jnp.int32), GRID_W)
    col = jnp.tile(jnp.arange(GRID_W, dtype=jnp.int32), rows)
    s_lat = jax.nn.silu(c)
    s_ctx = jax.nn.silu(c_ctx)
    cx = ctx
    for l in range(DEPTH):
        last = l == DEPTH - 1
        m_x = (s_lat @ ada_w[l] + ada_b[l])[:, None, :]
        m_c = s_ctx @ ada_w[l] + ada_b[l]
        sh1, sc1, g1, sh2, sc2, g2 = jnp.split(m_x, 6, axis=-1)
        csh1, csc1, cg1, csh2, csc2, cg2 = jnp.split(m_c, 6, axis=-1)

        zx = modulate(rmsnorm(x, norm1_g[l]), sh1, sc1) @ w_in[l]
        zc = modulate(rmsnorm(cx, norm1_g[l]), csh1, csc1) @ w_in[l]
        xq, xkv, xkr, xconv, xsgu = jnp.split(zx, IN_SPLITS, axis=-1)
        cq, ckv, ckr, cconv, csgu = jnp.split(zc, IN_SPLITS, axis=-1)

        q_x = mla_q(xq, q_norm_g[l], w_uq[l], row, col)
        k_x, v_x = mla_kv(xkv, xkr, kv_norm_g[l], w_ukv[l], row, col)
        k_c, v_c = mla_kv(ckv, ckr, kv_norm_g[l], w_ukv[l], None, None)
        a_x = attend_blocked(q_x, jnp.concatenate([k_c, k_x], 1), jnp.concatenate([v_c, v_x], 1))
        y_x = jnp.concatenate([
            a_x,
            conv_module(xconv, conv_w[l], conv_b[l], conv_ln_g[l], conv_ln_b[l]),
            sgu_module(xsgu, sgu_ln_g[l], sgu_ln_b[l], sgu_w[l], sgu_b[l]),
        ], axis=-1) @ w_out[l]
        x = x + g1 * y_x

        if not last:
            q_c = mla_q(cq, q_norm_g[l], w_uq[l], None, None)
            a_c = attend(q_c, k_c, v_c).reshape(bsz, n_ctx, MLA_OUT)
            y_c = jnp.concatenate([
                a_c,
                conv_module(cconv, conv_w[l], conv_b[l], conv_ln_g[l], conv_ln_b[l]),
                sgu_module(csgu, sgu_ln_g[l], sgu_ln_b[l], sgu_w[l], sgu_b[l]),
            ], axis=-1) @ w_out[l]
            cx = cx + cg1 * y_c

        x = x + g2 * sq_relu_mlp(modulate(rmsnorm(x, norm2_g[l]), sh2, sc2), w_ff1[l], w_ff2[l])
        if not last:
            cx = cx + cg2 * sq_relu_mlp(modulate(rmsnorm(cx, norm2_g[l]), csh2, csc2), w_ff1[l], w_ff2[l])

    return rmsnorm(x, final_g)
```

```python
import functools
import math

import jax
import jax.numpy as jnp
from jax import lax
from jax.experimental import pallas as pl
from jax.experimental.pallas import tpu as pltpu

F32 = jnp.float32
BF16 = jnp.bfloat16

GRID_W = 64
HEADS = 8
NOPE = 64
ROPE = 32
VDIM = 64
QK_PAD = 128
Q_RANK = 384
KV_RANK = 256
CONV_CH = 256
CONV_WIDTH = 31
CONV_HALO = 16
SGU_HEADS = 4
SGU_HEAD_DIM = 64
SGU_CH = 256
CHUNK = 128
EPS = 1e-6
ROPE_BASE = 10000.0
ATTN_SCALE = (NOPE + ROPE) ** -0.5
LOG2E = math.log2(math.e)

TOK_TILE = 256
Q_TILE = 256
KV_TILE = 512
FF_CHUNK = 1024
ADA_COLS = 1536
VMEM_LIMIT_BYTES = 56 * 1024 * 1024

ZC_Q = 0
ZC_KV = ZC_Q + Q_RANK
ZC_CONV = ZC_KV + KV_RANK
ZC_SGU = ZC_CONV + 2 * CONV_CH
ZC_KR = ZC_SGU + 2 * SGU_CH
ZC_KRR = ZC_KR + QK_PAD
Z_COLS = ZC_KRR + QK_PAD


def _cparams(sem):
    return pltpu.CompilerParams(dimension_semantics=sem, vmem_limit_bytes=VMEM_LIMIT_BYTES)


def _sigmoid(v):
    return 1.0 / (1.0 + jnp.exp(-v))


def _layernorm(v, g, b):
    mu = jnp.mean(v, -1, keepdims=True)
    d = v - mu
    var = jnp.mean(d * d, -1, keepdims=True)
    return d * lax.rsqrt(var + EPS) * g + b


def _rms(v, g):
    return v * lax.rsqrt(jnp.mean(v * v, -1, keepdims=True) + EPS) * g


def _dot(a, b):
    return jnp.dot(a, b, preferred_element_type=F32)


def _dot_nt(a, b):
    return lax.dot_general(a, b, (((1,), (1,)), ((), ())), preferred_element_type=F32)


def _dot_tn(a, b):
    return lax.dot_general(a, b, (((0,), (0,)), ((), ())), preferred_element_type=F32)


def _ada_kernel(c_ref, w_ref, b_ref, o_ref):
    c = c_ref[...]
    s = (c * _sigmoid(c)).astype(BF16)
    o_ref[0] = _dot(s, w_ref[0].astype(BF16)) + b_ref[0]


def _ada_call(cvec, ada_w, ada_b):
    n_layers, d, n6 = ada_w.shape
    rows = cvec.shape[0]
    return pl.pallas_call(
        _ada_kernel,
        grid=(n_layers, n6 // ADA_COLS),
        in_specs=[
            pl.BlockSpec((rows, d), lambda l, j: (0, 0)),
            pl.BlockSpec((1, d, ADA_COLS), lambda l, j: (l, 0, j)),
            pl.BlockSpec((1, 1, ADA_COLS), lambda l, j: (l, 0, j)),
        ],
        out_specs=pl.BlockSpec((1, rows, ADA_COLS), lambda l, j: (l, 0, j)),
        out_shape=jax.ShapeDtypeStruct((n_layers, rows, n6), F32),
        compiler_params=_cparams(("arbitrary", "arbitrary")),
        name="adaln",
    )(cvec, ada_w, ada_b.reshape(n_layers, 1, n6))


def _pre_kernel(x_ref, xp_ref, xn_ref, mod_ref, n1g_ref, win_ref, qg_ref, wqT_ref, wqrT_ref,
                kvg_ref, wk_ref, wvT_ref, cosk_ref, sink_ref, cosq_ref, sinq_ref,
                convw_ref, convb_ref, clng_ref, clnb_ref, slng_ref, slnb_ref, sguw_ref, sgub_ref,
                qT_out, k_out, vT_out, yc_out, ys_out, ybuf, *, n_tiles):
    i = pl.program_id(1)
    tt = x_ref.shape[1]
    g1 = n1g_ref[0]
    sh1 = mod_ref[0, 0, 0:1, :]
    sc1 = mod_ref[0, 0, 1:2, :]

    def normmod(xt):
        return (_rms(xt, g1) * (1.0 + sc1) + sh1).astype(BF16)

    z = _dot(normmod(x_ref[0]), win_ref[0])

    qn = _rms(z[:, ZC_Q:ZC_Q + Q_RANK], qg_ref[0]).astype(BF16)
    qT = _dot_nt(wqT_ref[0], qn)
    qrT = _dot_nt(wqrT_ref[0], qn)
    cq = cosq_ref[...]
    sq = sinq_ref[...]
    qscale = ATTN_SCALE * LOG2E
    for h in range(HEADS):
        r0 = h * QK_PAD
        qT_out[0, r0:r0 + NOPE, :] = (qT[r0:r0 + NOPE, :] * qscale).astype(BF16)
        rot = qT[r0 + NOPE:r0 + NOPE + ROPE, :] * cq + qrT[h * ROPE:(h + 1) * ROPE, :] * sq
        qT_out[0, r0 + NOPE:r0 + NOPE + ROPE, :] = (rot * qscale).astype(BF16)
        qT_out[0, r0 + NOPE + ROPE:r0 + QK_PAD, :] = jnp.zeros((QK_PAD - NOPE - ROPE, tt), BF16)

    kvn = _rms(z[:, ZC_KV:ZC_KV + KV_RANK], kvg_ref[0]).astype(BF16)
    kext = _dot(kvn, wk_ref[0])
    krope = z[:, ZC_KR:ZC_KR + QK_PAD] * cosk_ref[...] + z[:, ZC_KRR:ZC_KRR + QK_PAD] * sink_ref[...]
    for h in range(HEADS):
        c0 = h * QK_PAD
        k_out[0, :, c0:c0 + QK_PAD] = (kext[:, c0:c0 + QK_PAD] + krope).astype(BF16)
    vT_out[0] = _dot_nt(wvT_ref[0], kvn).astype(BF16)

    def glu(zc):
        return zc[:, :CONV_CH] * _sigmoid(zc[:, CONV_CH:])

    wconv_in = win_ref[0, :, ZC_CONV:ZC_CONV + 2 * CONV_CH]
    first = jnp.logical_or(i == 0, i == n_tiles - 1)
    last = i >= n_tiles - 2
    yp = glu(_dot(normmod(xp_ref[0]), wconv_in))
    yn = glu(_dot(normmod(xn_ref[0]), wconv_in))
    ybuf[0:CONV_HALO, :] = jnp.where(first, 0.0, yp)
    ybuf[CONV_HALO:CONV_HALO + tt, :] = glu(z[:, ZC_CONV:ZC_CONV + 2 * CONV_CH])
    ybuf[CONV_HALO + tt:, :] = jnp.where(last, 0.0, yn)
    cw = convw_ref[0]
    acc = jnp.zeros((tt, CONV_CH), F32)
    for k in range(CONV_WIDTH):
        acc = acc + ybuf[k + 1:k + 1 + tt, :] * cw[k:k + 1, :]
    yc = _layernorm(acc + convb_ref[0], clng_ref[0], clnb_ref[0])
    yc_out[0] = (yc * _sigmoid(yc)).astype(BF16)

    zs = z[:, ZC_SGU:ZC_SGU + 2 * SGU_CH]
    zg = 0.5 * zs * (1.0 + jnp.tanh(math.sqrt(2.0 / math.pi) * (zs + 0.044715 * (zs * zs * zs))))
    u = zg[:, :SGU_CH]
    v = _layernorm(zg[:, SGU_CH:], slng_ref[0], slnb_ref[0]).astype(BF16)
    lane = lax.broadcasted_iota(jnp.int32, (CHUNK, SGU_CH), 1)
    bias = sgub_ref[0]
    for c in range(tt // CHUNK):
        vc = v[c * CHUNK:(c + 1) * CHUNK, :]
        mixed = _dot(sguw_ref[0, 0], vc)
        for h in range(1, SGU_HEADS):
            mixed = jnp.where(lane >= h * SGU_HEAD_DIM, _dot(sguw_ref[0, h], vc), mixed)
        ys_out[0, c * CHUNK:(c + 1) * CHUNK, :] = (u[c * CHUNK:(c + 1) * CHUNK, :] * (mixed + bias)).astype(BF16)


def _pre_call(layer, xc, modsel, p, n_lat_tiles):
    bsz, ltot, d = xc.shape
    tt = TOK_TILE
    n_tiles = ltot // tt
    hb = tt // CONV_HALO
    n_hblocks = ltot // CONV_HALO

    def lay(shape):
        nd = len(shape)
        return pl.BlockSpec((1,) + shape, lambda b, i: (layer,) + (0,) * nd)

    in_specs = [
        pl.BlockSpec((1, tt, d), lambda b, i: (b, i, 0)),
        pl.BlockSpec((1, CONV_HALO, d), lambda b, i: (b, jnp.maximum(i * hb - 1, 0), 0)),
        pl.BlockSpec((1, CONV_HALO, d), lambda b, i: (b, jnp.minimum((i + 1) * hb, n_hblocks - 1), 0)),
        pl.BlockSpec((1, 1, 8, d), lambda b, i: (b, i // n_lat_tiles, 0, 0)),
        lay((1, d)),
        lay((d, Z_COLS)),
        lay((1, Q_RANK)),
        lay((HEADS * QK_PAD, Q_RANK)),
        lay((HEADS * ROPE, Q_RANK)),
        lay((1, KV_RANK)),
        lay((KV_RANK, HEADS * QK_PAD)),
        lay((HEADS * VDIM, KV_RANK)),
        pl.BlockSpec((tt, QK_PAD), lambda b, i: (i, 0)),
        pl.BlockSpec((tt, QK_PAD), lambda b, i: (i, 0)),
        pl.BlockSpec((ROPE, tt), lambda b, i: (0, i)),
        pl.BlockSpec((ROPE, tt), lambda b, i: (0, i)),
        lay((CONV_WIDTH, CONV_CH)),
        lay((1, CONV_CH)),
        lay((1, CONV_CH)),
        lay((1, CONV_CH)),
        lay((1, SGU_CH)),
        lay((1, SGU_CH)),
        lay((SGU_HEADS, CHUNK, CHUNK)),
        lay((CHUNK, SGU_CH)),
    ]
    out_specs = [
        pl.BlockSpec((1, HEADS * QK_PAD, tt), lambda b, i: (b, 0, i)),
        pl.BlockSpec((1, tt, HEADS * QK_PAD), lambda b, i: (b, i, 0)),
        pl.BlockSpec((1, HEADS * VDIM, tt), lambda b, i: (b, 0, i)),
        pl.BlockSpec((1, tt, CONV_CH), lambda b, i: (b, i, 0)),
        pl.BlockSpec((1, tt, SGU_CH), lambda b, i: (b, i, 0)),
    ]
    out_shape = [
        jax.ShapeDtypeStruct((bsz, HEADS * QK_PAD, ltot), BF16),
        jax.ShapeDtypeStruct((bsz, ltot, HEADS * QK_PAD), BF16),
        jax.ShapeDtypeStruct((bsz, HEADS * VDIM, ltot), BF16),
        jax.ShapeDtypeStruct((bsz, ltot, CONV_CH), BF16),
        jax.ShapeDtypeStruct((bsz, ltot, SGU_CH), BF16),
    ]
    return pl.pallas_call(
        functools.partial(_pre_kernel, n_tiles=n_tiles),
        grid=(bsz, n_tiles),
        in_specs=in_specs,
        out_specs=out_specs,
        out_shape=out_shape,
        scratch_shapes=[pltpu.VMEM((tt + 2 * CONV_HALO, CONV_CH), F32)],
        compiler_params=_cparams(("arbitrary", "arbitrary")),
        name="pre",
    )(xc, xc, xc, modsel, p["norm1_g"], p["w_in"], p["q_norm_g"], p["wqT"], p["wqrT"],
      p["kv_norm_g"], p["wk"], p["wvT"], p["cosk"], p["sink"], p["cosq"], p["sinq"],
      p["conv_w"], p["conv_b"], p["conv_ln_g"], p["conv_ln_b"], p["sgu_ln_g"], p["sgu_ln_b"],
      p["sgu_w"], p["sgu_bias"])


def _attn_kernel(qT_ref, k_ref, vT_ref, o_ref, *, n_lat, n_q_lat):
    qi = pl.program_id(2)
    qT = qT_ref[0]
    tq = qT.shape[1]

    def step(off, size, carry):
        m, l, acc = carry
        kt = k_ref[0, pl.ds(off, size), :]
        s = _dot(kt, qT)
        m_new = jnp.maximum(m, jnp.max(s, axis=0, keepdims=True))
        alpha = jnp.exp2(m - m_new)
        pr = jnp.exp2(s - m_new)
        l = alpha * l + jnp.sum(pr, axis=0, keepdims=True)
        vt = vT_ref[0, :, pl.ds(off, size)]
        acc = alpha * acc + _dot(vt, pr.astype(BF16))
        return m_new, l, acc

    init = (jnp.full((1, tq), -jnp.inf, F32), jnp.zeros((1, tq), F32), jnp.zeros((VDIM, tq), F32))
    n_steps = jnp.where(qi < n_q_lat, n_lat // KV_TILE, 0)
    carry = lax.fori_loop(
        0, n_steps, lambda j, c: step(pl.multiple_of(j * KV_TILE, KV_TILE), KV_TILE, c), init)
    m, l, acc = step(n_lat, k_ref.shape[1] - n_lat, carry)
    o_ref[0] = (acc / l).astype(BF16)


def _attn_call(qT, k, vT, n_lat, n_q_tiles):
    bsz, _, ltot = qT.shape
    tq = Q_TILE
    n_q_lat = n_lat // tq
    return pl.pallas_call(
        functools.partial(_attn_kernel, n_lat=n_lat, n_q_lat=n_q_lat),
        grid=(bsz, HEADS, n_q_tiles),
        in_specs=[
            pl.BlockSpec((1, QK_PAD, tq), lambda b, h, q: (b, h, q)),
            pl.BlockSpec((1, ltot, QK_PAD), lambda b, h, q: (b, 0, h)),
            pl.BlockSpec((1, VDIM, ltot), lambda b, h, q: (b, h, 0)),
        ],
        out_specs=pl.BlockSpec((1, VDIM, tq), lambda b, h, q: (b, h, q)),
        out_shape=jax.ShapeDtypeStruct((bsz, HEADS * VDIM, n_q_tiles * tq), BF16),
        compiler_params=_cparams(("arbitrary", "arbitrary", "arbitrary")),
        name="attn",
    )(qT, k, vT)


def _post_kernel(x_ref, aT_ref, yc_ref, ys_ref, mod_ref, n2g_ref, wout_ref, w1_ref, w2_ref, fg_ref,
                 o_ref, *, final):
    d = x_ref.shape[2]
    d_ff = w1_ref.shape[2]
    n_attn = aT_ref.shape[1]
    g1 = mod_ref[0, 0, 2:3, :]
    sh2 = mod_ref[0, 0, 3:4, :]
    sc2 = mod_ref[0, 0, 4:5, :]
    g2 = mod_ref[0, 0, 5:6, :]
    y = _dot_tn(aT_ref[0], wout_ref[0, 0:n_attn, :])
    y = y + _dot(yc_ref[0], wout_ref[0, n_attn:n_attn + CONV_CH, :])
    y = y + _dot(ys_ref[0], wout_ref[0, n_attn + CONV_CH:d, :])
    x1 = x_ref[0] + g1 * y
    h = (_rms(x1, n2g_ref[0]) * (1.0 + sc2) + sh2).astype(BF16)
    ff = jnp.zeros(x1.shape, F32)
    for c in range(d_ff // FF_CHUNK):
        f = jnp.maximum(_dot(h, w1_ref[0, :, c * FF_CHUNK:(c + 1) * FF_CHUNK]), 0.0)
        ff = ff + _dot((f * f).astype(BF16), w2_ref[0, c * FF_CHUNK:(c + 1) * FF_CHUNK, :])
    x2 = x1 + g2 * ff
    if final:
        x2 = _rms(x2, fg_ref[...])
    o_ref[0] = x2


def _post_call(layer, xc, aT, yc, ys, modsel, p, n_lat_tiles, n_out_tiles, final):
    bsz, _, d = xc.shape
    tt = TOK_TILE
    d_ff = p["w_ff1"].shape[2]

    def lay(shape):
        nd = len(shape)
        return pl.BlockSpec((1,) + shape, lambda b, i: (layer,) + (0,) * nd,
                            pipeline_mode=pl.Buffered(1))

    return pl.pallas_call(
        functools.partial(_post_kernel, final=final),
        grid=(bsz, n_out_tiles),
        in_specs=[
            pl.BlockSpec((1, tt, d), lambda b, i: (b, i, 0)),
            pl.BlockSpec((1, HEADS * VDIM, tt), lambda b, i: (b, 0, i)),
            pl.BlockSpec((1, tt, CONV_CH), lambda b, i: (b, i, 0)),
            pl.BlockSpec((1, tt, SGU_CH), lambda b, i: (b, i, 0)),
            pl.BlockSpec((1, 1, 8, d), lambda b, i: (b, i // n_lat_tiles, 0, 0)),
            pl.BlockSpec((1, 1, d), lambda b, i: (layer, 0, 0)),
            lay((d, d)),
            lay((d, d_ff)),
            lay((d_ff, d)),
            pl.BlockSpec((1, d), lambda b, i: (0, 0)),
        ],
        out_specs=pl.BlockSpec((1, tt, d), lambda b, i: (b, i, 0)),
        out_shape=jax.ShapeDtypeStruct((bsz, n_out_tiles * tt, d), F32),
        compiler_params=_cparams(("arbitrary", "arbitrary")),
        name="post",
    )(xc, aT, yc, ys, modsel, p["norm2_g"], p["w_out"], p["w_ff1"], p["w_ff2"], p["final_g"])


def _rotate_half_cols(w):
    n = ROPE // 4
    return jnp.concatenate([-w[..., n:2 * n], w[..., 0:n], -w[..., 3 * n:4 * n], w[..., 2 * n:3 * n]], -1)


def _prepare(n_lat, n_ctx, norm1_g, norm2_g, w_in, q_norm_g, w_uq, kv_norm_g, w_ukv, conv_w, conv_b,
             conv_ln_g, conv_ln_b, sgu_ln_g, sgu_ln_b, sgu_w, sgu_b, w_out, w_ff1, w_ff2, final_g):
    n_layers, d, _ = w_in.shape
    o_q, o_kv, o_kr = 0, Q_RANK, Q_RANK + KV_RANK
    o_conv = o_kr + ROPE
    o_sgu = o_conv + 2 * CONV_CH
    w_kr = w_in[:, :, o_kr:o_kr + ROPE]

    def placed(w):
        return jnp.pad(w, ((0, 0), (0, 0), (NOPE, QK_PAD - NOPE - ROPE)))

    w_in_x = jnp.concatenate([
        w_in[:, :, o_q:o_q + Q_RANK], w_in[:, :, o_kv:o_kv + KV_RANK],
        w_in[:, :, o_conv:o_conv + 2 * CONV_CH], w_in[:, :, o_sgu:o_sgu + 2 * SGU_CH],
        placed(w_kr), placed(_rotate_half_cols(w_kr))], -1).astype(BF16)

    wq = w_uq.reshape(n_layers, Q_RANK, HEADS, NOPE + ROPE)
    wq_pad = jnp.pad(wq, ((0, 0), (0, 0), (0, 0), (0, QK_PAD - NOPE - ROPE)))
    wqT = wq_pad.reshape(n_layers, Q_RANK, HEADS * QK_PAD).transpose(0, 2, 1).astype(BF16)
    wqr = _rotate_half_cols(wq[..., NOPE:])
    wqrT = wqr.reshape(n_layers, Q_RANK, HEADS * ROPE).transpose(0, 2, 1).astype(BF16)

    wkv = w_ukv.reshape(n_layers, KV_RANK, HEADS, NOPE + VDIM)
    wk = jnp.pad(wkv[..., :NOPE], ((0, 0), (0, 0), (0, 0), (0, QK_PAD - NOPE)))
    wk = wk.reshape(n_layers, KV_RANK, HEADS * QK_PAD).astype(BF16)
    wvT = wkv[..., NOPE:].reshape(n_layers, KV_RANK, HEADS * VDIM).transpose(0, 2, 1).astype(BF16)

    t = jnp.arange(n_lat, dtype=jnp.int32)
    n = ROPE // 4
    inv = 1.0 / (ROPE_BASE ** (jnp.arange(n, dtype=F32) / n))
    ang_r = (t // GRID_W).astype(F32)[:, None] * inv
    ang_c = (t % GRID_W).astype(F32)[:, None] * inv
    cos32 = jnp.concatenate([jnp.cos(ang_r)] * 2 + [jnp.cos(ang_c)] * 2, -1)
    sin32 = jnp.concatenate([jnp.sin(ang_r)] * 2 + [jnp.sin(ang_c)] * 2, -1)
    cos32 = jnp.concatenate([cos32, jnp.ones((n_ctx, ROPE), F32)], 0)
    sin32 = jnp.concatenate([sin32, jnp.zeros((n_ctx, ROPE), F32)], 0)
    pad = ((0, 0), (NOPE, QK_PAD - NOPE - ROPE))

    sgu_bias = jnp.broadcast_to(jnp.swapaxes(sgu_b, 1, 2)[:, :, :, None],
                                (n_layers, CHUNK, SGU_HEADS, SGU_HEAD_DIM)).reshape(n_layers, CHUNK, SGU_CH)

    def row(a):
        return a.reshape(n_layers, 1, a.shape[-1])

    return dict(
        norm1_g=row(norm1_g), norm2_g=row(norm2_g), w_in=w_in_x, q_norm_g=row(q_norm_g), wqT=wqT, wqrT=wqrT,
        kv_norm_g=row(kv_norm_g), wk=wk, wvT=wvT,
        cosk=jnp.pad(cos32, pad), sink=jnp.pad(sin32, pad), cosq=cos32.T, sinq=sin32.T,
        conv_w=conv_w, conv_b=row(conv_b), conv_ln_g=row(conv_ln_g), conv_ln_b=row(conv_ln_b),
        sgu_ln_g=row(sgu_ln_g), sgu_ln_b=row(sgu_ln_b), sgu_w=sgu_w.astype(BF16), sgu_bias=sgu_bias,
        w_out=w_out.astype(BF16), w_ff1=w_ff1.astype(BF16), w_ff2=w_ff2.astype(BF16),
        final_g=final_g.reshape(1, -1))


def kernel(x, c, ctx, c_ctx, ada_w, ada_b, norm1_g, norm2_g, w_in, q_norm_g, w_uq, kv_norm_g, w_ukv,
           conv_w, conv_b, conv_ln_g, conv_ln_b, sgu_ln_g, sgu_ln_b, sgu_w, sgu_b, w_out, w_ff1, w_ff2,
           final_g):
    bsz, n_lat, d = x.shape
    n_ctx = ctx.shape[1]
    n_layers = w_in.shape[0]
    assert n_ctx == TOK_TILE and n_lat % KV_TILE == 0 and n_lat % GRID_W == 0
    n_lat_tiles = n_lat // TOK_TILE
    n_tiles = n_lat_tiles + 1

    p = _prepare(n_lat, n_ctx, norm1_g, norm2_g, w_in, q_norm_g, w_uq, kv_norm_g, w_ukv, conv_w, conv_b,
                 conv_ln_g, conv_ln_b, sgu_ln_g, sgu_ln_b, sgu_w, sgu_b, w_out, w_ff1, w_ff2, final_g)

    rows = -(-(bsz + 1) // 8) * 8
    cvec = jnp.concatenate([c, c_ctx[None, :], jnp.zeros((rows - bsz - 1, d), F32)], 0)
    mods = _ada_call(cvec, ada_w, ada_b).reshape(n_layers, rows, 6, d)
    m_lat = mods[:, :bsz]
    m_ctx = jnp.broadcast_to(mods[:, bsz][:, None], m_lat.shape)
    modsel = jnp.pad(jnp.stack([m_lat, m_ctx], 2), ((0, 0), (0, 0), (0, 0), (0, 2), (0, 0)))

    xc = jnp.concatenate([x, ctx], 1)
    for layer in range(n_layers):
        last = layer == n_layers - 1
        n_act = n_lat_tiles if last else n_tiles
        qT, k, vT, yc, ys = _pre_call(layer, xc, modsel[layer], p, n_lat_tiles)
        aT = _attn_call(qT, k, vT, n_lat, n_act)
        xc = _post_call(layer, xc, aT, yc, ys, modsel[layer], p, n_lat_tiles, n_act, last)
    return xc
```

```python
import functools
import math

import jax
import jax.numpy as jnp
from jax import lax
from jax.experimental import pallas as pl
from jax.experimental.pallas import tpu as pltpu

F32 = jnp.float32
BF16 = jnp.bfloat16

GRID_W = 64
HEADS = 8
NOPE = 64
ROPE = 32
VDIM = 64
QK_PAD = 128
Q_RANK = 384
KV_RANK = 256
CONV_CH = 256
CONV_WIDTH = 31
CONV_HALO = 16
SGU_HEADS = 4
SGU_HEAD_DIM = 64
SGU_CH = 256
CHUNK = 128
EPS = 1e-6
ROPE_BASE = 10000.0
ATTN_SCALE = (NOPE + ROPE) ** -0.5
LOG2E = math.log2(math.e)

TOK_TILE = 256
Q_TILE = 1024
KV_TILE = 256
FF_CHUNK = 1024
ADA_COLS = 1536
VMEM_LIMIT_BYTES = 56 * 1024 * 1024

ZC_Q = 0
ZC_KV = ZC_Q + Q_RANK
ZC_CONV = ZC_KV + KV_RANK
ZC_SGU = ZC_CONV + 2 * CONV_CH
ZC_KR = ZC_SGU + 2 * SGU_CH
ZC_KRR = ZC_KR + QK_PAD
Z_COLS = ZC_KRR + QK_PAD


def _cparams(sem):
    return pltpu.CompilerParams(dimension_semantics=sem, vmem_limit_bytes=VMEM_LIMIT_BYTES)


def _sigmoid(v):
    return 1.0 / (1.0 + jnp.exp(-v))


def _layernorm(v, g, b):
    mu = jnp.mean(v, -1, keepdims=True)
    d = v - mu
    var = jnp.mean(d * d, -1, keepdims=True)
    return d * lax.rsqrt(var + EPS) * g + b


def _rms(v, g):
    return v * lax.rsqrt(jnp.mean(v * v, -1, keepdims=True) + EPS) * g


def _dot(a, b):
    return jnp.dot(a, b, preferred_element_type=F32)


def _dot_nt(a, b):
    return lax.dot_general(a, b, (((1,), (1,)), ((), ())), preferred_element_type=F32)


def _dot_tn(a, b):
    return lax.dot_general(a, b, (((0,), (0,)), ((), ())), preferred_element_type=F32)


def _ada_kernel(c_ref, w_ref, b_ref, o_ref):
    c = c_ref[...]
    s = (c * _sigmoid(c)).astype(BF16)
    o_ref[0] = _dot(s, w_ref[0].astype(BF16)) + b_ref[0]


def _ada_call(cvec, ada_w, ada_b):
    n_layers, d, n6 = ada_w.shape
    rows = cvec.shape[0]
    return pl.pallas_call(
        _ada_kernel,
        grid=(n_layers, n6 // ADA_COLS),
        in_specs=[
            pl.BlockSpec((rows, d), lambda l, j: (0, 0)),
            pl.BlockSpec((1, d, ADA_COLS), lambda l, j: (l, 0, j)),
            pl.BlockSpec((1, 1, ADA_COLS), lambda l, j: (l, 0, j)),
        ],
        out_specs=pl.BlockSpec((1, rows, ADA_COLS), lambda l, j: (l, 0, j)),
        out_shape=jax.ShapeDtypeStruct((n_layers, rows, n6), F32),
        compiler_params=_cparams(("arbitrary", "arbitrary")),
        name="adaln",
    )(cvec, ada_w, ada_b.reshape(n_layers, 1, n6))


def _pre_kernel(x_ref, xp_ref, xn_ref, mod_ref, n1g_ref, win_ref, qg_ref, wqT_ref, wqrT_ref,
                kvg_ref, wk_ref, wvT_ref, cosk_ref, sink_ref, cosq_ref, sinq_ref,
                convw_ref, convb_ref, clng_ref, clnb_ref, slng_ref, slnb_ref, sguw_ref, sgub_ref,
                qT_out, k_out, vT_out, yc_out, ys_out, ybuf, *, n_tiles):
    i = pl.program_id(1)
    tt = x_ref.shape[1]
    g1 = n1g_ref[0]
    sh1 = mod_ref[0, 0, 0:1, :]
    sc1 = mod_ref[0, 0, 1:2, :]

    def normmod(xt):
        return (_rms(xt, g1) * (1.0 + sc1) + sh1).astype(BF16)

    z = _dot(normmod(x_ref[0]), win_ref[0])

    qn = _rms(z[:, ZC_Q:ZC_Q + Q_RANK], qg_ref[0]).astype(BF16)
    qT = _dot_nt(wqT_ref[0], qn)
    qrT = _dot_nt(wqrT_ref[0], qn)
    cq = cosq_ref[...]
    sq = sinq_ref[...]
    qscale = ATTN_SCALE * LOG2E
    for h in range(HEADS):
        r0 = h * QK_PAD
        qT_out[0, r0:r0 + NOPE, :] = (qT[r0:r0 + NOPE, :] * qscale).astype(BF16)
        rot = qT[r0 + NOPE:r0 + NOPE + ROPE, :] * cq + qrT[h * ROPE:(h + 1) * ROPE, :] * sq
        qT_out[0, r0 + NOPE:r0 + NOPE + ROPE, :] = (rot * qscale).astype(BF16)
        qT_out[0, r0 + NOPE + ROPE:r0 + QK_PAD, :] = jnp.zeros((QK_PAD - NOPE - ROPE, tt), BF16)

    kvn = _rms(z[:, ZC_KV:ZC_KV + KV_RANK], kvg_ref[0]).astype(BF16)
    kext = _dot(kvn, wk_ref[0])
    krope = z[:, ZC_KR:ZC_KR + QK_PAD] * cosk_ref[...] + z[:, ZC_KRR:ZC_KRR + QK_PAD] * sink_ref[...]
    for h in range(HEADS):
        c0 = h * QK_PAD
        k_out[0, :, c0:c0 + QK_PAD] = (kext[:, c0:c0 + QK_PAD] + krope).astype(BF16)
    vT_out[0] = _dot_nt(wvT_ref[0], kvn).astype(BF16)

    def glu(zc):
        return zc[:, :CONV_CH] * _sigmoid(zc[:, CONV_CH:])

    wconv_in = win_ref[0, :, ZC_CONV:ZC_CONV + 2 * CONV_CH]
    first = jnp.logical_or(i == 0, i == n_tiles - 1)
    last = i >= n_tiles - 2
    yp = glu(_dot(normmod(xp_ref[0]), wconv_in))
    yn = glu(_dot(normmod(xn_ref[0]), wconv_in))
    ybuf[0:CONV_HALO, :] = jnp.where(first, 0.0, yp)
    ybuf[CONV_HALO:CONV_HALO + tt, :] = glu(z[:, ZC_CONV:ZC_CONV + 2 * CONV_CH])
    ybuf[CONV_HALO + tt:, :] = jnp.where(last, 0.0, yn)
    cw = convw_ref[0]
    acc = jnp.zeros((tt, CONV_CH), F32)
    for k in range(CONV_WIDTH):
        acc = acc + ybuf[k + 1:k + 1 + tt, :] * cw[k:k + 1, :]
    yc = _layernorm(acc + convb_ref[0], clng_ref[0], clnb_ref[0])
    yc_out[0] = (yc * _sigmoid(yc)).astype(BF16)

    zs = z[:, ZC_SGU:ZC_SGU + 2 * SGU_CH]
    zg = 0.5 * zs * (1.0 + jnp.tanh(math.sqrt(2.0 / math.pi) * (zs + 0.044715 * (zs * zs * zs))))
    u = zg[:, :SGU_CH]
    v = _layernorm(zg[:, SGU_CH:], slng_ref[0], slnb_ref[0]).astype(BF16)
    lane = lax.broadcasted_iota(jnp.int32, (CHUNK, SGU_CH), 1)
    bias = sgub_ref[0]
    for c in range(tt // CHUNK):
        vc = v[c * CHUNK:(c + 1) * CHUNK, :]
        mixed = _dot(sguw_ref[0, 0], vc)
        for h in range(1, SGU_HEADS):
            mixed = jnp.where(lane >= h * SGU_HEAD_DIM, _dot(sguw_ref[0, h], vc), mixed)
        ys_out[0, c * CHUNK:(c + 1) * CHUNK, :] = (u[c * CHUNK:(c + 1) * CHUNK, :] * (mixed + bias)).astype(BF16)


def _pre_call(layer, xc, modsel, p, n_lat_tiles):
    bsz, ltot, d = xc.shape
    tt = TOK_TILE
    n_tiles = ltot // tt
    hb = tt // CONV_HALO
    n_hblocks = ltot // CONV_HALO

    def lay(shape):
        nd = len(shape)
        return pl.BlockSpec((1,) + shape, lambda b, i: (layer,) + (0,) * nd)

    in_specs = [
        pl.BlockSpec((1, tt, d), lambda b, i: (b, i, 0)),
        pl.BlockSpec((1, CONV_HALO, d), lambda b, i: (b, jnp.maximum(i * hb - 1, 0), 0)),
        pl.BlockSpec((1, CONV_HALO, d), lambda b, i: (b, jnp.minimum((i + 1) * hb, n_hblocks - 1), 0)),
        pl.BlockSpec((1, 1, 8, d), lambda b, i: (b, i // n_lat_tiles, 0, 0)),
        lay((1, d)),
        lay((d, Z_COLS)),
        lay((1, Q_RANK)),
        lay((HEADS * QK_PAD, Q_RANK)),
        lay((HEADS * ROPE, Q_RANK)),
        lay((1, KV_RANK)),
        lay((KV_RANK, HEADS * QK_PAD)),
        lay((HEADS * VDIM, KV_RANK)),
        pl.BlockSpec((tt, QK_PAD), lambda b, i: (i, 0)),
        pl.BlockSpec((tt, QK_PAD), lambda b, i: (i, 0)),
        pl.BlockSpec((ROPE, tt), lambda b, i: (0, i)),
        pl.BlockSpec((ROPE, tt), lambda b, i: (0, i)),
        lay((CONV_WIDTH, CONV_CH)),
        lay((1, CONV_CH)),
        lay((1, CONV_CH)),
        lay((1, CONV_CH)),
        lay((1, SGU_CH)),
        lay((1, SGU_CH)),
        lay((SGU_HEADS, CHUNK, CHUNK)),
        lay((CHUNK, SGU_CH)),
    ]
    out_specs = [
        pl.BlockSpec((1, HEADS * QK_PAD, tt), lambda b, i: (b, 0, i)),
        pl.BlockSpec((1, tt, HEADS * QK_PAD), lambda b, i: (b, i, 0)),
        pl.BlockSpec((1, HEADS * VDIM, tt), lambda b, i: (b, 0, i)),
        pl.BlockSpec((1, tt, CONV_CH), lambda b, i: (b, i, 0)),
        pl.BlockSpec((1, tt, SGU_CH), lambda b, i: (b, i, 0)),
    ]
    out_shape = [
        jax.ShapeDtypeStruct((bsz, HEADS * QK_PAD, ltot), BF16),
        jax.ShapeDtypeStruct((bsz, ltot, HEADS * QK_PAD), BF16),
        jax.ShapeDtypeStruct((bsz, HEADS * VDIM, ltot), BF16),
        jax.ShapeDtypeStruct((bsz, ltot, CONV_CH), BF16),
        jax.ShapeDtypeStruct((bsz, ltot, SGU_CH), BF16),
    ]
    return pl.pallas_call(
        functools.partial(_pre_kernel, n_tiles=n_tiles),
        grid=(bsz, n_tiles),
        in_specs=in_specs,
        out_specs=out_specs,
        out_shape=out_shape,
        scratch_shapes=[pltpu.VMEM((tt + 2 * CONV_HALO, CONV_CH), F32)],
        compiler_params=_cparams(("arbitrary", "arbitrary")),
        name="pre",
    )(xc, xc, xc, modsel, p["norm1_g"], p["w_in"], p["q_norm_g"], p["wqT"], p["wqrT"],
      p["kv_norm_g"], p["wk"], p["wvT"], p["cosk"], p["sink"], p["cosq"], p["sinq"],
      p["conv_w"], p["conv_b"], p["conv_ln_g"], p["conv_ln_b"], p["sgu_ln_g"], p["sgu_ln_b"],
      p["sgu_w"], p["sgu_bias"])


def _softmax_step(s, vt, carry):
    m, l, acc = carry
    m_new = jnp.maximum(m, jnp.max(s, axis=0, keepdims=True))
    alpha = jnp.exp2(m - m_new)
    pr = jnp.exp2(s - m_new)
    l = alpha * l + jnp.sum(pr, axis=0, keepdims=True)
    acc = alpha * acc + _dot(vt, pr.astype(BF16))
    return m_new, l, acc


def _softmax_init(tq):
    return (jnp.full((1, tq), -jnp.inf, F32), jnp.zeros((1, tq), F32), jnp.zeros((VDIM, tq), F32))


def _attn_lat_kernel(qT_ref, k_ref, vT_ref, o_ref, s_scr, acc_scr):
    qT = qT_ref[0]
    tq = qT.shape[1]
    n_tiles = k_ref.shape[1] // KV_TILE

    def scores(t, slot):
        off = pl.multiple_of(t * KV_TILE, KV_TILE)
        s_scr[slot] = _dot(k_ref[0, pl.ds(off, KV_TILE), :], qT)

    def consume(t, slot, ml):
        m, l = ml
        off = pl.multiple_of(t * KV_TILE, KV_TILE)
        s = s_scr[slot]
        m_new = jnp.maximum(m, jnp.max(s, axis=0, keepdims=True))
        alpha = jnp.exp2(m - m_new)
        pr = jnp.exp2(s - m_new)
        l = alpha * l + jnp.sum(pr, axis=0, keepdims=True)
        acc_scr[...] = alpha * acc_scr[...] + _dot(vT_ref[0, :, pl.ds(off, KV_TILE)], pr.astype(BF16))
        return m_new, l

    def pair(jj, ml):
        t = 2 * jj
        scores(t + 1, 1)
        ml = consume(t, 0, ml)
        scores(t + 2, 0)
        return consume(t + 1, 1, ml)

    acc_scr[...] = jnp.zeros(acc_scr.shape, F32)
    scores(0, 0)
    ml = lax.fori_loop(0, (n_tiles - 1) // 2, pair,
                       (jnp.full((1, tq), -jnp.inf, F32), jnp.zeros((1, tq), F32)))
    m, l = consume(n_tiles - 1, 0, ml)
    o_ref[0] = (acc_scr[...] / l).astype(BF16)


def _attn_ctx_kernel(qT_ref, k_ref, vT_ref, a_hbm_ref, o_ref):
    del a_hbm_ref
    s = _dot(k_ref[0], qT_ref[0])
    m, l, acc = _softmax_step(s, vT_ref[0], _softmax_init(s.shape[1]))
    o_ref[0] = (acc / l).astype(BF16)


def _attn_call(qT, k, vT, n_lat, with_ctx):
    bsz, _, ltot = qT.shape
    tq = Q_TILE
    assert ltot % KV_TILE == 0 and (ltot // KV_TILE) % 2 == 1
    aT = pl.pallas_call(
        _attn_lat_kernel,
        grid=(bsz, HEADS, n_lat // tq),
        in_specs=[
            pl.BlockSpec((1, QK_PAD, tq), lambda b, h, q: (b, h, q)),
            pl.BlockSpec((1, ltot, QK_PAD), lambda b, h, q: (b, 0, h)),
            pl.BlockSpec((1, VDIM, ltot), lambda b, h, q: (b, h, 0)),
        ],
        out_specs=pl.BlockSpec((1, VDIM, tq), lambda b, h, q: (b, h, q)),
        out_shape=jax.ShapeDtypeStruct((bsz, HEADS * VDIM, ltot), BF16),
        scratch_shapes=[pltpu.VMEM((2, KV_TILE, tq), F32), pltpu.VMEM((VDIM, tq), F32)],
        compiler_params=_cparams(("arbitrary", "arbitrary", "arbitrary")),
        name="attn_lat",
    )(qT, k, vT)
    if not with_ctx:
        return aT
    n_ctx = ltot - n_lat
    cb = n_lat // n_ctx
    return pl.pallas_call(
        _attn_ctx_kernel,
        grid=(bsz, HEADS),
        in_specs=[
            pl.BlockSpec((1, QK_PAD, n_ctx), lambda b, h: (b, h, cb)),
            pl.BlockSpec((1, n_ctx, QK_PAD), lambda b, h: (b, cb, h)),
            pl.BlockSpec((1, VDIM, n_ctx), lambda b, h: (b, h, cb)),
            pl.BlockSpec(memory_space=pl.ANY),
        ],
        out_specs=pl.BlockSpec((1, VDIM, n_ctx), lambda b, h: (b, h, cb)),
        out_shape=jax.ShapeDtypeStruct(aT.shape, aT.dtype),
        input_output_aliases={3: 0},
        compiler_params=_cparams(("arbitrary", "arbitrary")),
        name="attn_ctx",
    )(qT, k, vT, aT)


def _post_kernel(x_ref, aT_ref, yc_ref, ys_ref, mod_ref, n2g_ref, wout_ref, w1_ref, w2_ref, fg_ref,
                 o_ref, *, final):
    d = x_ref.shape[2]
    d_ff = w1_ref.shape[2]
    n_attn = aT_ref.shape[1]
    g1 = mod_ref[0, 0, 2:3, :]
    sh2 = mod_ref[0, 0, 3:4, :]
    sc2 = mod_ref[0, 0, 4:5, :]
    g2 = mod_ref[0, 0, 5:6, :]
    y = _dot_tn(aT_ref[0], wout_ref[0, 0:n_attn, :])
    y = y + _dot(yc_ref[0], wout_ref[0, n_attn:n_attn + CONV_CH, :])
    y = y + _dot(ys_ref[0], wout_ref[0, n_attn + CONV_CH:d, :])
    x1 = x_ref[0] + g1 * y
    h = (_rms(x1, n2g_ref[0]) * (1.0 + sc2) + sh2).astype(BF16)
    ff = jnp.zeros(x1.shape, F32)
    for c in range(d_ff // FF_CHUNK):
        f = jnp.maximum(_dot(h, w1_ref[0, :, c * FF_CHUNK:(c + 1) * FF_CHUNK]), 0.0)
        ff = ff + _dot((f * f).astype(BF16), w2_ref[0, c * FF_CHUNK:(c + 1) * FF_CHUNK, :])
    x2 = x1 + g2 * ff
    if final:
        x2 = _rms(x2, fg_ref[...])
    o_ref[0] = x2


def _post_call(layer, xc, aT, yc, ys, modsel, p, n_lat_tiles, n_out_tiles, final):
    bsz, _, d = xc.shape
    tt = TOK_TILE
    d_ff = p["w_ff1"].shape[2]

    def lay(shape):
        nd = len(shape)
        return pl.BlockSpec((1,) + shape, lambda b, i: (layer,) + (0,) * nd,
                            pipeline_mode=pl.Buffered(1))

    return pl.pallas_call(
        functools.partial(_post_kernel, final=final),
        grid=(bsz, n_out_tiles),
        in_specs=[
            pl.BlockSpec((1, tt, d), lambda b, i: (b, i, 0)),
            pl.BlockSpec((1, HEADS * VDIM, tt), lambda b, i: (b, 0, i)),
            pl.BlockSpec((1, tt, CONV_CH), lambda b, i: (b, i, 0)),
            pl.BlockSpec((1, tt, SGU_CH), lambda b, i: (b, i, 0)),
            pl.BlockSpec((1, 1, 8, d), lambda b, i: (b, i // n_lat_tiles, 0, 0)),
            pl.BlockSpec((1, 1, d), lambda b, i: (layer, 0, 0)),
            lay((d, d)),
            lay((d, d_ff)),
            lay((d_ff, d)),
            pl.BlockSpec((1, d), lambda b, i: (0, 0)),
        ],
        out_specs=pl.BlockSpec((1, tt, d), lambda b, i: (b, i, 0)),
        out_shape=jax.ShapeDtypeStruct((bsz, n_out_tiles * tt, d), F32),
        compiler_params=_cparams(("arbitrary", "arbitrary")),
        name="post",
    )(xc, aT, yc, ys, modsel, p["norm2_g"], p["w_out"], p["w_ff1"], p["w_ff2"], p["final_g"])


def _rotate_half_cols(w):
    n = ROPE // 4
    return jnp.concatenate([-w[..., n:2 * n], w[..., 0:n], -w[..., 3 * n:4 * n], w[..., 2 * n:3 * n]], -1)


def _prepare(n_lat, n_ctx, norm1_g, norm2_g, w_in, q_norm_g, w_uq, kv_norm_g, w_ukv, conv_w, conv_b,
             conv_ln_g, conv_ln_b, sgu_ln_g, sgu_ln_b, sgu_w, sgu_b, w_out, w_ff1, w_ff2, final_g):
    n_layers, d, _ = w_in.shape
    o_q, o_kv, o_kr = 0, Q_RANK, Q_RANK + KV_RANK
    o_conv = o_kr + ROPE
    o_sgu = o_conv + 2 * CONV_CH
    w_kr = w_in[:, :, o_kr:o_kr + ROPE]

    def placed(w):
        return jnp.pad(w, ((0, 0), (0, 0), (NOPE, QK_PAD - NOPE - ROPE)))

    w_in_x = jnp.concatenate([
        w_in[:, :, o_q:o_q + Q_RANK], w_in[:, :, o_kv:o_kv + KV_RANK],
        w_in[:, :, o_conv:o_conv + 2 * CONV_CH], w_in[:, :, o_sgu:o_sgu + 2 * SGU_CH],
        placed(w_kr), placed(_rotate_half_cols(w_kr))], -1).astype(BF16)

    wq = w_uq.reshape(n_layers, Q_RANK, HEADS, NOPE + ROPE)
    wq_pad = jnp.pad(wq, ((0, 0), (0, 0), (0, 0), (0, QK_PAD - NOPE - ROPE)))
    wqT = wq_pad.reshape(n_layers, Q_RANK, HEADS * QK_PAD).transpose(0, 2, 1).astype(BF16)
    wqr = _rotate_half_cols(wq[..., NOPE:])
    wqrT = wqr.reshape(n_layers, Q_RANK, HEADS * ROPE).transpose(0, 2, 1).astype(BF16)

    wkv = w_ukv.reshape(n_layers, KV_RANK, HEADS, NOPE + VDIM)
    wk = jnp.pad(wkv[..., :NOPE], ((0, 0), (0, 0), (0, 0), (0, QK_PAD - NOPE)))
    wk = wk.reshape(n_layers, KV_RANK, HEADS * QK_PAD).astype(BF16)
    wvT = wkv[..., NOPE:].reshape(n_layers, KV_RANK, HEADS * VDIM).transpose(0, 2, 1).astype(BF16)

    t = jnp.arange(n_lat, dtype=jnp.int32)
    n = ROPE // 4
    inv = 1.0 / (ROPE_BASE ** (jnp.arange(n, dtype=F32) / n))
    ang_r = (t // GRID_W).astype(F32)[:, None] * inv
    ang_c = (t % GRID_W).astype(F32)[:, None] * inv
    cos32 = jnp.concatenate([jnp.cos(ang_r)] * 2 + [jnp.cos(ang_c)] * 2, -1)
    sin32 = jnp.concatenate([jnp.sin(ang_r)] * 2 + [jnp.sin(ang_c)] * 2, -1)
    cos32 = jnp.concatenate([cos32, jnp.ones((n_ctx, ROPE), F32)], 0)
    sin32 = jnp.concatenate([sin32, jnp.zeros((n_ctx, ROPE), F32)], 0)
    pad = ((0, 0), (NOPE, QK_PAD - NOPE - ROPE))

    sgu_bias = jnp.broadcast_to(jnp.swapaxes(sgu_b, 1, 2)[:, :, :, None],
                                (n_layers, CHUNK, SGU_HEADS, SGU_HEAD_DIM)).reshape(n_layers, CHUNK, SGU_CH)

    def row(a):
        return a.reshape(n_layers, 1, a.shape[-1])

    return dict(
        norm1_g=row(norm1_g), norm2_g=row(norm2_g), w_in=w_in_x, q_norm_g=row(q_norm_g), wqT=wqT, wqrT=wqrT,
        kv_norm_g=row(kv_norm_g), wk=wk, wvT=wvT,
        cosk=jnp.pad(cos32, pad), sink=jnp.pad(sin32, pad), cosq=cos32.T, sinq=sin32.T,
        conv_w=conv_w, conv_b=row(conv_b), conv_ln_g=row(conv_ln_g), conv_ln_b=row(conv_ln_b),
        sgu_ln_g=row(sgu_ln_g), sgu_ln_b=row(sgu_ln_b), sgu_w=sgu_w.astype(BF16), sgu_bias=sgu_bias,
        w_out=w_out.astype(BF16), w_ff1=w_ff1.astype(BF16), w_ff2=w_ff2.astype(BF16),
        final_g=final_g.reshape(1, -1))


def kernel(x, c, ctx, c_ctx, ada_w, ada_b, norm1_g, norm2_g, w_in, q_norm_g, w_uq, kv_norm_g, w_ukv,
           conv_w, conv_b, conv_ln_g, conv_ln_b, sgu_ln_g, sgu_ln_b, sgu_w, sgu_b, w_out, w_ff1, w_ff2,
           final_g):
    bsz, n_lat, d = x.shape
    n_ctx = ctx.shape[1]
    n_layers = w_in.shape[0]
    assert n_ctx == TOK_TILE and n_lat % KV_TILE == 0 and n_lat % Q_TILE == 0 and n_lat % GRID_W == 0
    n_lat_tiles = n_lat // TOK_TILE
    n_tiles = n_lat_tiles + 1

    p = _prepare(n_lat, n_ctx, norm1_g, norm2_g, w_in, q_norm_g, w_uq, kv_norm_g, w_ukv, conv_w, conv_b,
                 conv_ln_g, conv_ln_b, sgu_ln_g, sgu_ln_b, sgu_w, sgu_b, w_out, w_ff1, w_ff2, final_g)

    rows = -(-(bsz + 1) // 8) * 8
    cvec = jnp.concatenate([c, c_ctx[None, :], jnp.zeros((rows - bsz - 1, d), F32)], 0)
    mods = _ada_call(cvec, ada_w, ada_b).reshape(n_layers, rows, 6, d)
    m_lat = mods[:, :bsz]
    m_ctx = jnp.broadcast_to(mods[:, bsz][:, None], m_lat.shape)
    modsel = jnp.pad(jnp.stack([m_lat, m_ctx], 2), ((0, 0), (0, 0), (0, 0), (0, 2), (0, 0)))

    xc = jnp.concatenate([x, ctx], 1)
    for layer in range(n_layers):
        last = layer == n_layers - 1
        n_act = n_lat_tiles if last else n_tiles
        qT, k, vT, yc, ys = _pre_call(layer, xc, modsel[layer], p, n_lat_tiles)
        aT = _attn_call(qT, k, vT, n_lat, not last)
        xc = _post_call(layer, xc, aT, yc, ys, modsel[layer], p, n_lat_tiles, n_act, last)
    return xc
```

```python
import functools
import math

import jax
import jax.numpy as jnp
from jax import lax
from jax.experimental import pallas as pl
from jax.experimental.pallas import tpu as pltpu

F32 = jnp.float32
BF16 = jnp.bfloat16

GRID_W = 64
HEADS = 8
NOPE = 64
ROPE = 32
VDIM = 64
QK_PAD = 128
Q_RANK = 384
KV_RANK = 256
CONV_CH = 256
CONV_WIDTH = 31
CONV_HALO = 16
SGU_HEADS = 4
SGU_HEAD_DIM = 64
SGU_CH = 256
CHUNK = 128
EPS = 1e-6
ROPE_BASE = 10000.0
ATTN_SCALE = (NOPE + ROPE) ** -0.5
LOG2E = math.log2(math.e)

TOK_TILE = 256
Q_TILE = 2048
Q_BLOCK = 512
KV_STEP = 512
KV_UNROLL = 4
R0_KEYS = 16
STALE_MAX_HEADROOM = 64.0
V_EXT = VDIM + 16
FF_CHUNK = 1024
ADA_COLS = 1536
VMEM_LIMIT_BYTES = 56 * 1024 * 1024

ZC_Q = 0
ZC_KV = ZC_Q + Q_RANK
ZC_CONV = ZC_KV + KV_RANK
ZC_SGU = ZC_CONV + 2 * CONV_CH
ZC_KR = ZC_SGU + 2 * SGU_CH
ZC_KRR = ZC_KR + QK_PAD
Z_COLS = ZC_KRR + QK_PAD


def _cparams(sem):
    return pltpu.CompilerParams(dimension_semantics=sem, vmem_limit_bytes=VMEM_LIMIT_BYTES)


def _sigmoid(v):
    return 1.0 / (1.0 + jnp.exp(-v))


def _layernorm(v, g, b):
    mu = jnp.mean(v, -1, keepdims=True)
    d = v - mu
    var = jnp.mean(d * d, -1, keepdims=True)
    return d * lax.rsqrt(var + EPS) * g + b


def _rms(v, g):
    return v * lax.rsqrt(jnp.mean(v * v, -1, keepdims=True) + EPS) * g


def _dot(a, b):
    return jnp.dot(a, b, preferred_element_type=F32)


def _dot_nt(a, b):
    return lax.dot_general(a, b, (((1,), (1,)), ((), ())), preferred_element_type=F32)


def _dot_tn(a, b):
    return lax.dot_general(a, b, (((0,), (0,)), ((), ())), preferred_element_type=F32)


def _ada_kernel(c_ref, w_ref, b_ref, o_ref):
    c = c_ref[...]
    s = (c * _sigmoid(c)).astype(BF16)
    o_ref[0] = _dot(s, w_ref[0].astype(BF16)) + b_ref[0]


def _ada_call(cvec, ada_w, ada_b):
    n_layers, d, n6 = ada_w.shape
    rows = cvec.shape[0]
    return pl.pallas_call(
        _ada_kernel,
        grid=(n_layers, n6 // ADA_COLS),
        in_specs=[
            pl.BlockSpec((rows, d), lambda l, j: (0, 0)),
            pl.BlockSpec((1, d, ADA_COLS), lambda l, j: (l, 0, j)),
            pl.BlockSpec((1, 1, ADA_COLS), lambda l, j: (l, 0, j)),
        ],
        out_specs=pl.BlockSpec((1, rows, ADA_COLS), lambda l, j: (l, 0, j)),
        out_shape=jax.ShapeDtypeStruct((n_layers, rows, n6), F32),
        compiler_params=_cparams(("arbitrary", "arbitrary")),
        name="adaln",
    )(cvec, ada_w, ada_b.reshape(n_layers, 1, n6))


def _pre_kernel(x_ref, xp_ref, xn_ref, mod_ref, n1g_ref, win_ref, qg_ref, wqT_ref, wqrT_ref,
                kvg_ref, wk_ref, wvT_ref, cosk_ref, sink_ref, cosq_ref, sinq_ref,
                convw_ref, convb_ref, clng_ref, clnb_ref, slng_ref, slnb_ref, sguw_ref, sgub_ref,
                qT_out, k_out, vT_out, yc_out, ys_out, ybuf, *, n_tiles):
    i = pl.program_id(1)
    tt = x_ref.shape[1]
    g1 = n1g_ref[0]
    sh1 = mod_ref[0, 0, 0:1, :]
    sc1 = mod_ref[0, 0, 1:2, :]

    def normmod(xt):
        return (_rms(xt, g1) * (1.0 + sc1) + sh1).astype(BF16)

    z = _dot(normmod(x_ref[0]), win_ref[0])

    qn = _rms(z[:, ZC_Q:ZC_Q + Q_RANK], qg_ref[0]).astype(BF16)
    qT = _dot_nt(wqT_ref[0], qn)
    qrT = _dot_nt(wqrT_ref[0], qn)
    cq = cosq_ref[...]
    sq = sinq_ref[...]
    qscale = ATTN_SCALE * LOG2E
    for h in range(HEADS):
        r0 = h * QK_PAD
        qT_out[0, r0:r0 + NOPE, :] = (qT[r0:r0 + NOPE, :] * qscale).astype(BF16)
        rot = qT[r0 + NOPE:r0 + NOPE + ROPE, :] * cq + qrT[h * ROPE:(h + 1) * ROPE, :] * sq
        qT_out[0, r0 + NOPE:r0 + NOPE + ROPE, :] = (rot * qscale).astype(BF16)
        qT_out[0, r0 + NOPE + ROPE:r0 + QK_PAD, :] = jnp.zeros((QK_PAD - NOPE - ROPE, tt), BF16)

    kvn = _rms(z[:, ZC_KV:ZC_KV + KV_RANK], kvg_ref[0]).astype(BF16)
    kext = _dot(kvn, wk_ref[0])
    krope = z[:, ZC_KR:ZC_KR + QK_PAD] * cosk_ref[...] + z[:, ZC_KRR:ZC_KRR + QK_PAD] * sink_ref[...]
    for h in range(HEADS):
        c0 = h * QK_PAD
        k_out[0, :, c0:c0 + QK_PAD] = (kext[:, c0:c0 + QK_PAD] + krope).astype(BF16)
    vT = _dot_nt(wvT_ref[0], kvn)
    for h in range(HEADS):
        vT_out[0, h * V_EXT:h * V_EXT + VDIM, :] = vT[h * VDIM:(h + 1) * VDIM, :].astype(BF16)
        vT_out[0, h * V_EXT + VDIM:(h + 1) * V_EXT, :] = jnp.ones((V_EXT - VDIM, tt), BF16)

    def glu(zc):
        return zc[:, :CONV_CH] * _sigmoid(zc[:, CONV_CH:])

    wconv_in = win_ref[0, :, ZC_CONV:ZC_CONV + 2 * CONV_CH]
    first = jnp.logical_or(i == 0, i == n_tiles - 1)
    last = i >= n_tiles - 2
    yp = glu(_dot(normmod(xp_ref[0]), wconv_in))
    yn = glu(_dot(normmod(xn_ref[0]), wconv_in))
    ybuf[0:CONV_HALO, :] = jnp.where(first, 0.0, yp)
    ybuf[CONV_HALO:CONV_HALO + tt, :] = glu(z[:, ZC_CONV:ZC_CONV + 2 * CONV_CH])
    ybuf[CONV_HALO + tt:, :] = jnp.where(last, 0.0, yn)
    cw = convw_ref[0]
    acc = jnp.zeros((tt, CONV_CH), F32)
    for k in range(CONV_WIDTH):
        acc = acc + ybuf[k + 1:k + 1 + tt, :] * cw[k:k + 1, :]
    yc = _layernorm(acc + convb_ref[0], clng_ref[0], clnb_ref[0])
    yc_out[0] = (yc * _sigmoid(yc)).astype(BF16)

    zs = z[:, ZC_SGU:ZC_SGU + 2 * SGU_CH]
    zg = 0.5 * zs * (1.0 + jnp.tanh(math.sqrt(2.0 / math.pi) * (zs + 0.044715 * (zs * zs * zs))))
    u = zg[:, :SGU_CH]
    v = _layernorm(zg[:, SGU_CH:], slng_ref[0], slnb_ref[0]).astype(BF16)
    lane = lax.broadcasted_iota(jnp.int32, (CHUNK, SGU_CH), 1)
    bias = sgub_ref[0]
    for c in range(tt // CHUNK):
        vc = v[c * CHUNK:(c + 1) * CHUNK, :]
        mixed = _dot(sguw_ref[0, 0], vc)
        for h in range(1, SGU_HEADS):
            mixed = jnp.where(lane >= h * SGU_HEAD_DIM, _dot(sguw_ref[0, h], vc), mixed)
        ys_out[0, c * CHUNK:(c + 1) * CHUNK, :] = (u[c * CHUNK:(c + 1) * CHUNK, :] * (mixed + bias)).astype(BF16)


def _pre_call(layer, xc, modsel, p, n_lat_tiles):
    bsz, ltot, d = xc.shape
    tt = TOK_TILE
    n_tiles = ltot // tt
    hb = tt // CONV_HALO
    n_hblocks = ltot // CONV_HALO

    def lay(shape):
        nd = len(shape)
        return pl.BlockSpec((1,) + shape, lambda b, i: (layer,) + (0,) * nd)

    in_specs = [
        pl.BlockSpec((1, tt, d), lambda b, i: (b, i, 0)),
        pl.BlockSpec((1, CONV_HALO, d), lambda b, i: (b, jnp.maximum(i * hb - 1, 0), 0)),
        pl.BlockSpec((1, CONV_HALO, d), lambda b, i: (b, jnp.minimum((i + 1) * hb, n_hblocks - 1), 0)),
        pl.BlockSpec((1, 1, 8, d), lambda b, i: (b, i // n_lat_tiles, 0, 0)),
        lay((1, d)),
        lay((d, Z_COLS)),
        lay((1, Q_RANK)),
        lay((HEADS * QK_PAD, Q_RANK)),
        lay((HEADS * ROPE, Q_RANK)),
        lay((1, KV_RANK)),
        lay((KV_RANK, HEADS * QK_PAD)),
        lay((HEADS * VDIM, KV_RANK)),
        pl.BlockSpec((tt, QK_PAD), lambda b, i: (i, 0)),
        pl.BlockSpec((tt, QK_PAD), lambda b, i: (i, 0)),
        pl.BlockSpec((ROPE, tt), lambda b, i: (0, i)),
        pl.BlockSpec((ROPE, tt), lambda b, i: (0, i)),
        lay((CONV_WIDTH, CONV_CH)),
        lay((1, CONV_CH)),
        lay((1, CONV_CH)),
        lay((1, CONV_CH)),
        lay((1, SGU_CH)),
        lay((1, SGU_CH)),
        lay((SGU_HEADS, CHUNK, CHUNK)),
        lay((CHUNK, SGU_CH)),
    ]
    out_specs = [
        pl.BlockSpec((1, HEADS * QK_PAD, tt), lambda b, i: (b, 0, i)),
        pl.BlockSpec((1, tt, HEADS * QK_PAD), lambda b, i: (b, i, 0)),
        pl.BlockSpec((1, HEADS * V_EXT, tt), lambda b, i: (b, 0, i)),
        pl.BlockSpec((1, tt, CONV_CH), lambda b, i: (b, i, 0)),
        pl.BlockSpec((1, tt, SGU_CH), lambda b, i: (b, i, 0)),
    ]
    out_shape = [
        jax.ShapeDtypeStruct((bsz, HEADS * QK_PAD, ltot), BF16),
        jax.ShapeDtypeStruct((bsz, ltot, HEADS * QK_PAD), BF16),
        jax.ShapeDtypeStruct((bsz, HEADS * V_EXT, ltot), BF16),
        jax.ShapeDtypeStruct((bsz, ltot, CONV_CH), BF16),
        jax.ShapeDtypeStruct((bsz, ltot, SGU_CH), BF16),
    ]
    return pl.pallas_call(
        functools.partial(_pre_kernel, n_tiles=n_tiles),
        grid=(bsz, n_tiles),
        in_specs=in_specs,
        out_specs=out_specs,
        out_shape=out_shape,
        scratch_shapes=[pltpu.VMEM((tt + 2 * CONV_HALO, CONV_CH), F32)],
        compiler_params=_cparams(("arbitrary", "arbitrary")),
        name="pre",
    )(xc, xc, xc, modsel, p["norm1_g"], p["w_in"], p["q_norm_g"], p["wqT"], p["wqrT"],
      p["kv_norm_g"], p["wk"], p["wvT"], p["cosk"], p["sink"], p["cosq"], p["sinq"],
      p["conv_w"], p["conv_b"], p["conv_ln_g"], p["conv_ln_b"], p["sgu_ln_g"], p["sgu_ln_b"],
      p["sgu_w"], p["sgu_bias"])


def _softmax_update(s, vt, m, acc):
    m_new = jnp.maximum(m, jnp.max(s, axis=0, keepdims=True))
    alpha = jnp.exp2(m - m_new)
    pr = jnp.exp2(s - m_new).astype(BF16)
    return m_new, alpha * acc + _dot(vt, pr)


def _softmax_finish(acc):
    return (acc[0:VDIM, :] / acc[VDIM:VDIM + 1, :]).astype(BF16)


def _attn_lat_kernel(qT_ref, k_ref, vT_ref, o_ref, acc_scr, *, n_lat):
    tq = qT_ref.shape[2]
    n_cb = tq // Q_BLOCK
    n_ctx = k_ref.shape[1] - n_lat
    n_steps = n_lat // KV_STEP

    def cols(c):
        return slice(c * Q_BLOCK, (c + 1) * Q_BLOCK)

    def scores(o, size, c):
        return _dot(k_ref[0, pl.ds(o, size), :], qT_ref[0, :, cols(c)])

    def run(blocks, carry):
        rs, excess = list(carry[0]), list(carry[1])
        s_next = scores(*blocks[0])
        for i, (o, size, c) in enumerate(blocks):
            s = s_next
            if i + 1 < len(blocks):
                s_next = scores(*blocks[i + 1])
            mx = jnp.max(s, axis=0, keepdims=True)
            pv = _dot(vT_ref[0, :, pl.ds(o, size)], jnp.exp2(s - rs[c]).astype(BF16))
            r_new = jnp.maximum(rs[c], mx)
            acc_scr[:, cols(c)] = (acc_scr[:, cols(c)] + pv) * jnp.exp2(rs[c] - r_new)
            excess[c] = jnp.maximum(excess[c], mx - rs[c])
            rs[c] = r_new
        return tuple(rs), tuple(excess)

    def latent_steps(j, carry):
        offs = [pl.multiple_of((j * KV_UNROLL + u) * KV_STEP, KV_STEP) for u in range(KV_UNROLL)]
        return run([(o, KV_STEP, c) for o in offs for c in range(n_cb)], carry)

    acc_scr[...] = jnp.zeros(acc_scr.shape, F32)
    carry = (tuple(jnp.max(scores(n_lat, R0_KEYS, c), axis=0, keepdims=True) for c in range(n_cb)),
             tuple(jnp.zeros((1, Q_BLOCK), F32) for _ in range(n_cb)))
    carry = run([(n_lat, n_ctx, c) for c in range(n_cb)], carry)
    _, excess = lax.fori_loop(0, n_steps // KV_UNROLL, latent_steps, carry)
    o_ref[0] = _softmax_finish(acc_scr[...])

    @pl.when(jnp.max(jnp.concatenate(excess, axis=1)) > STALE_MAX_HEADROOM)
    def _():
        def exact_step(o, size, carry):
            s = _dot(k_ref[0, pl.ds(o, size), :], qT_ref[0])
            return _softmax_update(s, vT_ref[0, :, pl.ds(o, size)], *carry)

        carry = lax.fori_loop(
            0, n_steps, lambda t, cr: exact_step(pl.multiple_of(t * KV_STEP, KV_STEP), KV_STEP, cr),
            (jnp.full((1, tq), -jnp.inf, F32), jnp.zeros((V_EXT, tq), F32)))
        _, acc = exact_step(n_lat, n_ctx, carry)
        o_ref[0] = _softmax_finish(acc)


def _attn_ctx_kernel(qT_ref, k_ref, vT_ref, a_hbm_ref, o_ref):
    del a_hbm_ref
    s = _dot(k_ref[0], qT_ref[0])
    tq = s.shape[1]
    _, acc = _softmax_update(s, vT_ref[0], jnp.full((1, tq), -jnp.inf, F32), jnp.zeros((V_EXT, tq), F32))
    o_ref[0] = _softmax_finish(acc)


def _attn_call(qT, k, vT, n_lat, with_ctx):
    bsz, _, ltot = qT.shape
    tq = Q_TILE
    n_ctx = ltot - n_lat
    assert n_lat % (KV_STEP * KV_UNROLL) == 0 and n_lat % tq == 0 and tq % Q_BLOCK == 0
    aT = pl.pallas_call(
        functools.partial(_attn_lat_kernel, n_lat=n_lat),
        grid=(bsz, HEADS, n_lat // tq),
        in_specs=[
            pl.BlockSpec((1, QK_PAD, tq), lambda b, h, q: (b, h, q)),
            pl.BlockSpec((1, ltot, QK_PAD), lambda b, h, q: (b, 0, h)),
            pl.BlockSpec((1, V_EXT, ltot), lambda b, h, q: (b, h, 0)),
        ],
        out_specs=pl.BlockSpec((1, VDIM, tq), lambda b, h, q: (b, h, q)),
        out_shape=jax.ShapeDtypeStruct((bsz, HEADS * VDIM, ltot), BF16),
        scratch_shapes=[pltpu.VMEM((V_EXT, tq), F32)],
        compiler_params=_cparams(("arbitrary", "arbitrary", "arbitrary")),
        name="attn_lat",
    )(qT, k, vT)
    if not with_ctx:
        return aT
    cb = n_lat // n_ctx
    return pl.pallas_call(
        _attn_ctx_kernel,
        grid=(bsz, HEADS),
        in_specs=[
            pl.BlockSpec((1, QK_PAD, n_ctx), lambda b, h: (b, h, cb)),
            pl.BlockSpec((1, n_ctx, QK_PAD), lambda b, h: (b, cb, h)),
            pl.BlockSpec((1, V_EXT, n_ctx), lambda b, h: (b, h, cb)),
            pl.BlockSpec(memory_space=pl.ANY),
        ],
        out_specs=pl.BlockSpec((1, VDIM, n_ctx), lambda b, h: (b, h, cb)),
        out_shape=jax.ShapeDtypeStruct(aT.shape, aT.dtype),
        input_output_aliases={3: 0},
        compiler_params=_cparams(("arbitrary", "arbitrary")),
        name="attn_ctx",
    )(qT, k, vT, aT)


def _post_kernel(x_ref, aT_ref, yc_ref, ys_ref, mod_ref, n2g_ref, wout_ref, w1_ref, w2_ref, fg_ref,
                 o_ref, *, final):
    d = x_ref.shape[2]
    d_ff = w1_ref.shape[2]
    n_attn = aT_ref.shape[1]
    g1 = mod_ref[0, 0, 2:3, :]
    sh2 = mod_ref[0, 0, 3:4, :]
    sc2 = mod_ref[0, 0, 4:5, :]
    g2 = mod_ref[0, 0, 5:6, :]
    y = _dot_tn(aT_ref[0], wout_ref[0, 0:n_attn, :])
    y = y + _dot(yc_ref[0], wout_ref[0, n_attn:n_attn + CONV_CH, :])
    y = y + _dot(ys_ref[0], wout_ref[0, n_attn + CONV_CH:d, :])
    x1 = x_ref[0] + g1 * y
    h = (_rms(x1, n2g_ref[0]) * (1.0 + sc2) + sh2).astype(BF16)
    ff = jnp.zeros(x1.shape, F32)
    for c in range(d_ff // FF_CHUNK):
        f = jnp.maximum(_dot(h, w1_ref[0, :, c * FF_CHUNK:(c + 1) * FF_CHUNK]), 0.0)
        ff = ff + _dot((f * f).astype(BF16), w2_ref[0, c * FF_CHUNK:(c + 1) * FF_CHUNK, :])
    x2 = x1 + g2 * ff
    if final:
        x2 = _rms(x2, fg_ref[...])
    o_ref[0] = x2


def _post_call(layer, xc, aT, yc, ys, modsel, p, n_lat_tiles, n_out_tiles, final):
    bsz, _, d = xc.shape
    tt = TOK_TILE
    d_ff = p["w_ff1"].shape[2]

    def lay(shape):
        nd = len(shape)
        return pl.BlockSpec((1,) + shape, lambda b, i: (layer,) + (0,) * nd,
                            pipeline_mode=pl.Buffered(1))

    return pl.pallas_call(
        functools.partial(_post_kernel, final=final),
        grid=(bsz, n_out_tiles),
        in_specs=[
            pl.BlockSpec((1, tt, d), lambda b, i: (b, i, 0)),
            pl.BlockSpec((1, HEADS * VDIM, tt), lambda b, i: (b, 0, i)),
            pl.BlockSpec((1, tt, CONV_CH), lambda b, i: (b, i, 0)),
            pl.BlockSpec((1, tt, SGU_CH), lambda b, i: (b, i, 0)),
            pl.BlockSpec((1, 1, 8, d), lambda b, i: (b, i // n_lat_tiles, 0, 0)),
            pl.BlockSpec((1, 1, d), lambda b, i: (layer, 0, 0)),
            lay((d, d)),
            lay((d, d_ff)),
            lay((d_ff, d)),
            pl.BlockSpec((1, d), lambda b, i: (0, 0)),
        ],
        out_specs=pl.BlockSpec((1, tt, d), lambda b, i: (b, i, 0)),
        out_shape=jax.ShapeDtypeStruct((bsz, n_out_tiles * tt, d), F32),
        compiler_params=_cparams(("arbitrary", "arbitrary")),
        name="post",
    )(xc, aT, yc, ys, modsel, p["norm2_g"], p["w_out"], p["w_ff1"], p["w_ff2"], p["final_g"])


def _rotate_half_cols(w):
    n = ROPE // 4
    return jnp.concatenate([-w[..., n:2 * n], w[..., 0:n], -w[..., 3 * n:4 * n], w[..., 2 * n:3 * n]], -1)


def _prepare(n_lat, n_ctx, norm1_g, norm2_g, w_in, q_norm_g, w_uq, kv_norm_g, w_ukv, conv_w, conv_b,
             conv_ln_g, conv_ln_b, sgu_ln_g, sgu_ln_b, sgu_w, sgu_b, w_out, w_ff1, w_ff2, final_g):
    n_layers, d, _ = w_in.shape
    o_q, o_kv, o_kr = 0, Q_RANK, Q_RANK + KV_RANK
    o_conv = o_kr + ROPE
    o_sgu = o_conv + 2 * CONV_CH
    w_kr = w_in[:, :, o_kr:o_kr + ROPE]

    def placed(w):
        return jnp.pad(w, ((0, 0), (0, 0), (NOPE, QK_PAD - NOPE - ROPE)))

    w_in_x = jnp.concatenate([
        w_in[:, :, o_q:o_q + Q_RANK], w_in[:, :, o_kv:o_kv + KV_RANK],
        w_in[:, :, o_conv:o_conv + 2 * CONV_CH], w_in[:, :, o_sgu:o_sgu + 2 * SGU_CH],
        placed(w_kr), placed(_rotate_half_cols(w_kr))], -1).astype(BF16)

    wq = w_uq.reshape(n_layers, Q_RANK, HEADS, NOPE + ROPE)
    wq_pad = jnp.pad(wq, ((0, 0), (0, 0), (0, 0), (0, QK_PAD - NOPE - ROPE)))
    wqT = wq_pad.reshape(n_layers, Q_RANK, HEADS * QK_PAD).transpose(0, 2, 1).astype(BF16)
    wqr = _rotate_half_cols(wq[..., NOPE:])
    wqrT = wqr.reshape(n_layers, Q_RANK, HEADS * ROPE).transpose(0, 2, 1).astype(BF16)

    wkv = w_ukv.reshape(n_layers, KV_RANK, HEADS, NOPE + VDIM)
    wk = jnp.pad(wkv[..., :NOPE], ((0, 0), (0, 0), (0, 0), (0, QK_PAD - NOPE)))
    wk = wk.reshape(n_layers, KV_RANK, HEADS * QK_PAD).astype(BF16)
    wvT = wkv[..., NOPE:].reshape(n_layers, KV_RANK, HEADS * VDIM).transpose(0, 2, 1).astype(BF16)

    t = jnp.arange(n_lat, dtype=jnp.int32)
    n = ROPE // 4
    inv = 1.0 / (ROPE_BASE ** (jnp.arange(n, dtype=F32) / n))
    ang_r = (t // GRID_W).astype(F32)[:, None] * inv
    ang_c = (t % GRID_W).astype(F32)[:, None] * inv
    cos32 = jnp.concatenate([jnp.cos(ang_r)] * 2 + [jnp.cos(ang_c)] * 2, -1)
    sin32 = jnp.concatenate([jnp.sin(ang_r)] * 2 + [jnp.sin(ang_c)] * 2, -1)
    cos32 = jnp.concatenate([cos32, jnp.ones((n_ctx, ROPE), F32)], 0)
    sin32 = jnp.concatenate([sin32, jnp.zeros((n_ctx, ROPE), F32)], 0)
    pad = ((0, 0), (NOPE, QK_PAD - NOPE - ROPE))

    sgu_bias = jnp.broadcast_to(jnp.swapaxes(sgu_b, 1, 2)[:, :, :, None],
                                (n_layers, CHUNK, SGU_HEADS, SGU_HEAD_DIM)).reshape(n_layers, CHUNK, SGU_CH)

    def row(a):
        return a.reshape(n_layers, 1, a.shape[-1])

    return dict(
        norm1_g=row(norm1_g), norm2_g=row(norm2_g), w_in=w_in_x, q_norm_g=row(q_norm_g), wqT=wqT, wqrT=wqrT,
        kv_norm_g=row(kv_norm_g), wk=wk, wvT=wvT,
        cosk=jnp.pad(cos32, pad), sink=jnp.pad(sin32, pad), cosq=cos32.T, sinq=sin32.T,
        conv_w=conv_w, conv_b=row(conv_b), conv_ln_g=row(conv_ln_g), conv_ln_b=row(conv_ln_b),
        sgu_ln_g=row(sgu_ln_g), sgu_ln_b=row(sgu_ln_b), sgu_w=sgu_w.astype(BF16), sgu_bias=sgu_bias,
        w_out=w_out.astype(BF16), w_ff1=w_ff1.astype(BF16), w_ff2=w_ff2.astype(BF16),
        final_g=final_g.reshape(1, -1))


def kernel(x, c, ctx, c_ctx, ada_w, ada_b, norm1_g, norm2_g, w_in, q_norm_g, w_uq, kv_norm_g, w_ukv,
           conv_w, conv_b, conv_ln_g, conv_ln_b, sgu_ln_g, sgu_ln_b, sgu_w, sgu_b, w_out, w_ff1, w_ff2,
           final_g):
    bsz, n_lat, d = x.shape
    n_ctx = ctx.shape[1]
    n_layers = w_in.shape[0]
    assert n_ctx == TOK_TILE and n_lat % TOK_TILE == 0 and n_lat % GRID_W == 0
    n_lat_tiles = n_lat // TOK_TILE
    n_tiles = n_lat_tiles + 1

    p = _prepare(n_lat, n_ctx, norm1_g, norm2_g, w_in, q_norm_g, w_uq, kv_norm_g, w_ukv, conv_w, conv_b,
                 conv_ln_g, conv_ln_b, sgu_ln_g, sgu_ln_b, sgu_w, sgu_b, w_out, w_ff1, w_ff2, final_g)

    rows = -(-(bsz + 1) // 8) * 8
    cvec = jnp.concatenate([c, c_ctx[None, :], jnp.zeros((rows - bsz - 1, d), F32)], 0)
    mods = _ada_call(cvec, ada_w, ada_b).reshape(n_layers, rows, 6, d)
    m_lat = mods[:, :bsz]
    m_ctx = jnp.broadcast_to(mods[:, bsz][:, None], m_lat.shape)
    modsel = jnp.pad(jnp.stack([m_lat, m_ctx], 2), ((0, 0), (0, 0), (0, 0), (0, 2), (0, 0)))

    xc = jnp.concatenate([x, ctx], 1)
    for layer in range(n_layers):
        last = layer == n_layers - 1
        n_act = n_lat_tiles if last else n_tiles
        qT, k, vT, yc, ys = _pre_call(layer, xc, modsel[layer], p, n_lat_tiles)
        aT = _attn_call(qT, k, vT, n_lat, not last)
        xc = _post_call(layer, xc, aT, yc, ys, modsel[layer], p, n_lat_tiles, n_act, last)
    return xc
```

```python
import functools
import math

import jax
import jax.numpy as jnp
from jax import lax
from jax.experimental import pallas as pl
from jax.experimental.pallas import tpu as pltpu

F32 = jnp.float32
BF16 = jnp.bfloat16

GRID_W = 64
HEADS = 8
NOPE = 64
ROPE = 32
VDIM = 64
QK_PAD = 128
Q_RANK = 384
KV_RANK = 256
CONV_CH = 256
CONV_WIDTH = 31
CONV_HALO = 16
SGU_HEADS = 4
SGU_HEAD_DIM = 64
SGU_CH = 256
CHUNK = 128
EPS = 1e-6
ROPE_BASE = 10000.0
ATTN_SCALE = (NOPE + ROPE) ** -0.5
LOG2E = math.log2(math.e)

TOK_TILE = 256
Q_TILE = 2048
Q_BLOCK = 512
KV_STEP = 512
KV_UNROLL = 8
R0_KEYS = 16
STALE_MAX_HEADROOM = 64.0
V_EXT = VDIM + 16
FF_CHUNK = 1024
ADA_COLS = 1536
VMEM_LIMIT_BYTES = 56 * 1024 * 1024

ZC_Q = 0
ZC_KV = ZC_Q + Q_RANK
ZC_CONV = ZC_KV + KV_RANK
ZC_SGU = ZC_CONV + 2 * CONV_CH
ZC_KR = ZC_SGU + 2 * SGU_CH
ZC_KRR = ZC_KR + QK_PAD
Z_COLS = ZC_KRR + QK_PAD


def _cparams(sem):
    return pltpu.CompilerParams(dimension_semantics=sem, vmem_limit_bytes=VMEM_LIMIT_BYTES)


def _sigmoid(v):
    return 1.0 / (1.0 + jnp.exp(-v))


def _layernorm(v, g, b):
    mu = jnp.mean(v, -1, keepdims=True)
    d = v - mu
    var = jnp.mean(d * d, -1, keepdims=True)
    return d * lax.rsqrt(var + EPS) * g + b


def _rms(v, g):
    return v * lax.rsqrt(jnp.mean(v * v, -1, keepdims=True) + EPS) * g


def _dot(a, b):
    return jnp.dot(a, b, preferred_element_type=F32)


def _dot_nt(a, b):
    return lax.dot_general(a, b, (((1,), (1,)), ((), ())), preferred_element_type=F32)


def _dot_tn(a, b):
    return lax.dot_general(a, b, (((0,), (0,)), ((), ())), preferred_element_type=F32)


def _ada_kernel(c_ref, w_ref, b_ref, o_ref):
    c = c_ref[...]
    s = (c * _sigmoid(c)).astype(BF16)
    o_ref[0] = _dot(s, w_ref[0].astype(BF16)) + b_ref[0]


def _ada_call(cvec, ada_w, ada_b):
    n_layers, d, n6 = ada_w.shape
    rows = cvec.shape[0]
    return pl.pallas_call(
        _ada_kernel,
        grid=(n_layers, n6 // ADA_COLS),
        in_specs=[
            pl.BlockSpec((rows, d), lambda l, j: (0, 0)),
            pl.BlockSpec((1, d, ADA_COLS), lambda l, j: (l, 0, j)),
            pl.BlockSpec((1, 1, ADA_COLS), lambda l, j: (l, 0, j)),
        ],
        out_specs=pl.BlockSpec((1, rows, ADA_COLS), lambda l, j: (l, 0, j)),
        out_shape=jax.ShapeDtypeStruct((n_layers, rows, n6), F32),
        compiler_params=_cparams(("arbitrary", "arbitrary")),
        name="adaln",
    )(cvec, ada_w, ada_b.reshape(n_layers, 1, n6))


def _pre_kernel(x_ref, xp_ref, xn_ref, mod_ref, n1g_ref, win_ref, qg_ref, wqT_ref, wqrT_ref,
                kvg_ref, wk_ref, wvT_ref, cosk_ref, sink_ref, cosq_ref, sinq_ref,
                convw_ref, convb_ref, clng_ref, clnb_ref, slng_ref, slnb_ref, sguw_ref, sgub_ref,
                qT_out, k_out, vT_out, yc_out, ys_out, ybuf, yshift, *, n_tiles):
    i = pl.program_id(1)
    tt = x_ref.shape[1]
    sh1 = mod_ref[0, 0, 0:1, :]
    gain1 = n1g_ref[0] * (1.0 + mod_ref[0, 0, 1:2, :])

    def normmod(xt):
        return (_rms(xt, gain1) + sh1).astype(BF16)

    z = _dot(normmod(x_ref[0]), win_ref[0])

    qn = _rms(z[:, ZC_Q:ZC_Q + Q_RANK], qg_ref[0]).astype(BF16)
    qT = _dot_nt(wqT_ref[0], qn)
    qrT = _dot_nt(wqrT_ref[0], qn)
    cq = cosq_ref[...]
    sq = sinq_ref[...]
    qscale = ATTN_SCALE * LOG2E
    for h in range(HEADS):
        r0 = h * QK_PAD
        qT_out[0, r0:r0 + NOPE, :] = (qT[r0:r0 + NOPE, :] * qscale).astype(BF16)
        rot = qT[r0 + NOPE:r0 + NOPE + ROPE, :] * cq + qrT[h * ROPE:(h + 1) * ROPE, :] * sq
        qT_out[0, r0 + NOPE:r0 + NOPE + ROPE, :] = (rot * qscale).astype(BF16)
        qT_out[0, r0 + NOPE + ROPE:r0 + QK_PAD, :] = jnp.zeros((QK_PAD - NOPE - ROPE, tt), BF16)

    kvn = _rms(z[:, ZC_KV:ZC_KV + KV_RANK], kvg_ref[0]).astype(BF16)
    kext = _dot(kvn, wk_ref[0])
    krope = z[:, ZC_KR:ZC_KR + QK_PAD] * cosk_ref[...] + z[:, ZC_KRR:ZC_KRR + QK_PAD] * sink_ref[...]
    for h in range(HEADS):
        c0 = h * QK_PAD
        k_out[0, :, c0:c0 + QK_PAD] = (kext[:, c0:c0 + QK_PAD] + krope).astype(BF16)
    vT = _dot_nt(wvT_ref[0], kvn)
    for h in range(HEADS):
        vT_out[0, h * V_EXT:h * V_EXT + VDIM, :] = vT[h * VDIM:(h + 1) * VDIM, :].astype(BF16)
        vT_out[0, h * V_EXT + VDIM:(h + 1) * V_EXT, :] = jnp.ones((V_EXT - VDIM, tt), BF16)

    def glu(zc):
        return zc[:, :CONV_CH] * _sigmoid(zc[:, CONV_CH:])

    wconv_in = win_ref[0, :, ZC_CONV:ZC_CONV + 2 * CONV_CH]
    first = jnp.logical_or(i == 0, i == n_tiles - 1)
    last = i >= n_tiles - 2
    yp = glu(_dot(normmod(xp_ref[0]), wconv_in))
    yn = glu(_dot(normmod(xn_ref[0]), wconv_in))
    ybuf[0:CONV_HALO, :] = jnp.where(first, 0.0, yp)
    ybuf[CONV_HALO:CONV_HALO + tt, :] = glu(z[:, ZC_CONV:ZC_CONV + 2 * CONV_CH])
    ybuf[CONV_HALO + tt:, :] = jnp.where(last, 0.0, yn)
    cw = convw_ref[0]
    acc = jnp.zeros((tt, CONV_CH), F32)
    n_a = (CONV_WIDTH + 8) // 8
    for b in range(8):
        src = ybuf
        if b:
            yshift[...] = ybuf[b:b + yshift.shape[0], :]
            src = yshift
        for a in range(n_a):
            k = 8 * a + b - 1
            if 0 <= k < CONV_WIDTH:
                acc = acc + src[8 * a:8 * a + tt, :] * cw[k:k + 1, :]
    yc = _layernorm(acc + convb_ref[0], clng_ref[0], clnb_ref[0])
    yc_out[0] = (yc * _sigmoid(yc)).astype(BF16)

    zs = z[:, ZC_SGU:ZC_SGU + 2 * SGU_CH]
    zg = 0.5 * zs * (1.0 + jnp.tanh(math.sqrt(2.0 / math.pi) * (zs + 0.044715 * (zs * zs * zs))))
    u = zg[:, :SGU_CH]
    v = _layernorm(zg[:, SGU_CH:], slng_ref[0], slnb_ref[0]).astype(BF16)
    lane = lax.broadcasted_iota(jnp.int32, (CHUNK, SGU_CH), 1)
    bias = sgub_ref[0]
    for c in range(tt // CHUNK):
        vc = v[c * CHUNK:(c + 1) * CHUNK, :]
        mixed = _dot(sguw_ref[0, 0], vc)
        for h in range(1, SGU_HEADS):
            mixed = jnp.where(lane >= h * SGU_HEAD_DIM, _dot(sguw_ref[0, h], vc), mixed)
        ys_out[0, c * CHUNK:(c + 1) * CHUNK, :] = (u[c * CHUNK:(c + 1) * CHUNK, :] * (mixed + bias)).astype(BF16)


def _pre_call(layer, xc, modsel, p, n_lat_tiles):
    bsz, ltot, d = xc.shape
    tt = TOK_TILE
    n_tiles = ltot // tt
    hb = tt // CONV_HALO
    n_hblocks = ltot // CONV_HALO

    def lay(shape):
        nd = len(shape)
        return pl.BlockSpec((1,) + shape, lambda b, i: (layer,) + (0,) * nd)

    in_specs = [
        pl.BlockSpec((1, tt, d), lambda b, i: (b, i, 0)),
        pl.BlockSpec((1, CONV_HALO, d), lambda b, i: (b, jnp.maximum(i * hb - 1, 0), 0)),
        pl.BlockSpec((1, CONV_HALO, d), lambda b, i: (b, jnp.minimum((i + 1) * hb, n_hblocks - 1), 0)),
        pl.BlockSpec((1, 1, 8, d), lambda b, i: (b, i // n_lat_tiles, 0, 0)),
        lay((1, d)),
        lay((d, Z_COLS)),
        lay((1, Q_RANK)),
        lay((HEADS * QK_PAD, Q_RANK)),
        lay((HEADS * ROPE, Q_RANK)),
        lay((1, KV_RANK)),
        lay((KV_RANK, HEADS * QK_PAD)),
        lay((HEADS * VDIM, KV_RANK)),
        pl.BlockSpec((tt, QK_PAD), lambda b, i: (i, 0)),
        pl.BlockSpec((tt, QK_PAD), lambda b, i: (i, 0)),
        pl.BlockSpec((ROPE, tt), lambda b, i: (0, i)),
        pl.BlockSpec((ROPE, tt), lambda b, i: (0, i)),
        lay((CONV_WIDTH, CONV_CH)),
        lay((1, CONV_CH)),
        lay((1, CONV_CH)),
        lay((1, CONV_CH)),
        lay((1, SGU_CH)),
        lay((1, SGU_CH)),
        lay((SGU_HEADS, CHUNK, CHUNK)),
        lay((CHUNK, SGU_CH)),
    ]
    out_specs = [
        pl.BlockSpec((1, HEADS * QK_PAD, tt), lambda b, i: (b, 0, i)),
        pl.BlockSpec((1, tt, HEADS * QK_PAD), lambda b, i: (b, i, 0)),
        pl.BlockSpec((1, HEADS * V_EXT, tt), lambda b, i: (b, 0, i)),
        pl.BlockSpec((1, tt, CONV_CH), lambda b, i: (b, i, 0)),
        pl.BlockSpec((1, tt, SGU_CH), lambda b, i: (b, i, 0)),
    ]
    out_shape = [
        jax.ShapeDtypeStruct((bsz, HEADS * QK_PAD, ltot), BF16),
        jax.ShapeDtypeStruct((bsz, ltot, HEADS * QK_PAD), BF16),
        jax.ShapeDtypeStruct((bsz, HEADS * V_EXT, ltot), BF16),
        jax.ShapeDtypeStruct((bsz, ltot, CONV_CH), BF16),
        jax.ShapeDtypeStruct((bsz, ltot, SGU_CH), BF16),
    ]
    return pl.pallas_call(
        functools.partial(_pre_kernel, n_tiles=n_tiles),
        grid=(bsz, n_tiles),
        in_specs=in_specs,
        out_specs=out_specs,
        out_shape=out_shape,
        scratch_shapes=[pltpu.VMEM((tt + 2 * CONV_HALO, CONV_CH), F32),
                        pltpu.VMEM((tt + 2 * CONV_HALO - 8, CONV_CH), F32)],
        compiler_params=_cparams(("arbitrary", "arbitrary")),
        name="pre",
    )(xc, xc, xc, modsel, p["norm1_g"], p["w_in"], p["q_norm_g"], p["wqT"], p["wqrT"],
      p["kv_norm_g"], p["wk"], p["wvT"], p["cosk"], p["sink"], p["cosq"], p["sinq"],
      p["conv_w"], p["conv_b"], p["conv_ln_g"], p["conv_ln_b"], p["sgu_ln_g"], p["sgu_ln_b"],
      p["sgu_w"], p["sgu_bias"])


def _softmax_update(s, vt, m, acc):
    m_new = jnp.maximum(m, jnp.max(s, axis=0, keepdims=True))
    alpha = jnp.exp2(m - m_new)
    pr = jnp.exp2(s - m_new).astype(BF16)
    return m_new, alpha * acc + _dot(vt, pr)


def _softmax_finish(acc):
    return (acc[0:VDIM, :] / acc[VDIM:VDIM + 1, :]).astype(BF16)


def _attn_lat_kernel(qT_ref, k_ref, vT_ref, o_ref, acc_scr, *, n_lat):
    tq = qT_ref.shape[2]
    n_cb = tq // Q_BLOCK
    n_ctx = k_ref.shape[1] - n_lat
    n_steps = n_lat // KV_STEP

    def cols(c):
        return slice(c * Q_BLOCK, (c + 1) * Q_BLOCK)

    def scores(o, size, c):
        return _dot(k_ref[0, pl.ds(o, size), :], qT_ref[0, :, cols(c)])

    def run(blocks, carry):
        rs, excess = list(carry[0]), list(carry[1])
        s_next = scores(*blocks[0])
        for i, (o, size, c) in enumerate(blocks):
            s = s_next
            if i + 1 < len(blocks):
                s_next = scores(*blocks[i + 1])
            mx = jnp.max(s, axis=0, keepdims=True)
            pv = _dot(vT_ref[0, :, pl.ds(o, size)], jnp.exp2(s - rs[c]).astype(BF16))
            r_new = jnp.maximum(rs[c], mx)
            acc_scr[:, cols(c)] = (acc_scr[:, cols(c)] + pv) * jnp.exp2(rs[c] - r_new)
            excess[c] = jnp.maximum(excess[c], mx - rs[c])
            rs[c] = r_new
        return tuple(rs), tuple(excess)

    def latent_steps(j, carry):
        offs = [pl.multiple_of((j * KV_UNROLL + u) * KV_STEP, KV_STEP) for u in range(KV_UNROLL)]
        return run([(o, KV_STEP, c) for o in offs for c in range(n_cb)], carry)

    acc_scr[...] = jnp.zeros(acc_scr.shape, F32)
    carry = (tuple(jnp.max(scores(n_lat, R0_KEYS, c), axis=0, keepdims=True) for c in range(n_cb)),
             tuple(jnp.zeros((1, Q_BLOCK), F32) for _ in range(n_cb)))
    carry = run([(n_lat, n_ctx, c) for c in range(n_cb)], carry)
    _, excess = lax.fori_loop(0, n_steps // KV_UNROLL, latent_steps, carry)
    o_ref[0] = _softmax_finish(acc_scr[...])

    @pl.when(jnp.max(jnp.concatenate(excess, axis=1)) > STALE_MAX_HEADROOM)
    def _():
        def exact_step(o, size, carry):
            s = _dot(k_ref[0, pl.ds(o, size), :], qT_ref[0])
            return _softmax_update(s, vT_ref[0, :, pl.ds(o, size)], *carry)

        carry = lax.fori_loop(
            0, n_steps, lambda t, cr: exact_step(pl.multiple_of(t * KV_STEP, KV_STEP), KV_STEP, cr),
            (jnp.full((1, tq), -jnp.inf, F32), jnp.zeros((V_EXT, tq), F32)))
        _, acc = exact_step(n_lat, n_ctx, carry)
        o_ref[0] = _softmax_finish(acc)


def _attn_ctx_kernel(qT_ref, k_ref, vT_ref, a_hbm_ref, o_ref):
    del a_hbm_ref
    s = _dot(k_ref[0], qT_ref[0])
    tq = s.shape[1]
    _, acc = _softmax_update(s, vT_ref[0], jnp.full((1, tq), -jnp.inf, F32), jnp.zeros((V_EXT, tq), F32))
    o_ref[0] = _softmax_finish(acc)


def _attn_call(qT, k, vT, n_lat, with_ctx):
    bsz, _, ltot = qT.shape
    tq = Q_TILE
    n_ctx = ltot - n_lat
    assert n_lat % (KV_STEP * KV_UNROLL) == 0 and n_lat % tq == 0 and tq % Q_BLOCK == 0
    aT = pl.pallas_call(
        functools.partial(_attn_lat_kernel, n_lat=n_lat),
        grid=(bsz, HEADS, n_lat // tq),
        in_specs=[
            pl.BlockSpec((1, QK_PAD, tq), lambda b, h, q: (b, h, q)),
            pl.BlockSpec((1, ltot, QK_PAD), lambda b, h, q: (b, 0, h)),
            pl.BlockSpec((1, V_EXT, ltot), lambda b, h, q: (b, h, 0)),
        ],
        out_specs=pl.BlockSpec((1, VDIM, tq), lambda b, h, q: (b, h, q)),
        out_shape=jax.ShapeDtypeStruct((bsz, HEADS * VDIM, ltot), BF16),
        scratch_shapes=[pltpu.VMEM((V_EXT, tq), F32)],
        compiler_params=_cparams(("arbitrary", "arbitrary", "arbitrary")),
        name="attn_lat",
    )(qT, k, vT)
    if not with_ctx:
        return aT
    cb = n_lat // n_ctx
    return pl.pallas_call(
        _attn_ctx_kernel,
        grid=(bsz, HEADS),
        in_specs=[
            pl.BlockSpec((1, QK_PAD, n_ctx), lambda b, h: (b, h, cb)),
            pl.BlockSpec((1, n_ctx, QK_PAD), lambda b, h: (b, cb, h)),
            pl.BlockSpec((1, V_EXT, n_ctx), lambda b, h: (b, h, cb)),
            pl.BlockSpec(memory_space=pl.ANY),
        ],
        out_specs=pl.BlockSpec((1, VDIM, n_ctx), lambda b, h: (b, h, cb)),
        out_shape=jax.ShapeDtypeStruct(aT.shape, aT.dtype),
        input_output_aliases={3: 0},
        compiler_params=_cparams(("arbitrary", "arbitrary")),
        name="attn_ctx",
    )(qT, k, vT, aT)


def _post_kernel(x_ref, aT_ref, yc_ref, ys_ref, mod_ref, n2g_ref, wout_ref, w1_ref, w2_ref, fg_ref,
                 o_ref, *, final):
    d = x_ref.shape[2]
    d_ff = w1_ref.shape[2]
    n_attn = aT_ref.shape[1]
    g1 = mod_ref[0, 0, 2:3, :]
    sh2 = mod_ref[0, 0, 3:4, :]
    sc2 = mod_ref[0, 0, 4:5, :]
    g2 = mod_ref[0, 0, 5:6, :]
    y = _dot_tn(aT_ref[0], wout_ref[0, 0:n_attn, :])
    y = y + _dot(yc_ref[0], wout_ref[0, n_attn:n_attn + CONV_CH, :])
    y = y + _dot(ys_ref[0], wout_ref[0, n_attn + CONV_CH:d, :])
    x1 = x_ref[0] + g1 * y
    h = (_rms(x1, n2g_ref[0] * (1.0 + sc2)) + sh2).astype(BF16)
    ff = jnp.zeros(x1.shape, F32)
    for c in range(d_ff // FF_CHUNK):
        f = jnp.maximum(_dot(h, w1_ref[0, :, c * FF_CHUNK:(c + 1) * FF_CHUNK]), 0.0)
        ff = ff + _dot((f * f).astype(BF16), w2_ref[0, c * FF_CHUNK:(c + 1) * FF_CHUNK, :])
    x2 = x1 + g2 * ff
    if final:
        x2 = _rms(x2, fg_ref[...])
    o_ref[0] = x2


def _post_call(layer, xc, aT, yc, ys, modsel, p, n_lat_tiles, n_out_tiles, final):
    bsz, _, d = xc.shape
    tt = TOK_TILE
    d_ff = p["w_ff1"].shape[2]

    def lay(shape):
        nd = len(shape)
        return pl.BlockSpec((1,) + shape, lambda b, i: (layer,) + (0,) * nd,
                            pipeline_mode=pl.Buffered(1))

    return pl.pallas_call(
        functools.partial(_post_kernel, final=final),
        grid=(bsz, n_out_tiles),
        in_specs=[
            pl.BlockSpec((1, tt, d), lambda b, i: (b, i, 0)),
            pl.BlockSpec((1, HEADS * VDIM, tt), lambda b, i: (b, 0, i)),
            pl.BlockSpec((1, tt, CONV_CH), lambda b, i: (b, i, 0)),
            pl.BlockSpec((1, tt, SGU_CH), lambda b, i: (b, i, 0)),
            pl.BlockSpec((1, 1, 8, d), lambda b, i: (b, i // n_lat_tiles, 0, 0)),
            pl.BlockSpec((1, 1, d), lambda b, i: (layer, 0, 0)),
            lay((d, d)),
            lay((d, d_ff)),
            lay((d_ff, d)),
            pl.BlockSpec((1, d), lambda b, i: (0, 0)),
        ],
        out_specs=pl.BlockSpec((1, tt, d), lambda b, i: (b, i, 0)),
        out_shape=jax.ShapeDtypeStruct((bsz, n_out_tiles * tt, d), F32),
        compiler_params=_cparams(("arbitrary", "arbitrary")),
        name="post",
    )(xc, aT, yc, ys, modsel, p["norm2_g"], p["w_out"], p["w_ff1"], p["w_ff2"], p["final_g"])


def _rotate_half_cols(w):
    n = ROPE // 4
    return jnp.concatenate([-w[..., n:2 * n], w[..., 0:n], -w[..., 3 * n:4 * n], w[..., 2 * n:3 * n]], -1)


def _prepare(n_lat, n_ctx, norm1_g, norm2_g, w_in, q_norm_g, w_uq, kv_norm_g, w_ukv, conv_w, conv_b,
             conv_ln_g, conv_ln_b, sgu_ln_g, sgu_ln_b, sgu_w, sgu_b, w_out, w_ff1, w_ff2, final_g):
    n_layers, d, _ = w_in.shape
    o_q, o_kv, o_kr = 0, Q_RANK, Q_RANK + KV_RANK
    o_conv = o_kr + ROPE
    o_sgu = o_conv + 2 * CONV_CH
    w_kr = w_in[:, :, o_kr:o_kr + ROPE]

    def placed(w):
        return jnp.pad(w, ((0, 0), (0, 0), (NOPE, QK_PAD - NOPE - ROPE)))

    w_in_x = jnp.concatenate([
        w_in[:, :, o_q:o_q + Q_RANK], w_in[:, :, o_kv:o_kv + KV_RANK],
        w_in[:, :, o_conv:o_conv + 2 * CONV_CH], w_in[:, :, o_sgu:o_sgu + 2 * SGU_CH],
        placed(w_kr), placed(_rotate_half_cols(w_kr))], -1).astype(BF16)

    wq = w_uq.reshape(n_layers, Q_RANK, HEADS, NOPE + ROPE)
    wq_pad = jnp.pad(wq, ((0, 0), (0, 0), (0, 0), (0, QK_PAD - NOPE - ROPE)))
    wqT = wq_pad.reshape(n_layers, Q_RANK, HEADS * QK_PAD).transpose(0, 2, 1).astype(BF16)
    wqr = _rotate_half_cols(wq[..., NOPE:])
    wqrT = wqr.reshape(n_layers, Q_RANK, HEADS * ROPE).transpose(0, 2, 1).astype(BF16)

    wkv = w_ukv.reshape(n_layers, KV_RANK, HEADS, NOPE + VDIM)
    wk = jnp.pad(wkv[..., :NOPE], ((0, 0), (0, 0), (0, 0), (0, QK_PAD - NOPE)))
    wk = wk.reshape(n_layers, KV_RANK, HEADS * QK_PAD).astype(BF16)
    wvT = wkv[..., NOPE:].reshape(n_layers, KV_RANK, HEADS * VDIM).transpose(0, 2, 1).astype(BF16)

    t = jnp.arange(n_lat, dtype=jnp.int32)
    n = ROPE // 4
    inv = 1.0 / (ROPE_BASE ** (jnp.arange(n, dtype=F32) / n))
    ang_r = (t // GRID_W).astype(F32)[:, None] * inv
    ang_c = (t % GRID_W).astype(F32)[:, None] * inv
    cos32 = jnp.concatenate([jnp.cos(ang_r)] * 2 + [jnp.cos(ang_c)] * 2, -1)
    sin32 = jnp.concatenate([jnp.sin(ang_r)] * 2 + [jnp.sin(ang_c)] * 2, -1)
    cos32 = jnp.concatenate([cos32, jnp.ones((n_ctx, ROPE), F32)], 0)
    sin32 = jnp.concatenate([sin32, jnp.zeros((n_ctx, ROPE), F32)], 0)
    pad = ((0, 0), (NOPE, QK_PAD - NOPE - ROPE))

    sgu_bias = jnp.broadcast_to(jnp.swapaxes(sgu_b, 1, 2)[:, :, :, None],
                                (n_layers, CHUNK, SGU_HEADS, SGU_HEAD_DIM)).reshape(n_layers, CHUNK, SGU_CH)

    def row(a):
        return a.reshape(n_layers, 1, a.shape[-1])

    return dict(
        norm1_g=row(norm1_g), norm2_g=row(norm2_g), w_in=w_in_x, q_norm_g=row(q_norm_g), wqT=wqT, wqrT=wqrT,
        kv_norm_g=row(kv_norm_g), wk=wk, wvT=wvT,
        cosk=jnp.pad(cos32, pad), sink=jnp.pad(sin32, pad), cosq=cos32.T, sinq=sin32.T,
        conv_w=conv_w, conv_b=row(conv_b), conv_ln_g=row(conv_ln_g), conv_ln_b=row(conv_ln_b),
        sgu_ln_g=row(sgu_ln_g), sgu_ln_b=row(sgu_ln_b), sgu_w=sgu_w.astype(BF16), sgu_bias=sgu_bias,
        w_out=w_out.astype(BF16), w_ff1=w_ff1.astype(BF16), w_ff2=w_ff2.astype(BF16),
        final_g=final_g.reshape(1, -1))


def kernel(x, c, ctx, c_ctx, ada_w, ada_b, norm1_g, norm2_g, w_in, q_norm_g, w_uq, kv_norm_g, w_ukv,
           conv_w, conv_b, conv_ln_g, conv_ln_b, sgu_ln_g, sgu_ln_b, sgu_w, sgu_b, w_out, w_ff1, w_ff2,
           final_g):
    bsz, n_lat, d = x.shape
    n_ctx = ctx.shape[1]
    n_layers = w_in.shape[0]
    assert n_ctx == TOK_TILE and n_lat % TOK_TILE == 0 and n_lat % GRID_W == 0
    n_lat_tiles = n_lat // TOK_TILE
    n_tiles = n_lat_tiles + 1

    p = _prepare(n_lat, n_ctx, norm1_g, norm2_g, w_in, q_norm_g, w_uq, kv_norm_g, w_ukv, conv_w, conv_b,
                 conv_ln_g, conv_ln_b, sgu_ln_g, sgu_ln_b, sgu_w, sgu_b, w_out, w_ff1, w_ff2, final_g)

    rows = -(-(bsz + 1) // 8) * 8
    cvec = jnp.concatenate([c, c_ctx[None, :], jnp.zeros((rows - bsz - 1, d), F32)], 0)
    mods = _ada_call(cvec, ada_w, ada_b).reshape(n_layers, rows, 6, d)
    m_lat = mods[:, :bsz]
    m_ctx = jnp.broadcast_to(mods[:, bsz][:, None], m_lat.shape)
    modsel = jnp.pad(jnp.stack([m_lat, m_ctx], 2), ((0, 0), (0, 0), (0, 0), (0, 2), (0, 0)))

    xc = jnp.concatenate([x, ctx], 1)
    for layer in range(n_layers):
        last = layer == n_layers - 1
        n_act = n_lat_tiles if last else n_tiles
        qT, k, vT, yc, ys = _pre_call(layer, xc, modsel[layer], p, n_lat_tiles)
        aT = _attn_call(qT, k, vT, n_lat, not last)
        xc = _post_call(layer, xc, aT, yc, ys, modsel[layer], p, n_lat_tiles, n_act, last)
    return xc
```

```python
import functools
import math

import jax
import jax.numpy as jnp
from jax import lax
from jax.experimental import pallas as pl
from jax.experimental.pallas import tpu as pltpu

F32 = jnp.float32
BF16 = jnp.bfloat16

GRID_W = 64
HEADS = 8
NOPE = 64
ROPE = 32
VDIM = 64
QK_PAD = 128
Q_RANK = 384
KV_RANK = 256
CONV_CH = 256
CONV_WIDTH = 31
CONV_HALO = 16
SGU_HEADS = 4
SGU_HEAD_DIM = 64
SGU_CH = 256
CHUNK = 128
EPS = 1e-6
ROPE_BASE = 10000.0
ATTN_SCALE = (NOPE + ROPE) ** -0.5
LOG2E = math.log2(math.e)

TOK_TILE = 256
Q_TILE = 2048
Q_BLOCK = 512
KV_STEP = 512
KV_UNROLL = 8
R0_KEYS = 16
STALE_MAX_HEADROOM = 64.0
V_EXT = VDIM + 16
V_ROWS = 128
FF_CHUNK = 1024
ADA_COLS = 1536
VMEM_LIMIT_BYTES = 56 * 1024 * 1024

ZC_Q = 0
ZC_KV = ZC_Q + Q_RANK
ZC_CONV = ZC_KV + KV_RANK
ZC_SGU = ZC_CONV + 2 * CONV_CH
ZC_KR = ZC_SGU + 2 * SGU_CH
ZC_KRR = ZC_KR + QK_PAD
Z_COLS = ZC_KRR + QK_PAD


def _cparams(sem):
    return pltpu.CompilerParams(dimension_semantics=sem, vmem_limit_bytes=VMEM_LIMIT_BYTES)


def _sigmoid(v):
    return 1.0 / (1.0 + jnp.exp(-v))


def _layernorm(v, g, b):
    mu = jnp.mean(v, -1, keepdims=True)
    d = v - mu
    var = jnp.mean(d * d, -1, keepdims=True)
    return d * lax.rsqrt(var + EPS) * g + b


def _rms(v, g):
    return v * lax.rsqrt(jnp.mean(v * v, -1, keepdims=True) + EPS) * g


def _dot(a, b):
    return jnp.dot(a, b, preferred_element_type=F32)


def _dot_nt(a, b):
    return lax.dot_general(a, b, (((1,), (1,)), ((), ())), preferred_element_type=F32)


def _dot_tn(a, b):
    return lax.dot_general(a, b, (((0,), (0,)), ((), ())), preferred_element_type=F32)


def _ada_kernel(c_ref, w_ref, b_ref, o_ref):
    c = c_ref[...]
    s = (c * _sigmoid(c)).astype(BF16)
    o_ref[0] = _dot(s, w_ref[0].astype(BF16)) + b_ref[0]


def _ada_call(cvec, ada_w, ada_b):
    n_layers, d, n6 = ada_w.shape
    rows = cvec.shape[0]
    return pl.pallas_call(
        _ada_kernel,
        grid=(n_layers, n6 // ADA_COLS),
        in_specs=[
            pl.BlockSpec((rows, d), lambda l, j: (0, 0)),
            pl.BlockSpec((1, d, ADA_COLS), lambda l, j: (l, 0, j)),
            pl.BlockSpec((1, 1, ADA_COLS), lambda l, j: (l, 0, j)),
        ],
        out_specs=pl.BlockSpec((1, rows, ADA_COLS), lambda l, j: (l, 0, j)),
        out_shape=jax.ShapeDtypeStruct((n_layers, rows, n6), F32),
        compiler_params=_cparams(("arbitrary", "arbitrary")),
        name="adaln",
    )(cvec, ada_w, ada_b.reshape(n_layers, 1, n6))


def _pre_kernel(x_ref, xp_ref, xn_ref, mod_ref, n1g_ref, win_ref, qg_ref, wqT_ref, wqrT_ref,
                kvg_ref, wk_ref, wvT_ref, cosk_ref, sink_ref, cosq_ref, sinq_ref,
                convw_ref, convb_ref, clng_ref, clnb_ref, slng_ref, slnb_ref, sguw_ref, sgub_ref,
                qT_out, k_out, vT_out, yc_out, ys_out, ybuf, yshift, *, n_tiles):
    i = pl.program_id(1)
    tt = x_ref.shape[1]
    sh1 = mod_ref[0, 0, 0:1, :]
    gain1 = n1g_ref[0] * (1.0 + mod_ref[0, 0, 1:2, :])

    def normmod(xt):
        return (_rms(xt, gain1) + sh1).astype(BF16)

    z = _dot(normmod(x_ref[0]), win_ref[0])

    qn = _rms(z[:, ZC_Q:ZC_Q + Q_RANK], qg_ref[0]).astype(BF16)
    qT = _dot_nt(wqT_ref[0], qn)
    qrT = _dot_nt(wqrT_ref[0], qn)
    cq = cosq_ref[...]
    sq = sinq_ref[...]
    qscale = ATTN_SCALE * LOG2E
    for h in range(HEADS):
        r0 = h * QK_PAD
        qT_out[0, r0:r0 + NOPE, :] = (qT[r0:r0 + NOPE, :] * qscale).astype(BF16)
        rot = qT[r0 + NOPE:r0 + NOPE + ROPE, :] * cq + qrT[h * ROPE:(h + 1) * ROPE, :] * sq
        qT_out[0, r0 + NOPE:r0 + NOPE + ROPE, :] = (rot * qscale).astype(BF16)
        qT_out[0, r0 + NOPE + ROPE:r0 + QK_PAD, :] = jnp.zeros((QK_PAD - NOPE - ROPE, tt), BF16)

    kvn = _rms(z[:, ZC_KV:ZC_KV + KV_RANK], kvg_ref[0]).astype(BF16)
    kext = _dot(kvn, wk_ref[0])
    krope = z[:, ZC_KR:ZC_KR + QK_PAD] * cosk_ref[...] + z[:, ZC_KRR:ZC_KRR + QK_PAD] * sink_ref[...]
    for h in range(HEADS):
        c0 = h * QK_PAD
        k_out[0, :, c0:c0 + QK_PAD] = (kext[:, c0:c0 + QK_PAD] + krope).astype(BF16)
    vT = _dot_nt(wvT_ref[0], kvn)
    for h in range(HEADS):
        vT_out[0, h * V_ROWS:h * V_ROWS + VDIM, :] = vT[h * VDIM:(h + 1) * VDIM, :].astype(BF16)
        vT_out[0, h * V_ROWS + VDIM:h * V_ROWS + V_EXT, :] = jnp.ones((V_EXT - VDIM, tt), BF16)
        vT_out[0, h * V_ROWS + V_EXT:(h + 1) * V_ROWS, :] = jnp.zeros((V_ROWS - V_EXT, tt), BF16)

    def glu(zc):
        return zc[:, :CONV_CH] * _sigmoid(zc[:, CONV_CH:])

    wconv_in = win_ref[0, :, ZC_CONV:ZC_CONV + 2 * CONV_CH]
    first = jnp.logical_or(i == 0, i == n_tiles - 1)
    last = i >= n_tiles - 2
    yp = glu(_dot(normmod(xp_ref[0]), wconv_in))
    yn = glu(_dot(normmod(xn_ref[0]), wconv_in))
    ybuf[0:CONV_HALO, :] = jnp.where(first, 0.0, yp)
    ybuf[CONV_HALO:CONV_HALO + tt, :] = glu(z[:, ZC_CONV:ZC_CONV + 2 * CONV_CH])
    ybuf[CONV_HALO + tt:, :] = jnp.where(last, 0.0, yn)
    cw = convw_ref[0]
    acc = jnp.zeros((tt, CONV_CH), F32)
    n_a = (CONV_WIDTH + 8) // 8
    for b in range(8):
        src = ybuf
        if b:
            yshift[...] = ybuf[b:b + yshift.shape[0], :]
            src = yshift
        for a in range(n_a):
            k = 8 * a + b - 1
            if 0 <= k < CONV_WIDTH:
                acc = acc + src[8 * a:8 * a + tt, :] * cw[k:k + 1, :]
    yc = _layernorm(acc + convb_ref[0], clng_ref[0], clnb_ref[0])
    yc_out[0] = (yc * _sigmoid(yc)).astype(BF16)

    zs = z[:, ZC_SGU:ZC_SGU + 2 * SGU_CH]
    zg = 0.5 * zs * (1.0 + jnp.tanh(math.sqrt(2.0 / math.pi) * (zs + 0.044715 * (zs * zs * zs))))
    u = zg[:, :SGU_CH]
    v = _layernorm(zg[:, SGU_CH:], slng_ref[0], slnb_ref[0]).astype(BF16)
    lane = lax.broadcasted_iota(jnp.int32, (CHUNK, SGU_CH), 1)
    bias = sgub_ref[0]
    for c in range(tt // CHUNK):
        vc = v[c * CHUNK:(c + 1) * CHUNK, :]
        mixed = _dot(sguw_ref[0, 0], vc)
        for h in range(1, SGU_HEADS):
            mixed = jnp.where(lane >= h * SGU_HEAD_DIM, _dot(sguw_ref[0, h], vc), mixed)
        ys_out[0, c * CHUNK:(c + 1) * CHUNK, :] = (u[c * CHUNK:(c + 1) * CHUNK, :] * (mixed + bias)).astype(BF16)


def _pre_call(layer, xc, modsel, p, n_lat_tiles):
    bsz, ltot, d = xc.shape
    tt = TOK_TILE
    n_tiles = ltot // tt
    hb = tt // CONV_HALO
    n_hblocks = ltot // CONV_HALO

    def lay(shape):
        nd = len(shape)
        return pl.BlockSpec((1,) + shape, lambda b, i: (layer,) + (0,) * nd)

    in_specs = [
        pl.BlockSpec((1, tt, d), lambda b, i: (b, i, 0)),
        pl.BlockSpec((1, CONV_HALO, d), lambda b, i: (b, jnp.maximum(i * hb - 1, 0), 0)),
        pl.BlockSpec((1, CONV_HALO, d), lambda b, i: (b, jnp.minimum((i + 1) * hb, n_hblocks - 1), 0)),
        pl.BlockSpec((1, 1, 8, d), lambda b, i: (b, i // n_lat_tiles, 0, 0)),
        lay((1, d)),
        lay((d, Z_COLS)),
        lay((1, Q_RANK)),
        lay((HEADS * QK_PAD, Q_RANK)),
        lay((HEADS * ROPE, Q_RANK)),
        lay((1, KV_RANK)),
        lay((KV_RANK, HEADS * QK_PAD)),
        lay((HEADS * VDIM, KV_RANK)),
        pl.BlockSpec((tt, QK_PAD), lambda b, i: (i, 0)),
        pl.BlockSpec((tt, QK_PAD), lambda b, i: (i, 0)),
        pl.BlockSpec((ROPE, tt), lambda b, i: (0, i)),
        pl.BlockSpec((ROPE, tt), lambda b, i: (0, i)),
        lay((CONV_WIDTH, CONV_CH)),
        lay((1, CONV_CH)),
        lay((1, CONV_CH)),
        lay((1, CONV_CH)),
        lay((1, SGU_CH)),
        lay((1, SGU_CH)),
        lay((SGU_HEADS, CHUNK, CHUNK)),
        lay((CHUNK, SGU_CH)),
    ]
    out_specs = [
        pl.BlockSpec((1, HEADS * QK_PAD, tt), lambda b, i: (b, 0, i)),
        pl.BlockSpec((1, tt, HEADS * QK_PAD), lambda b, i: (b, i, 0)),
        pl.BlockSpec((1, HEADS * V_ROWS, tt), lambda b, i: (b, 0, i)),
        pl.BlockSpec((1, tt, CONV_CH), lambda b, i: (b, i, 0)),
        pl.BlockSpec((1, tt, SGU_CH), lambda b, i: (b, i, 0)),
    ]
    out_shape = [
        jax.ShapeDtypeStruct((bsz, HEADS * QK_PAD, ltot), BF16),
        jax.ShapeDtypeStruct((bsz, ltot, HEADS * QK_PAD), BF16),
        jax.ShapeDtypeStruct((bsz, HEADS * V_ROWS, ltot), BF16),
        jax.ShapeDtypeStruct((bsz, ltot, CONV_CH), BF16),
        jax.ShapeDtypeStruct((bsz, ltot, SGU_CH), BF16),
    ]
    return pl.pallas_call(
        functools.partial(_pre_kernel, n_tiles=n_tiles),
        grid=(bsz, n_tiles),
        in_specs=in_specs,
        out_specs=out_specs,
        out_shape=out_shape,
        scratch_shapes=[pltpu.VMEM((tt + 2 * CONV_HALO, CONV_CH), F32),
                        pltpu.VMEM((tt + 2 * CONV_HALO - 8, CONV_CH), F32)],
        compiler_params=_cparams(("arbitrary", "arbitrary")),
        name="pre",
    )(xc, xc, xc, modsel, p["norm1_g"], p["w_in"], p["q_norm_g"], p["wqT"], p["wqrT"],
      p["kv_norm_g"], p["wk"], p["wvT"], p["cosk"], p["sink"], p["cosq"], p["sinq"],
      p["conv_w"], p["conv_b"], p["conv_ln_g"], p["conv_ln_b"], p["sgu_ln_g"], p["sgu_ln_b"],
      p["sgu_w"], p["sgu_bias"])


def _softmax_update(s, vt, m, acc):
    m_new = jnp.maximum(m, jnp.max(s, axis=0, keepdims=True))
    alpha = jnp.exp2(m - m_new)
    pr = jnp.exp2(s - m_new).astype(BF16)
    return m_new, alpha * acc + _dot(vt, pr)


def _softmax_finish(acc):
    return (acc[0:VDIM, :] / acc[VDIM:VDIM + 1, :]).astype(BF16)


def _attn_lat_kernel(qT_ref, k_ref, vT_ref, o_ref, acc_scr, *, n_lat):
    tq = qT_ref.shape[2]
    n_cb = tq // Q_BLOCK
    n_ctx = k_ref.shape[1] - n_lat
    n_steps = n_lat // KV_STEP

    def cols(c):
        return slice(c * Q_BLOCK, (c + 1) * Q_BLOCK)

    def scores(o, size, c):
        return _dot(k_ref[0, pl.ds(o, size), :], qT_ref[0, :, cols(c)])

    def run(blocks, carry):
        rs, excess = list(carry[0]), list(carry[1])
        s_next = scores(*blocks[0])
        for i, (o, size, c) in enumerate(blocks):
            s = s_next
            if i + 1 < len(blocks):
                s_next = scores(*blocks[i + 1])
            mx = jnp.max(s, axis=0, keepdims=True)
            pv = _dot(vT_ref[0, :, pl.ds(o, size)], jnp.exp2(s - rs[c]).astype(BF16))[0:V_EXT, :]
            r_new = jnp.maximum(rs[c], mx)
            acc_scr[:, cols(c)] = (acc_scr[:, cols(c)] + pv) * jnp.exp2(rs[c] - r_new)
            excess[c] = jnp.maximum(excess[c], mx - rs[c])
            rs[c] = r_new
        return tuple(rs), tuple(excess)

    def latent_steps(j, carry):
        offs = [pl.multiple_of((j * KV_UNROLL + u) * KV_STEP, KV_STEP) for u in range(KV_UNROLL)]
        return run([(o, KV_STEP, c) for o in offs for c in range(n_cb)], carry)

    acc_scr[...] = jnp.zeros(acc_scr.shape, F32)
    carry = (tuple(jnp.max(scores(n_lat, R0_KEYS, c), axis=0, keepdims=True) for c in range(n_cb)),
             tuple(jnp.zeros((1, Q_BLOCK), F32) for _ in range(n_cb)))
    carry = run([(n_lat, n_ctx, c) for c in range(n_cb)], carry)
    _, excess = lax.fori_loop(0, n_steps // KV_UNROLL, latent_steps, carry)
    o_ref[0] = _softmax_finish(acc_scr[...])

    @pl.when(jnp.max(jnp.concatenate(excess, axis=1)) > STALE_MAX_HEADROOM)
    def _():
        def exact_step(o, size, carry):
            s = _dot(k_ref[0, pl.ds(o, size), :], qT_ref[0])
            return _softmax_update(s, vT_ref[0, 0:V_EXT, pl.ds(o, size)], *carry)

        carry = lax.fori_loop(
            0, n_steps, lambda t, cr: exact_step(pl.multiple_of(t * KV_STEP, KV_STEP), KV_STEP, cr),
            (jnp.full((1, tq), -jnp.inf, F32), jnp.zeros((V_EXT, tq), F32)))
        _, acc = exact_step(n_lat, n_ctx, carry)
        o_ref[0] = _softmax_finish(acc)


def _attn_ctx_kernel(qT_ref, k_ref, vT_ref, a_hbm_ref, o_ref):
    del a_hbm_ref
    s = _dot(k_ref[0], qT_ref[0])
    tq = s.shape[1]
    _, acc = _softmax_update(s, vT_ref[0, 0:V_EXT, :], jnp.full((1, tq), -jnp.inf, F32),
                             jnp.zeros((V_EXT, tq), F32))
    o_ref[0] = _softmax_finish(acc)


def _attn_call(qT, k, vT, n_lat, with_ctx):
    bsz, _, ltot = qT.shape
    tq = Q_TILE
    n_ctx = ltot - n_lat
    assert n_lat % (KV_STEP * KV_UNROLL) == 0 and n_lat % tq == 0 and tq % Q_BLOCK == 0
    aT = pl.pallas_call(
        functools.partial(_attn_lat_kernel, n_lat=n_lat),
        grid=(bsz, HEADS, n_lat // tq),
        in_specs=[
            pl.BlockSpec((1, QK_PAD, tq), lambda b, h, q: (b, h, q)),
            pl.BlockSpec((1, ltot, QK_PAD), lambda b, h, q: (b, 0, h)),
            pl.BlockSpec((1, V_ROWS, ltot), lambda b, h, q: (b, h, 0)),
        ],
        out_specs=pl.BlockSpec((1, VDIM, tq), lambda b, h, q: (b, h, q)),
        out_shape=jax.ShapeDtypeStruct((bsz, HEADS * VDIM, ltot), BF16),
        scratch_shapes=[pltpu.VMEM((V_EXT, tq), F32)],
        compiler_params=_cparams(("arbitrary", "arbitrary", "arbitrary")),
        name="attn_lat",
    )(qT, k, vT)
    if not with_ctx:
        return aT
    cb = n_lat // n_ctx
    return pl.pallas_call(
        _attn_ctx_kernel,
        grid=(bsz, HEADS),
        in_specs=[
            pl.BlockSpec((1, QK_PAD, n_ctx), lambda b, h: (b, h, cb)),
            pl.BlockSpec((1, n_ctx, QK_PAD), lambda b, h: (b, cb, h)),
            pl.BlockSpec((1, V_ROWS, n_ctx), lambda b, h: (b, h, cb)),
            pl.BlockSpec(memory_space=pl.ANY),
        ],
        out_specs=pl.BlockSpec((1, VDIM, n_ctx), lambda b, h: (b, h, cb)),
        out_shape=jax.ShapeDtypeStruct(aT.shape, aT.dtype),
        input_output_aliases={3: 0},
        compiler_params=_cparams(("arbitrary", "arbitrary")),
        name="attn_ctx",
    )(qT, k, vT, aT)


def _post_kernel(x_ref, aT_ref, yc_ref, ys_ref, mod_ref, n2g_ref, wout_ref, w1_ref, w2_ref, fg_ref,
                 o_ref, *, final):
    d = x_ref.shape[2]
    d_ff = w1_ref.shape[2]
    n_attn = aT_ref.shape[1]
    g1 = mod_ref[0, 0, 2:3, :]
    sh2 = mod_ref[0, 0, 3:4, :]
    sc2 = mod_ref[0, 0, 4:5, :]
    g2 = mod_ref[0, 0, 5:6, :]
    y = _dot_tn(aT_ref[0], wout_ref[0, 0:n_attn, :])
    y = y + _dot(yc_ref[0], wout_ref[0, n_attn:n_attn + CONV_CH, :])
    y = y + _dot(ys_ref[0], wout_ref[0, n_attn + CONV_CH:d, :])
    x1 = x_ref[0] + g1 * y
    h = (_rms(x1, n2g_ref[0] * (1.0 + sc2)) + sh2).astype(BF16)
    ff = jnp.zeros(x1.shape, F32)
    for c in range(d_ff // FF_CHUNK):
        f = jnp.maximum(_dot(h, w1_ref[0, :, c * FF_CHUNK:(c + 1) * FF_CHUNK]), 0.0)
        ff = ff + _dot((f * f).astype(BF16), w2_ref[0, c * FF_CHUNK:(c + 1) * FF_CHUNK, :])
    x2 = x1 + g2 * ff
    if final:
        x2 = _rms(x2, fg_ref[...])
    o_ref[0] = x2


def _post_call(layer, xc, aT, yc, ys, modsel, p, n_lat_tiles, n_out_tiles, final):
    bsz, _, d = xc.shape
    tt = TOK_TILE
    d_ff = p["w_ff1"].shape[2]

    def lay(shape):
        nd = len(shape)
        return pl.BlockSpec((1,) + shape, lambda b, i: (layer,) + (0,) * nd,
                            pipeline_mode=pl.Buffered(1))

    return pl.pallas_call(
        functools.partial(_post_kernel, final=final),
        grid=(bsz, n_out_tiles),
        in_specs=[
            pl.BlockSpec((1, tt, d), lambda b, i: (b, i, 0)),
            pl.BlockSpec((1, HEADS * VDIM, tt), lambda b, i: (b, 0, i)),
            pl.BlockSpec((1, tt, CONV_CH), lambda b, i: (b, i, 0)),
            pl.BlockSpec((1, tt, SGU_CH), lambda b, i: (b, i, 0)),
            pl.BlockSpec((1, 1, 8, d), lambda b, i: (b, i // n_lat_tiles, 0, 0)),
            pl.BlockSpec((1, 1, d), lambda b, i: (layer, 0, 0)),
            lay((d, d)),
            lay((d, d_ff)),
            lay((d_ff, d)),
            pl.BlockSpec((1, d), lambda b, i: (0, 0)),
        ],
        out_specs=pl.BlockSpec((1, tt, d), lambda b, i: (b, i, 0)),
        out_shape=jax.ShapeDtypeStruct((bsz, n_out_tiles * tt, d), F32),
        compiler_params=_cparams(("arbitrary", "arbitrary")),
        name="post",
    )(xc, aT, yc, ys, modsel, p["norm2_g"], p["w_out"], p["w_ff1"], p["w_ff2"], p["final_g"])


def _rotate_half_cols(w):
    n = ROPE // 4
    return jnp.concatenate([-w[..., n:2 * n], w[..., 0:n], -w[..., 3 * n:4 * n], w[..., 2 * n:3 * n]], -1)


def _prepare(n_lat, n_ctx, norm1_g, norm2_g, w_in, q_norm_g, w_uq, kv_norm_g, w_ukv, conv_w, conv_b,
             conv_ln_g, conv_ln_b, sgu_ln_g, sgu_ln_b, sgu_w, sgu_b, w_out, w_ff1, w_ff2, final_g):
    n_layers, d, _ = w_in.shape
    o_q, o_kv, o_kr = 0, Q_RANK, Q_RANK + KV_RANK
    o_conv = o_kr + ROPE
    o_sgu = o_conv + 2 * CONV_CH
    w_kr = w_in[:, :, o_kr:o_kr + ROPE]

    def placed(w):
        return jnp.pad(w, ((0, 0), (0, 0), (NOPE, QK_PAD - NOPE - ROPE)))

    w_in_x = jnp.concatenate([
        w_in[:, :, o_q:o_q + Q_RANK], w_in[:, :, o_kv:o_kv + KV_RANK],
        w_in[:, :, o_conv:o_conv + 2 * CONV_CH], w_in[:, :, o_sgu:o_sgu + 2 * SGU_CH],
        placed(w_kr), placed(_rotate_half_cols(w_kr))], -1).astype(BF16)

    wq = w_uq.reshape(n_layers, Q_RANK, HEADS, NOPE + ROPE)
    wq_pad = jnp.pad(wq, ((0, 0), (0, 0), (0, 0), (0, QK_PAD - NOPE - ROPE)))
    wqT = wq_pad.reshape(n_layers, Q_RANK, HEADS * QK_PAD).transpose(0, 2, 1).astype(BF16)
    wqr = _rotate_half_cols(wq[..., NOPE:])
    wqrT = wqr.reshape(n_layers, Q_RANK, HEADS * ROPE).transpose(0, 2, 1).astype(BF16)

    wkv = w_ukv.reshape(n_layers, KV_RANK, HEADS, NOPE + VDIM)
    wk = jnp.pad(wkv[..., :NOPE], ((0, 0), (0, 0), (0, 0), (0, QK_PAD - NOPE)))
    wk = wk.reshape(n_layers, KV_RANK, HEADS * QK_PAD).astype(BF16)
    wvT = wkv[..., NOPE:].reshape(n_layers, KV_RANK, HEADS * VDIM).transpose(0, 2, 1).astype(BF16)

    t = jnp.arange(n_lat, dtype=jnp.int32)
    n = ROPE // 4
    inv = 1.0 / (ROPE_BASE ** (jnp.arange(n, dtype=F32) / n))
    ang_r = (t // GRID_W).astype(F32)[:, None] * inv
    ang_c = (t % GRID_W).astype(F32)[:, None] * inv
    cos32 = jnp.concatenate([jnp.cos(ang_r)] * 2 + [jnp.cos(ang_c)] * 2, -1)
    sin32 = jnp.concatenate([jnp.sin(ang_r)] * 2 + [jnp.sin(ang_c)] * 2, -1)
    cos32 = jnp.concatenate([cos32, jnp.ones((n_ctx, ROPE), F32)], 0)
    sin32 = jnp.concatenate([sin32, jnp.zeros((n_ctx, ROPE), F32)], 0)
    pad = ((0, 0), (NOPE, QK_PAD - NOPE - ROPE))

    sgu_bias = jnp.broadcast_to(jnp.swapaxes(sgu_b, 1, 2)[:, :, :, None],
                                (n_layers, CHUNK, SGU_HEADS, SGU_HEAD_DIM)).reshape(n_layers, CHUNK, SGU_CH)

    def row(a):
        return a.reshape(n_layers, 1, a.shape[-1])

    return dict(
        norm1_g=row(norm1_g), norm2_g=row(norm2_g), w_in=w_in_x, q_norm_g=row(q_norm_g), wqT=wqT, wqrT=wqrT,
        kv_norm_g=row(kv_norm_g), wk=wk, wvT=wvT,
        cosk=jnp.pad(cos32, pad), sink=jnp.pad(sin32, pad), cosq=cos32.T, sinq=sin32.T,
        conv_w=conv_w, conv_b=row(conv_b), conv_ln_g=row(conv_ln_g), conv_ln_b=row(conv_ln_b),
        sgu_ln_g=row(sgu_ln_g), sgu_ln_b=row(sgu_ln_b), sgu_w=sgu_w.astype(BF16), sgu_bias=sgu_bias,
        w_out=w_out.astype(BF16), w_ff1=w_ff1.astype(BF16), w_ff2=w_ff2.astype(BF16),
        final_g=final_g.reshape(1, -1))


def kernel(x, c, ctx, c_ctx, ada_w, ada_b, norm1_g, norm2_g, w_in, q_norm_g, w_uq, kv_norm_g, w_ukv,
           conv_w, conv_b, conv_ln_g, conv_ln_b, sgu_ln_g, sgu_ln_b, sgu_w, sgu_b, w_out, w_ff1, w_ff2,
           final_g):
    bsz, n_lat, d = x.shape
    n_ctx = ctx.shape[1]
    n_layers = w_in.shape[0]
    assert n_ctx == TOK_TILE and n_lat % TOK_TILE == 0 and n_lat % GRID_W == 0
    n_lat_tiles = n_lat // TOK_TILE
    n_tiles = n_lat_tiles + 1

    p = _prepare(n_lat, n_ctx, norm1_g, norm2_g, w_in, q_norm_g, w_uq, kv_norm_g, w_ukv, conv_w, conv_b,
                 conv_ln_g, conv_ln_b, sgu_ln_g, sgu_ln_b, sgu_w, sgu_b, w_out, w_ff1, w_ff2, final_g)

    rows = -(-(bsz + 1) // 8) * 8
    cvec = jnp.concatenate([c, c_ctx[None, :], jnp.zeros((rows - bsz - 1, d), F32)], 0)
    mods = _ada_call(cvec, ada_w, ada_b).reshape(n_layers, rows, 6, d)
    m_lat = mods[:, :bsz]
    m_ctx = jnp.broadcast_to(mods[:, bsz][:, None], m_lat.shape)
    modsel = jnp.pad(jnp.stack([m_lat, m_ctx], 2), ((0, 0), (0, 0), (0, 0), (0, 2), (0, 0)))

    xc = jnp.concatenate([x, ctx], 1)
    for layer in range(n_layers):
        last = layer == n_layers - 1
        n_act = n_lat_tiles if last else n_tiles
        qT, k, vT, yc, ys = _pre_call(layer, xc, modsel[layer], p, n_lat_tiles)
        aT = _attn_call(qT, k, vT, n_lat, not last)
        xc = _post_call(layer, xc, aT, yc, ys, modsel[layer], p, n_lat_tiles, n_act, last)
    return xc
```

```python
import functools
import math

import jax
import jax.numpy as jnp
from jax import lax
from jax.experimental import pallas as pl
from jax.experimental.pallas import tpu as pltpu

F32 = jnp.float32
BF16 = jnp.bfloat16

GRID_W = 64
HEADS = 8
NOPE = 64
ROPE = 32
VDIM = 64
QK_PAD = 128
Q_RANK = 384
KV_RANK = 256
CONV_CH = 256
CONV_WIDTH = 31
CONV_HALO = 16
SGU_HEADS = 4
SGU_HEAD_DIM = 64
SGU_CH = 256
CHUNK = 128
EPS = 1e-6
ROPE_BASE = 10000.0
ATTN_SCALE = (NOPE + ROPE) ** -0.5
LOG2E = math.log2(math.e)

TOK_TILE = 512
SUB_TILE = 256
Q_TILE = 2048
Q_BLOCK = 512
KV_STEP = 512
KV_UNROLL = 8
R0_KEYS = 16
STALE_MAX_HEADROOM = 64.0
V_EXT = VDIM + 16
V_ROWS = 128
FF_CHUNK = 1024
ADA_COLS = 1536
VMEM_LIMIT_BYTES = 56 * 1024 * 1024

ZC_Q = 0
ZC_KV = ZC_Q + Q_RANK
ZC_CONV = ZC_KV + KV_RANK
ZC_SGU = ZC_CONV + 2 * CONV_CH
ZC_KR = ZC_SGU + 2 * SGU_CH
ZC_KRR = ZC_KR + QK_PAD
Z_COLS = ZC_KRR + QK_PAD


def _cparams(sem):
    return pltpu.CompilerParams(dimension_semantics=sem, vmem_limit_bytes=VMEM_LIMIT_BYTES)


def _sigmoid(v):
    return 1.0 / (1.0 + jnp.exp(-v))


def _layernorm(v, g, b):
    mu = jnp.mean(v, -1, keepdims=True)
    d = v - mu
    var = jnp.mean(d * d, -1, keepdims=True)
    return d * lax.rsqrt(var + EPS) * g + b


def _rms(v, g):
    return v * lax.rsqrt(jnp.mean(v * v, -1, keepdims=True) + EPS) * g


def _dot(a, b):
    return jnp.dot(a, b, preferred_element_type=F32)


def _dot_nt(a, b):
    return lax.dot_general(a, b, (((1,), (1,)), ((), ())), preferred_element_type=F32)


def _dot_tn(a, b):
    return lax.dot_general(a, b, (((0,), (0,)), ((), ())), preferred_element_type=F32)


def _ada_kernel(c_ref, w_ref, b_ref, o_ref):
    c = c_ref[...]
    s = (c * _sigmoid(c)).astype(BF16)
    o_ref[0] = _dot(s, w_ref[0].astype(BF16)) + b_ref[0]


def _ada_call(cvec, ada_w, ada_b):
    n_layers, d, n6 = ada_w.shape
    rows = cvec.shape[0]
    return pl.pallas_call(
        _ada_kernel,
        grid=(n_layers, n6 // ADA_COLS),
        in_specs=[
            pl.BlockSpec((rows, d), lambda l, j: (0, 0)),
            pl.BlockSpec((1, d, ADA_COLS), lambda l, j: (l, 0, j)),
            pl.BlockSpec((1, 1, ADA_COLS), lambda l, j: (l, 0, j)),
        ],
        out_specs=pl.BlockSpec((1, rows, ADA_COLS), lambda l, j: (l, 0, j)),
        out_shape=jax.ShapeDtypeStruct((n_layers, rows, n6), F32),
        compiler_params=_cparams(("arbitrary", "arbitrary")),
        name="adaln",
    )(cvec, ada_w, ada_b.reshape(n_layers, 1, n6))


def _pre_kernel(x_ref, xp_ref, xn_ref, mod_ref, n1g_ref, win_ref, qg_ref, wqT_ref, wqrT_ref,
                kvg_ref, wk_ref, wvT_ref, cosk_ref, sink_ref, cosq_ref, sinq_ref,
                convw_ref, convb_ref, clng_ref, clnb_ref, slng_ref, slnb_ref, sguw_ref, sgub_ref,
                qT_out, k_out, vT_out, yc_out, ys_out, ybuf, yshift, *, n_lat_tiles, n_ctx):
    i = pl.program_id(1)
    tt = x_ref.shape[1]
    subs = [(r, SUB_TILE) for r in range(0, tt, SUB_TILE)]
    sh1 = mod_ref[0, 0, 0:1, :]
    gain1 = n1g_ref[0] * (1.0 + mod_ref[0, 0, 1:2, :])

    def normmod(xt):
        return (_rms(xt, gain1) + sh1).astype(BF16)

    zs = [_dot(normmod(x_ref[0, r:r + n, :]), win_ref[0]) for r, n in subs]

    def glu(zc):
        return zc[:, :CONV_CH] * _sigmoid(zc[:, CONV_CH:])

    wconv_in = win_ref[0, :, ZC_CONV:ZC_CONV + 2 * CONV_CH]
    is_ctx = i == n_lat_tiles
    first = jnp.logical_or(i == 0, is_ctx)
    last = jnp.logical_or(i == n_lat_tiles - 1, is_ctx)
    ybuf[0:CONV_HALO, :] = jnp.where(first, 0.0, glu(_dot(normmod(xp_ref[0]), wconv_in)))
    ybuf[CONV_HALO + tt:, :] = jnp.where(last, 0.0, glu(_dot(normmod(xn_ref[0]), wconv_in)))
    for (r, n), z in zip(subs, zs):
        y = glu(z[:, ZC_CONV:ZC_CONV + 2 * CONV_CH])
        if r >= n_ctx:
            y = jnp.where(is_ctx, 0.0, y)
        ybuf[CONV_HALO + r:CONV_HALO + r + n, :] = y

    qscale = ATTN_SCALE * LOG2E
    for (r, n), z in zip(subs, zs):
        qn = _rms(z[:, ZC_Q:ZC_Q + Q_RANK], qg_ref[0]).astype(BF16)
        qT = _dot_nt(wqT_ref[0], qn)
        qrT = _dot_nt(wqrT_ref[0], qn)
        cq = cosq_ref[:, r:r + n]
        sq = sinq_ref[:, r:r + n]
        for h in range(HEADS):
            r0 = h * QK_PAD
            qT_out[0, r0:r0 + NOPE, r:r + n] = (qT[r0:r0 + NOPE, :] * qscale).astype(BF16)
            rot = qT[r0 + NOPE:r0 + NOPE + ROPE, :] * cq + qrT[h * ROPE:(h + 1) * ROPE, :] * sq
            qT_out[0, r0 + NOPE:r0 + NOPE + ROPE, r:r + n] = (rot * qscale).astype(BF16)
            qT_out[0, r0 + NOPE + ROPE:r0 + QK_PAD, r:r + n] = jnp.zeros((QK_PAD - NOPE - ROPE, n), BF16)

        kvn = _rms(z[:, ZC_KV:ZC_KV + KV_RANK], kvg_ref[0]).astype(BF16)
        kext = _dot(kvn, wk_ref[0])
        krope = (z[:, ZC_KR:ZC_KR + QK_PAD] * cosk_ref[r:r + n, :]
                 + z[:, ZC_KRR:ZC_KRR + QK_PAD] * sink_ref[r:r + n, :])
        for h in range(HEADS):
            c0 = h * QK_PAD
            k_out[0, r:r + n, c0:c0 + QK_PAD] = (kext[:, c0:c0 + QK_PAD] + krope).astype(BF16)
        vT = _dot_nt(wvT_ref[0], kvn)
        for h in range(HEADS):
            vT_out[0, h * V_ROWS:h * V_ROWS + VDIM, r:r + n] = vT[h * VDIM:(h + 1) * VDIM, :].astype(BF16)
            vT_out[0, h * V_ROWS + VDIM:h * V_ROWS + V_EXT, r:r + n] = jnp.ones((V_EXT - VDIM, n), BF16)
            vT_out[0, h * V_ROWS + V_EXT:(h + 1) * V_ROWS, r:r + n] = jnp.zeros((V_ROWS - V_EXT, n), BF16)

    lane = lax.broadcasted_iota(jnp.int32, (CHUNK, SGU_CH), 1)
    bias = sgub_ref[0]
    for (r, n), z in zip(subs, zs):
        zsg = z[:, ZC_SGU:ZC_SGU + 2 * SGU_CH]
        zg = 0.5 * zsg * (1.0 + jnp.tanh(math.sqrt(2.0 / math.pi) * (zsg + 0.044715 * (zsg * zsg * zsg))))
        u = zg[:, :SGU_CH]
        v = _layernorm(zg[:, SGU_CH:], slng_ref[0], slnb_ref[0]).astype(BF16)
        for c in range(n // CHUNK):
            vc = v[c * CHUNK:(c + 1) * CHUNK, :]
            mixed = _dot(sguw_ref[0, 0], vc)
            for h in range(1, SGU_HEADS):
                mixed = jnp.where(lane >= h * SGU_HEAD_DIM, _dot(sguw_ref[0, h], vc), mixed)
            ys_out[0, r + c * CHUNK:r + (c + 1) * CHUNK, :] = (
                u[c * CHUNK:(c + 1) * CHUNK, :] * (mixed + bias)).astype(BF16)

    cw = convw_ref[0]
    accs = [jnp.zeros((n, CONV_CH), F32) for _, n in subs]
    n_a = (CONV_WIDTH + 8) // 8
    for b in range(8):
        src = ybuf
        if b:
            yshift[...] = ybuf[b:b + yshift.shape[0], :]
            src = yshift
        for a in range(n_a):
            k = 8 * a + b - 1
            if 0 <= k < CONV_WIDTH:
                for j, (r, n) in enumerate(subs):
                    accs[j] = accs[j] + src[r + 8 * a:r + 8 * a + n, :] * cw[k:k + 1, :]
    for j, (r, n) in enumerate(subs):
        yc = _layernorm(accs[j] + convb_ref[0], clng_ref[0], clnb_ref[0])
        yc_out[0, r:r + n, :] = (yc * _sigmoid(yc)).astype(BF16)


def _pre_call(layer, xc, modsel, p, n_lat_tiles, n_ctx):
    bsz, ltot, d = xc.shape
    tt = TOK_TILE
    n_tiles = ltot // tt
    hb = tt // CONV_HALO
    n_hblocks = ltot // CONV_HALO

    def lay(shape):
        nd = len(shape)
        return pl.BlockSpec((1,) + shape, lambda b, i: (layer,) + (0,) * nd)

    in_specs = [
        pl.BlockSpec((1, tt, d), lambda b, i: (b, i, 0)),
        pl.BlockSpec((1, CONV_HALO, d), lambda b, i: (b, jnp.maximum(i * hb - 1, 0), 0)),
        pl.BlockSpec((1, CONV_HALO, d), lambda b, i: (b, jnp.minimum((i + 1) * hb, n_hblocks - 1), 0)),
        pl.BlockSpec((1, 1, 8, d), lambda b, i: (b, i // n_lat_tiles, 0, 0)),
        lay((1, d)),
        lay((d, Z_COLS)),
        lay((1, Q_RANK)),
        lay((HEADS * QK_PAD, Q_RANK)),
        lay((HEADS * ROPE, Q_RANK)),
        lay((1, KV_RANK)),
        lay((KV_RANK, HEADS * QK_PAD)),
        lay((HEADS * VDIM, KV_RANK)),
        pl.BlockSpec((tt, QK_PAD), lambda b, i: (i, 0)),
        pl.BlockSpec((tt, QK_PAD), lambda b, i: (i, 0)),
        pl.BlockSpec((ROPE, tt), lambda b, i: (0, i)),
        pl.BlockSpec((ROPE, tt), lambda b, i: (0, i)),
        lay((CONV_WIDTH, CONV_CH)),
        lay((1, CONV_CH)),
        lay((1, CONV_CH)),
        lay((1, CONV_CH)),
        lay((1, SGU_CH)),
        lay((1, SGU_CH)),
        lay((SGU_HEADS, CHUNK, CHUNK)),
        lay((CHUNK, SGU_CH)),
    ]
    out_specs = [
        pl.BlockSpec((1, HEADS * QK_PAD, tt), lambda b, i: (b, 0, i)),
        pl.BlockSpec((1, tt, HEADS * QK_PAD), lambda b, i: (b, i, 0)),
        pl.BlockSpec((1, HEADS * V_ROWS, tt), lambda b, i: (b, 0, i)),
        pl.BlockSpec((1, tt, CONV_CH), lambda b, i: (b, i, 0)),
        pl.BlockSpec((1, tt, SGU_CH), lambda b, i: (b, i, 0)),
    ]
    out_shape = [
        jax.ShapeDtypeStruct((bsz, HEADS * QK_PAD, ltot), BF16),
        jax.ShapeDtypeStruct((bsz, ltot, HEADS * QK_PAD), BF16),
        jax.ShapeDtypeStruct((bsz, HEADS * V_ROWS, ltot), BF16),
        jax.ShapeDtypeStruct((bsz, ltot, CONV_CH), BF16),
        jax.ShapeDtypeStruct((bsz, ltot, SGU_CH), BF16),
    ]
    return pl.pallas_call(
        functools.partial(_pre_kernel, n_lat_tiles=n_lat_tiles, n_ctx=n_ctx),
        grid=(bsz, n_tiles),
        in_specs=in_specs,
        out_specs=out_specs,
        out_shape=out_shape,
        scratch_shapes=[pltpu.VMEM((tt + 2 * CONV_HALO, CONV_CH), F32),
                        pltpu.VMEM((tt + 2 * CONV_HALO - 8, CONV_CH), F32)],
        compiler_params=_cparams(("arbitrary", "arbitrary")),
        name="pre",
    )(xc, xc, xc, modsel, p["norm1_g"], p["w_in"], p["q_norm_g"], p["wqT"], p["wqrT"],
      p["kv_norm_g"], p["wk"], p["wvT"], p["cosk"], p["sink"], p["cosq"], p["sinq"],
      p["conv_w"], p["conv_b"], p["conv_ln_g"], p["conv_ln_b"], p["sgu_ln_g"], p["sgu_ln_b"],
      p["sgu_w"], p["sgu_bias"])


def _softmax_update(s, vt, m, acc):
    m_new = jnp.maximum(m, jnp.max(s, axis=0, keepdims=True))
    alpha = jnp.exp2(m - m_new)
    pr = jnp.exp2(s - m_new).astype(BF16)
    return m_new, alpha * acc + _dot(vt, pr)


def _softmax_finish(acc):
    return (acc[0:VDIM, :] / acc[VDIM:VDIM + 1, :]).astype(BF16)


def _attn_lat_kernel(qT_ref, k_ref, vT_ref, o_ref, acc_scr, *, n_lat, n_ctx):
    tq = qT_ref.shape[2]
    n_cb = tq // Q_BLOCK
    n_steps = n_lat // KV_STEP

    def cols(c):
        return slice(c * Q_BLOCK, (c + 1) * Q_BLOCK)

    def scores(o, size, c):
        return _dot(k_ref[0, pl.ds(o, size), :], qT_ref[0, :, cols(c)])

    def run(blocks, carry):
        rs, excess = list(carry[0]), list(carry[1])
        s_next = scores(*blocks[0])
        for i, (o, size, c) in enumerate(blocks):
            s = s_next
            if i + 1 < len(blocks):
                s_next = scores(*blocks[i + 1])
            mx = jnp.max(s, axis=0, keepdims=True)
            pv = _dot(vT_ref[0, :, pl.ds(o, size)], jnp.exp2(s - rs[c]).astype(BF16))[0:V_EXT, :]
            r_new = jnp.maximum(rs[c], mx)
            acc_scr[:, cols(c)] = (acc_scr[:, cols(c)] + pv) * jnp.exp2(rs[c] - r_new)
            excess[c] = jnp.maximum(excess[c], mx - rs[c])
            rs[c] = r_new
        return tuple(rs), tuple(excess)

    def latent_steps(j, carry):
        offs = [pl.multiple_of((j * KV_UNROLL + u) * KV_STEP, KV_STEP) for u in range(KV_UNROLL)]
        return run([(o, KV_STEP, c) for o in offs for c in range(n_cb)], carry)

    acc_scr[...] = jnp.zeros(acc_scr.shape, F32)
    carry = (tuple(jnp.max(scores(n_lat, R0_KEYS, c), axis=0, keepdims=True) for c in range(n_cb)),
             tuple(jnp.zeros((1, Q_BLOCK), F32) for _ in range(n_cb)))
    carry = run([(n_lat, n_ctx, c) for c in range(n_cb)], carry)
    _, excess = lax.fori_loop(0, n_steps // KV_UNROLL, latent_steps, carry)
    o_ref[0] = _softmax_finish(acc_scr[...])

    @pl.when(jnp.max(jnp.concatenate(excess, axis=1)) > STALE_MAX_HEADROOM)
    def _():
        def exact_step(o, size, carry):
            s = _dot(k_ref[0, pl.ds(o, size), :], qT_ref[0])
            return _softmax_update(s, vT_ref[0, 0:V_EXT, pl.ds(o, size)], *carry)

        carry = lax.fori_loop(
            0, n_steps, lambda t, cr: exact_step(pl.multiple_of(t * KV_STEP, KV_STEP), KV_STEP, cr),
            (jnp.full((1, tq), -jnp.inf, F32), jnp.zeros((V_EXT, tq), F32)))
        _, acc = exact_step(n_lat, n_ctx, carry)
        o_ref[0] = _softmax_finish(acc)


def _attn_ctx_kernel(qT_ref, k_ref, vT_ref, a_hbm_ref, o_ref):
    del a_hbm_ref
    s = _dot(k_ref[0], qT_ref[0])
    tq = s.shape[1]
    _, acc = _softmax_update(s, vT_ref[0, 0:V_EXT, :], jnp.full((1, tq), -jnp.inf, F32),
                             jnp.zeros((V_EXT, tq), F32))
    o_ref[0, :, 0:tq] = _softmax_finish(acc)
    if o_ref.shape[2] > tq:
        o_ref[0, :, tq:] = jnp.zeros((o_ref.shape[1], o_ref.shape[2] - tq), BF16)


def _attn_call(qT, k, vT, n_lat, n_ctx, with_ctx):
    bsz, _, ltot = qT.shape
    tq = Q_TILE
    assert n_lat % (KV_STEP * KV_UNROLL) == 0 and n_lat % tq == 0 and tq % Q_BLOCK == 0
    aT = pl.pallas_call(
        functools.partial(_attn_lat_kernel, n_lat=n_lat, n_ctx=n_ctx),
        grid=(bsz, HEADS, n_lat // tq),
        in_specs=[
            pl.BlockSpec((1, QK_PAD, tq), lambda b, h, q: (b, h, q)),
            pl.BlockSpec((1, ltot, QK_PAD), lambda b, h, q: (b, 0, h)),
            pl.BlockSpec((1, V_ROWS, ltot), lambda b, h, q: (b, h, 0)),
        ],
        out_specs=pl.BlockSpec((1, VDIM, tq), lambda b, h, q: (b, h, q)),
        out_shape=jax.ShapeDtypeStruct((bsz, HEADS * VDIM, ltot), BF16),
        scratch_shapes=[pltpu.VMEM((V_EXT, tq), F32)],
        compiler_params=_cparams(("arbitrary", "arbitrary", "arbitrary")),
        name="attn_lat",
    )(qT, k, vT)
    if not with_ctx:
        return aT
    cb = n_lat // n_ctx
    return pl.pallas_call(
        _attn_ctx_kernel,
        grid=(bsz, HEADS),
        in_specs=[
            pl.BlockSpec((1, QK_PAD, n_ctx), lambda b, h: (b, h, cb)),
            pl.BlockSpec((1, n_ctx, QK_PAD), lambda b, h: (b, cb, h)),
            pl.BlockSpec((1, V_ROWS, n_ctx), lambda b, h: (b, h, cb)),
            pl.BlockSpec(memory_space=pl.ANY),
        ],
        out_specs=pl.BlockSpec((1, VDIM, TOK_TILE), lambda b, h: (b, h, n_lat // TOK_TILE)),
        out_shape=jax.ShapeDtypeStruct(aT.shape, aT.dtype),
        input_output_aliases={3: 0},
        compiler_params=_cparams(("arbitrary", "arbitrary")),
        name="attn_ctx",
    )(qT, k, vT, aT)


def _post_kernel(x_ref, aT_ref, yc_ref, ys_ref, mod_ref, n2g_ref, wout_ref, w1_ref, w2_ref, fg_ref,
                 o_ref, *, final):
    tt, d = x_ref.shape[1], x_ref.shape[2]
    d_ff = w1_ref.shape[2]
    n_attn = aT_ref.shape[1]
    subs = [(r, SUB_TILE) for r in range(0, tt, SUB_TILE)]
    g1 = mod_ref[0, 0, 2:3, :]
    sh2 = mod_ref[0, 0, 3:4, :]
    gain2 = n2g_ref[0] * (1.0 + mod_ref[0, 0, 4:5, :])
    g2 = mod_ref[0, 0, 5:6, :]
    x1s, hs = [], []
    for r, n in subs:
        y = _dot_tn(aT_ref[0, :, r:r + n], wout_ref[0, 0:n_attn, :])
        y = y + _dot(yc_ref[0, r:r + n, :], wout_ref[0, n_attn:n_attn + CONV_CH, :])
        y = y + _dot(ys_ref[0, r:r + n, :], wout_ref[0, n_attn + CONV_CH:d, :])
        x1 = x_ref[0, r:r + n, :] + g1 * y
        x1s.append(x1)
        hs.append((_rms(x1, gain2) + sh2).astype(BF16))
    for (r, n), x1, h in zip(subs, x1s, hs):
        ff = jnp.zeros(x1.shape, F32)
        for c in range(d_ff // FF_CHUNK):
            f = jnp.maximum(_dot(h, w1_ref[0, :, c * FF_CHUNK:(c + 1) * FF_CHUNK]), 0.0)
            ff = ff + _dot((f * f).astype(BF16), w2_ref[0, c * FF_CHUNK:(c + 1) * FF_CHUNK, :])
        x2 = x1 + g2 * ff
        if final:
            x2 = _rms(x2, fg_ref[...])
        o_ref[0, r:r + n, :] = x2


def _post_call(layer, xc, aT, yc, ys, modsel, p, n_lat_tiles, n_out_tiles, final):
    bsz, _, d = xc.shape
    tt = TOK_TILE
    d_ff = p["w_ff1"].shape[2]

    def lay(shape):
        nd = len(shape)
        return pl.BlockSpec((1,) + shape, lambda b, i: (layer,) + (0,) * nd,
                            pipeline_mode=pl.Buffered(1))

    return pl.pallas_call(
        functools.partial(_post_kernel, final=final),
        grid=(bsz, n_out_tiles),
        in_specs=[
            pl.BlockSpec((1, tt, d), lambda b, i: (b, i, 0)),
            pl.BlockSpec((1, HEADS * VDIM, tt), lambda b, i: (b, 0, i)),
            pl.BlockSpec((1, tt, CONV_CH), lambda b, i: (b, i, 0)),
            pl.BlockSpec((1, tt, SGU_CH), lambda b, i: (b, i, 0)),
            pl.BlockSpec((1, 1, 8, d), lambda b, i: (b, i // n_lat_tiles, 0, 0)),
            pl.BlockSpec((1, 1, d), lambda b, i: (layer, 0, 0)),
            lay((d, d)),
            lay((d, d_ff)),
            lay((d_ff, d)),
            pl.BlockSpec((1, d), lambda b, i: (0, 0)),
        ],
        out_specs=pl.BlockSpec((1, tt, d), lambda b, i: (b, i, 0)),
        out_shape=jax.ShapeDtypeStruct((bsz, n_out_tiles * tt, d), F32),
        compiler_params=_cparams(("arbitrary", "arbitrary")),
        name="post",
    )(xc, aT, yc, ys, modsel, p["norm2_g"], p["w_out"], p["w_ff1"], p["w_ff2"], p["final_g"])


def _rotate_half_cols(w):
    n = ROPE // 4
    return jnp.concatenate([-w[..., n:2 * n], w[..., 0:n], -w[..., 3 * n:4 * n], w[..., 2 * n:3 * n]], -1)


def _prepare(n_lat, n_tail, norm1_g, norm2_g, w_in, q_norm_g, w_uq, kv_norm_g, w_ukv, conv_w, conv_b,
             conv_ln_g, conv_ln_b, sgu_ln_g, sgu_ln_b, sgu_w, sgu_b, w_out, w_ff1, w_ff2, final_g):
    n_layers, d, _ = w_in.shape
    o_q, o_kv, o_kr = 0, Q_RANK, Q_RANK + KV_RANK
    o_conv = o_kr + ROPE
    o_sgu = o_conv + 2 * CONV_CH
    w_kr = w_in[:, :, o_kr:o_kr + ROPE]

    def placed(w):
        return jnp.pad(w, ((0, 0), (0, 0), (NOPE, QK_PAD - NOPE - ROPE)))

    w_in_x = jnp.concatenate([
        w_in[:, :, o_q:o_q + Q_RANK], w_in[:, :, o_kv:o_kv + KV_RANK],
        w_in[:, :, o_conv:o_conv + 2 * CONV_CH], w_in[:, :, o_sgu:o_sgu + 2 * SGU_CH],
        placed(w_kr), placed(_rotate_half_cols(w_kr))], -1).astype(BF16)

    wq = w_uq.reshape(n_layers, Q_RANK, HEADS, NOPE + ROPE)
    wq_pad = jnp.pad(wq, ((0, 0), (0, 0), (0, 0), (0, QK_PAD - NOPE - ROPE)))
    wqT = wq_pad.reshape(n_layers, Q_RANK, HEADS * QK_PAD).transpose(0, 2, 1).astype(BF16)
    wqr = _rotate_half_cols(wq[..., NOPE:])
    wqrT = wqr.reshape(n_layers, Q_RANK, HEADS * ROPE).transpose(0, 2, 1).astype(BF16)

    wkv = w_ukv.reshape(n_layers, KV_RANK, HEADS, NOPE + VDIM)
    wk = jnp.pad(wkv[..., :NOPE], ((0, 0), (0, 0), (0, 0), (0, QK_PAD - NOPE)))
    wk = wk.reshape(n_layers, KV_RANK, HEADS * QK_PAD).astype(BF16)
    wvT = wkv[..., NOPE:].reshape(n_layers, KV_RANK, HEADS * VDIM).transpose(0, 2, 1).astype(BF16)

    t = jnp.arange(n_lat, dtype=jnp.int32)
    n = ROPE // 4
    inv = 1.0 / (ROPE_BASE ** (jnp.arange(n, dtype=F32) / n))
    ang_r = (t // GRID_W).astype(F32)[:, None] * inv
    ang_c = (t % GRID_W).astype(F32)[:, None] * inv
    cos32 = jnp.concatenate([jnp.cos(ang_r)] * 2 + [jnp.cos(ang_c)] * 2, -1)
    sin32 = jnp.concatenate([jnp.sin(ang_r)] * 2 + [jnp.sin(ang_c)] * 2, -1)
    cos32 = jnp.concatenate([cos32, jnp.ones((n_tail, ROPE), F32)], 0)
    sin32 = jnp.concatenate([sin32, jnp.zeros((n_tail, ROPE), F32)], 0)
    pad = ((0, 0), (NOPE, QK_PAD - NOPE - ROPE))

    sgu_bias = jnp.broadcast_to(jnp.swapaxes(sgu_b, 1, 2)[:, :, :, None],
                                (n_layers, CHUNK, SGU_HEADS, SGU_HEAD_DIM)).reshape(n_layers, CHUNK, SGU_CH)

    def row(a):
        return a.reshape(n_layers, 1, a.shape[-1])

    return dict(
        norm1_g=row(norm1_g), norm2_g=row(norm2_g), w_in=w_in_x, q_norm_g=row(q_norm_g), wqT=wqT, wqrT=wqrT,
        kv_norm_g=row(kv_norm_g), wk=wk, wvT=wvT,
        cosk=jnp.pad(cos32, pad), sink=jnp.pad(sin32, pad), cosq=cos32.T, sinq=sin32.T,
        conv_w=conv_w, conv_b=row(conv_b), conv_ln_g=row(conv_ln_g), conv_ln_b=row(conv_ln_b),
        sgu_ln_g=row(sgu_ln_g), sgu_ln_b=row(sgu_ln_b), sgu_w=sgu_w.astype(BF16), sgu_bias=sgu_bias,
        w_out=w_out.astype(BF16), w_ff1=w_ff1.astype(BF16), w_ff2=w_ff2.astype(BF16),
        final_g=final_g.reshape(1, -1))


def kernel(x, c, ctx, c_ctx, ada_w, ada_b, norm1_g, norm2_g, w_in, q_norm_g, w_uq, kv_norm_g, w_ukv,
           conv_w, conv_b, conv_ln_g, conv_ln_b, sgu_ln_g, sgu_ln_b, sgu_w, sgu_b, w_out, w_ff1, w_ff2,
           final_g):
    bsz, n_lat, d = x.shape
    n_ctx = ctx.shape[1]
    n_layers = w_in.shape[0]
    assert n_lat % TOK_TILE == 0 and n_lat % GRID_W == 0
    assert n_ctx <= TOK_TILE and n_ctx % SUB_TILE == 0
    assert n_lat % n_ctx == 0
    n_lat_tiles = n_lat // TOK_TILE
    n_tiles = n_lat_tiles + 1

    p = _prepare(n_lat, TOK_TILE, norm1_g, norm2_g, w_in, q_norm_g, w_uq, kv_norm_g, w_ukv, conv_w, conv_b,
                 conv_ln_g, conv_ln_b, sgu_ln_g, sgu_ln_b, sgu_w, sgu_b, w_out, w_ff1, w_ff2, final_g)

    rows = -(-(bsz + 1) // 8) * 8
    cvec = jnp.concatenate([c, c_ctx[None, :], jnp.zeros((rows - bsz - 1, d), F32)], 0)
    mods = _ada_call(cvec, ada_w, ada_b).reshape(n_layers, rows, 6, d)
    m_lat = mods[:, :bsz]
    m_ctx = jnp.broadcast_to(mods[:, bsz][:, None], m_lat.shape)
    modsel = jnp.pad(jnp.stack([m_lat, m_ctx], 2), ((0, 0), (0, 0), (0, 0), (0, 2), (0, 0)))

    xc = jnp.concatenate([x, ctx, jnp.zeros((bsz, TOK_TILE - n_ctx, d), x.dtype)], 1)
    for layer in range(n_layers):
        last = layer == n_layers - 1
        n_act = n_lat_tiles if last else n_tiles
        qT, k, vT, yc, ys = _pre_call(layer, xc, modsel[layer], p, n_lat_tiles, n_ctx)
        aT = _attn_call(qT, k, vT, n_lat, n_ctx, not last)
        xc = _post_call(layer, xc, aT, yc, ys, modsel[layer], p, n_lat_tiles, n_act, last)
    return xc
```

```python
import functools
import math

import jax
import jax.numpy as jnp
from jax import lax
from jax.experimental import pallas as pl
from jax.experimental.pallas import tpu as pltpu

F32 = jnp.float32
BF16 = jnp.bfloat16

GRID_W = 64
HEADS = 8
NOPE = 64
ROPE = 32
VDIM = 64
QK_PAD = 128
Q_RANK = 384
KV_RANK = 256
CONV_CH = 256
CONV_WIDTH = 31
CONV_HALO = 16
SGU_HEADS = 4
SGU_HEAD_DIM = 64
SGU_CH = 256
CHUNK = 128
EPS = 1e-6
ROPE_BASE = 10000.0
ATTN_SCALE = (NOPE + ROPE) ** -0.5
LOG2E = math.log2(math.e)

TOK_TILE = 512
SUB_TILE = 256
Q_TILE = 4096
Q_BLOCK = 512
KV_STEP = 512
KV_UNROLL = 4
R0_KEYS = 16
STALE_MAX_HEADROOM = 64.0
V_EXT = VDIM + 16
V_ROWS = 128
FF_CHUNK = 1024
ADA_COLS = 1536
VMEM_LIMIT_BYTES = 56 * 1024 * 1024

ZC_Q = 0
ZC_KV = ZC_Q + Q_RANK
ZC_CONV = ZC_KV + KV_RANK
ZC_SGU = ZC_CONV + 2 * CONV_CH
ZC_KR = ZC_SGU + 2 * SGU_CH
Z_COLS = ZC_KR + QK_PAD


def _cparams(sem):
    return pltpu.CompilerParams(dimension_semantics=sem, vmem_limit_bytes=VMEM_LIMIT_BYTES)


def _sigmoid(v):
    return 1.0 / (1.0 + jnp.exp(-v))


def _layernorm(v, g, b):
    mu = jnp.mean(v, -1, keepdims=True)
    d = v - mu
    var = jnp.mean(d * d, -1, keepdims=True)
    return d * lax.rsqrt(var + EPS) * g + b


def _rms(v, g):
    return v * lax.rsqrt(jnp.mean(v * v, -1, keepdims=True) + EPS) * g


def _dot(a, b):
    return jnp.dot(a, b, preferred_element_type=F32)


def _dot_nt(a, b):
    return lax.dot_general(a, b, (((1,), (1,)), ((), ())), preferred_element_type=F32)


def _dot_tn(a, b):
    return lax.dot_general(a, b, (((0,), (0,)), ((), ())), preferred_element_type=F32)


def _ada_kernel(c_ref, w_ref, b_ref, o_ref):
    c = c_ref[...]
    s = (c * _sigmoid(c)).astype(BF16)
    o_ref[0] = _dot(s, w_ref[0].astype(BF16)) + b_ref[0]


def _ada_call(cvec, ada_w, ada_b):
    n_layers, d, n6 = ada_w.shape
    rows = cvec.shape[0]
    return pl.pallas_call(
        _ada_kernel,
        grid=(n_layers, n6 // ADA_COLS),
        in_specs=[
            pl.BlockSpec((rows, d), lambda l, j: (0, 0)),
            pl.BlockSpec((1, d, ADA_COLS), lambda l, j: (l, 0, j)),
            pl.BlockSpec((1, 1, ADA_COLS), lambda l, j: (l, 0, j)),
        ],
        out_specs=pl.BlockSpec((1, rows, ADA_COLS), lambda l, j: (l, 0, j)),
        out_shape=jax.ShapeDtypeStruct((n_layers, rows, n6), F32),
        compiler_params=_cparams(("arbitrary", "arbitrary")),
        name="adaln",
    )(cvec, ada_w, ada_b.reshape(n_layers, 1, n6))


def _pre_kernel(x_ref, xp_ref, xn_ref, mod_ref, n1g_ref, win_ref, qg_ref, wqT_ref, wqrT_ref,
                kvg_ref, wk_ref, wvT_ref, cosk_ref, sink_ref, cosq_ref, sinq_ref,
                convw_ref, convb_ref, clng_ref, clnb_ref, slng_ref, slnb_ref, sguw_ref, sgub_ref,
                qT_out, k_out, vT_out, yc_out, ys_out, ybuf, yshift, *, n_lat_tiles, n_ctx):
    i = pl.program_id(1)
    tt = x_ref.shape[1]
    subs = [(r, SUB_TILE) for r in range(0, tt, SUB_TILE)]
    sh1 = mod_ref[0, 0, 0:1, :]
    gain1 = n1g_ref[0] * (1.0 + mod_ref[0, 0, 1:2, :])

    def normmod(xt):
        return (_rms(xt, gain1) + sh1).astype(BF16)

    zs = [_dot(normmod(x_ref[0, r:r + n, :]), win_ref[0]) for r, n in subs]

    def glu(zc):
        return zc[:, :CONV_CH] * _sigmoid(zc[:, CONV_CH:])

    wconv_in = win_ref[0, :, ZC_CONV:ZC_CONV + 2 * CONV_CH]
    is_ctx = i == n_lat_tiles
    first = jnp.logical_or(i == 0, is_ctx)
    last = jnp.logical_or(i == n_lat_tiles - 1, is_ctx)
    ybuf[0:CONV_HALO, :] = jnp.where(first, 0.0, glu(_dot(normmod(xp_ref[0]), wconv_in)))
    ybuf[CONV_HALO + tt:, :] = jnp.where(last, 0.0, glu(_dot(normmod(xn_ref[0]), wconv_in)))
    for (r, n), z in zip(subs, zs):
        y = glu(z[:, ZC_CONV:ZC_CONV + 2 * CONV_CH])
        if r >= n_ctx:
            y = jnp.where(is_ctx, 0.0, y)
        ybuf[CONV_HALO + r:CONV_HALO + r + n, :] = y

    qscale = ATTN_SCALE * LOG2E
    for (r, n), z in zip(subs, zs):
        qn = _rms(z[:, ZC_Q:ZC_Q + Q_RANK], qg_ref[0]).astype(BF16)
        qT = _dot_nt(wqT_ref[0], qn)
        qrT = _dot_nt(wqrT_ref[0], qn)
        cq = cosq_ref[:, r:r + n]
        sq = sinq_ref[:, r:r + n]
        for h in range(HEADS):
            r0 = h * QK_PAD
            qT_out[0, r0:r0 + NOPE, r:r + n] = (qT[r0:r0 + NOPE, :] * qscale).astype(BF16)
            rot = qT[r0 + NOPE:r0 + NOPE + ROPE, :] * cq + qrT[h * ROPE:(h + 1) * ROPE, :] * sq
            qT_out[0, r0 + NOPE:r0 + NOPE + ROPE, r:r + n] = (rot * qscale).astype(BF16)
            qT_out[0, r0 + NOPE + ROPE:r0 + QK_PAD, r:r + n] = jnp.zeros((QK_PAD - NOPE - ROPE, n), BF16)

        kvn = _rms(z[:, ZC_KV:ZC_KV + KV_RANK], kvg_ref[0]).astype(BF16)
        kext = _dot(kvn, wk_ref[0])
        kr = z[:, ZC_KR:ZC_KR + QK_PAD]
        krope = kr * cosk_ref[r:r + n, :] + pltpu.roll(kr, QK_PAD // 2, 1) * sink_ref[r:r + n, :]
        for h in range(HEADS):
            c0 = h * QK_PAD
            k_out[0, r:r + n, c0:c0 + QK_PAD] = (kext[:, c0:c0 + QK_PAD] + krope).astype(BF16)
        vT = _dot_nt(wvT_ref[0], kvn)
        for h in range(HEADS):
            vT_out[0, h * V_ROWS:h * V_ROWS + VDIM, r:r + n] = vT[h * VDIM:(h + 1) * VDIM, :].astype(BF16)
            vT_out[0, h * V_ROWS + VDIM:h * V_ROWS + V_EXT, r:r + n] = jnp.ones((V_EXT - VDIM, n), BF16)
            vT_out[0, h * V_ROWS + V_EXT:(h + 1) * V_ROWS, r:r + n] = jnp.zeros((V_ROWS - V_EXT, n), BF16)

    lane = lax.broadcasted_iota(jnp.int32, (CHUNK, SGU_CH), 1)
    bias = sgub_ref[0]
    for (r, n), z in zip(subs, zs):
        zsg = z[:, ZC_SGU:ZC_SGU + 2 * SGU_CH]
        zg = 0.5 * zsg * (1.0 + jnp.tanh(math.sqrt(2.0 / math.pi) * (zsg + 0.044715 * (zsg * zsg * zsg))))
        u = zg[:, :SGU_CH]
        v = _layernorm(zg[:, SGU_CH:], slng_ref[0], slnb_ref[0]).astype(BF16)
        for c in range(n // CHUNK):
            vc = v[c * CHUNK:(c + 1) * CHUNK, :]
            mixed = _dot(sguw_ref[0, 0], vc)
            for h in range(1, SGU_HEADS):
                mixed = jnp.where(lane >= h * SGU_HEAD_DIM, _dot(sguw_ref[0, h], vc), mixed)
            ys_out[0, r + c * CHUNK:r + (c + 1) * CHUNK, :] = (
                u[c * CHUNK:(c + 1) * CHUNK, :] * (mixed + bias)).astype(BF16)

    cw = convw_ref[0]
    accs = [jnp.zeros((n, CONV_CH), F32) for _, n in subs]
    n_a = (CONV_WIDTH + 8) // 8
    for b in range(8):
        src = ybuf
        if b:
            yshift[...] = ybuf[b:b + yshift.shape[0], :]
            src = yshift
        for a in range(n_a):
            k = 8 * a + b - 1
            if 0 <= k < CONV_WIDTH:
                for j, (r, n) in enumerate(subs):
                    accs[j] = accs[j] + src[r + 8 * a:r + 8 * a + n, :] * cw[k:k + 1, :]
    for j, (r, n) in enumerate(subs):
        yc = _layernorm(accs[j] + convb_ref[0], clng_ref[0], clnb_ref[0])
        yc_out[0, r:r + n, :] = (yc * _sigmoid(yc)).astype(BF16)


def _pre_call(layer, xc, modsel, p, n_lat_tiles, n_ctx):
    bsz, ltot, d = xc.shape
    tt = TOK_TILE
    n_tiles = ltot // tt
    hb = tt // CONV_HALO
    n_hblocks = ltot // CONV_HALO

    def lay(shape):
        nd = len(shape)
        return pl.BlockSpec((1,) + shape, lambda b, i: (layer,) + (0,) * nd)

    in_specs = [
        pl.BlockSpec((1, tt, d), lambda b, i: (b, i, 0)),
        pl.BlockSpec((1, CONV_HALO, d), lambda b, i: (b, jnp.maximum(i * hb - 1, 0), 0)),
        pl.BlockSpec((1, CONV_HALO, d), lambda b, i: (b, jnp.minimum((i + 1) * hb, n_hblocks - 1), 0)),
        pl.BlockSpec((1, 1, 8, d), lambda b, i: (b, i // n_lat_tiles, 0, 0)),
        lay((1, d)),
        lay((d, Z_COLS)),
        lay((1, Q_RANK)),
        lay((HEADS * QK_PAD, Q_RANK)),
        lay((HEADS * ROPE, Q_RANK)),
        lay((1, KV_RANK)),
        lay((KV_RANK, HEADS * QK_PAD)),
        lay((HEADS * VDIM, KV_RANK)),
        pl.BlockSpec((tt, QK_PAD), lambda b, i: (i, 0)),
        pl.BlockSpec((tt, QK_PAD), lambda b, i: (i, 0)),
        pl.BlockSpec((ROPE, tt), lambda b, i: (0, i)),
        pl.BlockSpec((ROPE, tt), lambda b, i: (0, i)),
        lay((CONV_WIDTH, CONV_CH)),
        lay((1, CONV_CH)),
        lay((1, CONV_CH)),
        lay((1, CONV_CH)),
        lay((1, SGU_CH)),
        lay((1, SGU_CH)),
        lay((SGU_HEADS, CHUNK, CHUNK)),
        lay((CHUNK, SGU_CH)),
    ]
    out_specs = [
        pl.BlockSpec((1, HEADS * QK_PAD, tt), lambda b, i: (b, 0, i)),
        pl.BlockSpec((1, tt, HEADS * QK_PAD), lambda b, i: (b, i, 0)),
        pl.BlockSpec((1, HEADS * V_ROWS, tt), lambda b, i: (b, 0, i)),
        pl.BlockSpec((1, tt, CONV_CH), lambda b, i: (b, i, 0)),
        pl.BlockSpec((1, tt, SGU_CH), lambda b, i: (b, i, 0)),
    ]
    out_shape = [
        jax.ShapeDtypeStruct((bsz, HEADS * QK_PAD, ltot), BF16),
        jax.ShapeDtypeStruct((bsz, ltot, HEADS * QK_PAD), BF16),
        jax.ShapeDtypeStruct((bsz, HEADS * V_ROWS, ltot), BF16),
        jax.ShapeDtypeStruct((bsz, ltot, CONV_CH), BF16),
        jax.ShapeDtypeStruct((bsz, ltot, SGU_CH), BF16),
    ]
    return pl.pallas_call(
        functools.partial(_pre_kernel, n_lat_tiles=n_lat_tiles, n_ctx=n_ctx),
        grid=(bsz, n_tiles),
        in_specs=in_specs,
        out_specs=out_specs,
        out_shape=out_shape,
        scratch_shapes=[pltpu.VMEM((tt + 2 * CONV_HALO, CONV_CH), F32),
                        pltpu.VMEM((tt + 2 * CONV_HALO - 8, CONV_CH), F32)],
        compiler_params=_cparams(("arbitrary", "arbitrary")),
        name="pre",
    )(xc, xc, xc, modsel, p["norm1_g"], p["w_in"], p["q_norm_g"], p["wqT"], p["wqrT"],
      p["kv_norm_g"], p["wk"], p["wvT"], p["cosk"], p["sink"], p["cosq"], p["sinq"],
      p["conv_w"], p["conv_b"], p["conv_ln_g"], p["conv_ln_b"], p["sgu_ln_g"], p["sgu_ln_b"],
      p["sgu_w"], p["sgu_bias"])


def _softmax_update(s, vt, m, acc):
    m_new = jnp.maximum(m, jnp.max(s, axis=0, keepdims=True))
    alpha = jnp.exp2(m - m_new)
    pr = jnp.exp2(s - m_new).astype(BF16)
    return m_new, alpha * acc + _dot(vt, pr)


def _softmax_finish(acc):
    return (acc[0:VDIM, :] / acc[VDIM:VDIM + 1, :]).astype(BF16)


def _attn_lat_kernel(qT_ref, k_ref, vT_ref, o_ref, acc_scr, *, n_lat, n_ctx):
    tq = qT_ref.shape[2]
    n_cb = tq // Q_BLOCK
    n_steps = n_lat // KV_STEP

    def cols(c):
        return slice(c * Q_BLOCK, (c + 1) * Q_BLOCK)

    def scores(o, size, c):
        return _dot(k_ref[0, pl.ds(o, size), :], qT_ref[0, :, cols(c)])

    def run(blocks, carry):
        rs, excess = list(carry[0]), list(carry[1])
        s_next = scores(*blocks[0])
        for i, (o, size, c) in enumerate(blocks):
            s = s_next
            if i + 1 < len(blocks):
                s_next = scores(*blocks[i + 1])
            mx = jnp.max(s, axis=0, keepdims=True)
            pv = _dot(vT_ref[0, :, pl.ds(o, size)], jnp.exp2(s - rs[c]).astype(BF16))[0:V_EXT, :]
            r_new = jnp.maximum(rs[c], mx)
            acc_scr[:, cols(c)] = (acc_scr[:, cols(c)] + pv) * jnp.exp2(rs[c] - r_new)
            excess[c] = jnp.maximum(excess[c], mx - rs[c])
            rs[c] = r_new
        return tuple(rs), tuple(excess)

    def latent_steps(j, carry):
        offs = [pl.multiple_of((j * KV_UNROLL + u) * KV_STEP, KV_STEP) for u in range(KV_UNROLL)]
        return run([(o, KV_STEP, c) for o in offs for c in range(n_cb)], carry)

    acc_scr[...] = jnp.zeros(acc_scr.shape, F32)
    carry = (tuple(jnp.max(scores(n_lat, R0_KEYS, c), axis=0, keepdims=True) for c in range(n_cb)),
             tuple(jnp.zeros((1, Q_BLOCK), F32) for _ in range(n_cb)))
    carry = run([(n_lat, n_ctx, c) for c in range(n_cb)], carry)
    _, excess = lax.fori_loop(0, n_steps // KV_UNROLL, latent_steps, carry)
    o_ref[0] = _softmax_finish(acc_scr[...])

    @pl.when(jnp.max(jnp.concatenate(excess, axis=1)) > STALE_MAX_HEADROOM)
    def _():
        def exact_step(o, size, carry):
            s = _dot(k_ref[0, pl.ds(o, size), :], qT_ref[0])
            return _softmax_update(s, vT_ref[0, 0:V_EXT, pl.ds(o, size)], *carry)

        carry = lax.fori_loop(
            0, n_steps, lambda t, cr: exact_step(pl.multiple_of(t * KV_STEP, KV_STEP), KV_STEP, cr),
            (jnp.full((1, tq), -jnp.inf, F32), jnp.zeros((V_EXT, tq), F32)))
        _, acc = exact_step(n_lat, n_ctx, carry)
        o_ref[0] = _softmax_finish(acc)


def _attn_ctx_kernel(qT_ref, k_ref, vT_ref, a_hbm_ref, o_ref):
    del a_hbm_ref
    s = _dot(k_ref[0], qT_ref[0])
    tq = s.shape[1]
    _, acc = _softmax_update(s, vT_ref[0, 0:V_EXT, :], jnp.full((1, tq), -jnp.inf, F32),
                             jnp.zeros((V_EXT, tq), F32))
    o_ref[0, :, 0:tq] = _softmax_finish(acc)
    if o_ref.shape[2] > tq:
        o_ref[0, :, tq:] = jnp.zeros((o_ref.shape[1], o_ref.shape[2] - tq), BF16)


def _attn_call(qT, k, vT, n_lat, n_ctx, with_ctx):
    bsz, _, ltot = qT.shape
    tq = Q_TILE
    assert n_lat % (KV_STEP * KV_UNROLL) == 0 and n_lat % tq == 0 and tq % Q_BLOCK == 0
    aT = pl.pallas_call(
        functools.partial(_attn_lat_kernel, n_lat=n_lat, n_ctx=n_ctx),
        grid=(bsz, HEADS, n_lat // tq),
        in_specs=[
            pl.BlockSpec((1, QK_PAD, tq), lambda b, h, q: (b, h, q)),
            pl.BlockSpec((1, ltot, QK_PAD), lambda b, h, q: (b, 0, h)),
            pl.BlockSpec((1, V_ROWS, ltot), lambda b, h, q: (b, h, 0)),
        ],
        out_specs=pl.BlockSpec((1, VDIM, tq), lambda b, h, q: (b, h, q)),
        out_shape=jax.ShapeDtypeStruct((bsz, HEADS * VDIM, ltot), BF16),
        scratch_shapes=[pltpu.VMEM((V_EXT, tq), F32)],
        compiler_params=_cparams(("arbitrary", "arbitrary", "arbitrary")),
        name="attn_lat",
    )(qT, k, vT)
    if not with_ctx:
        return aT
    cb = n_lat // n_ctx
    return pl.pallas_call(
        _attn_ctx_kernel,
        grid=(bsz, HEADS),
        in_specs=[
            pl.BlockSpec((1, QK_PAD, n_ctx), lambda b, h: (b, h, cb)),
            pl.BlockSpec((1, n_ctx, QK_PAD), lambda b, h: (b, cb, h)),
            pl.BlockSpec((1, V_ROWS, n_ctx), lambda b, h: (b, h, cb)),
            pl.BlockSpec(memory_space=pl.ANY),
        ],
        out_specs=pl.BlockSpec((1, VDIM, TOK_TILE), lambda b, h: (b, h, n_lat // TOK_TILE)),
        out_shape=jax.ShapeDtypeStruct(aT.shape, aT.dtype),
        input_output_aliases={3: 0},
        compiler_params=_cparams(("arbitrary", "arbitrary")),
        name="attn_ctx",
    )(qT, k, vT, aT)


def _post_kernel(x_ref, aT_ref, yc_ref, ys_ref, mod_ref, n2g_ref, wout_ref, w1_ref, w2_ref, fg_ref,
                 o_ref, *, final):
    tt, d = x_ref.shape[1], x_ref.shape[2]
    d_ff = w1_ref.shape[2]
    n_attn = aT_ref.shape[1]
    subs = [(r, SUB_TILE) for r in range(0, tt, SUB_TILE)]
    g1 = mod_ref[0, 0, 2:3, :]
    sh2 = mod_ref[0, 0, 3:4, :]
    gain2 = n2g_ref[0] * (1.0 + mod_ref[0, 0, 4:5, :])
    g2 = mod_ref[0, 0, 5:6, :]
    x1s, hs = [], []
    for r, n in subs:
        y = _dot_tn(aT_ref[0, :, r:r + n], wout_ref[0, 0:n_attn, :])
        y = y + _dot(yc_ref[0, r:r + n, :], wout_ref[0, n_attn:n_attn + CONV_CH, :])
        y = y + _dot(ys_ref[0, r:r + n, :], wout_ref[0, n_attn + CONV_CH:d, :])
        x1 = x_ref[0, r:r + n, :] + g1 * y
        x1s.append(x1)
        hs.append((_rms(x1, gain2) + sh2).astype(BF16))
    for (r, n), x1, h in zip(subs, x1s, hs):
        ff = jnp.zeros(x1.shape, F32)
        for c in range(d_ff // FF_CHUNK):
            f = jnp.maximum(_dot(h, w1_ref[0, :, c * FF_CHUNK:(c + 1) * FF_CHUNK]), 0.0)
            ff = ff + _dot((f * f).astype(BF16), w2_ref[0, c * FF_CHUNK:(c + 1) * FF_CHUNK, :])
        x2 = x1 + g2 * ff
        if final:
            x2 = _rms(x2, fg_ref[...])
        o_ref[0, r:r + n, :] = x2


def _post_call(layer, xc, aT, yc, ys, modsel, p, n_lat_tiles, n_out_tiles, final):
    bsz, _, d = xc.shape
    tt = TOK_TILE
    d_ff = p["w_ff1"].shape[2]

    def lay(shape):
        nd = len(shape)
        return pl.BlockSpec((1,) + shape, lambda b, i: (layer,) + (0,) * nd,
                            pipeline_mode=pl.Buffered(1))

    return pl.pallas_call(
        functools.partial(_post_kernel, final=final),
        grid=(bsz, n_out_tiles),
        in_specs=[
            pl.BlockSpec((1, tt, d), lambda b, i: (b, i, 0)),
            pl.BlockSpec((1, HEADS * VDIM, tt), lambda b, i: (b, 0, i)),
            pl.BlockSpec((1, tt, CONV_CH), lambda b, i: (b, i, 0)),
            pl.BlockSpec((1, tt, SGU_CH), lambda b, i: (b, i, 0)),
            pl.BlockSpec((1, 1, 8, d), lambda b, i: (b, i // n_lat_tiles, 0, 0)),
            pl.BlockSpec((1, 1, d), lambda b, i: (layer, 0, 0)),
            lay((d, d)),
            lay((d, d_ff)),
            lay((d_ff, d)),
            pl.BlockSpec((1, d), lambda b, i: (0, 0)),
        ],
        out_specs=pl.BlockSpec((1, tt, d), lambda b, i: (b, i, 0)),
        out_shape=jax.ShapeDtypeStruct((bsz, n_out_tiles * tt, d), F32),
        compiler_params=_cparams(("arbitrary", "arbitrary")),
        name="post",
    )(xc, aT, yc, ys, modsel, p["norm2_g"], p["w_out"], p["w_ff1"], p["w_ff2"], p["final_g"])


def _rotate_half_cols(w):
    n = ROPE // 4
    return jnp.concatenate([-w[..., n:2 * n], w[..., 0:n], -w[..., 3 * n:4 * n], w[..., 2 * n:3 * n]], -1)


def _prepare(n_lat, n_tail, norm1_g, norm2_g, w_in, q_norm_g, w_uq, kv_norm_g, w_ukv, conv_w, conv_b,
             conv_ln_g, conv_ln_b, sgu_ln_g, sgu_ln_b, sgu_w, sgu_b, w_out, w_ff1, w_ff2, final_g):
    n_layers, d, _ = w_in.shape
    o_q, o_kv, o_kr = 0, Q_RANK, Q_RANK + KV_RANK
    o_conv = o_kr + ROPE
    o_sgu = o_conv + 2 * CONV_CH
    w_kr = w_in[:, :, o_kr:o_kr + ROPE]

    gap = jnp.zeros((n_layers, d, QK_PAD // 2 - ROPE), w_in.dtype)
    kr_slab = jnp.concatenate([_rotate_half_cols(w_kr), gap, w_kr, gap], -1)
    w_in_x = jnp.concatenate([
        w_in[:, :, o_q:o_q + Q_RANK], w_in[:, :, o_kv:o_kv + KV_RANK],
        w_in[:, :, o_conv:o_conv + 2 * CONV_CH], w_in[:, :, o_sgu:o_sgu + 2 * SGU_CH],
        kr_slab], -1).astype(BF16)

    wq = w_uq.reshape(n_layers, Q_RANK, HEADS, NOPE + ROPE)
    wq_pad = jnp.pad(wq, ((0, 0), (0, 0), (0, 0), (0, QK_PAD - NOPE - ROPE)))
    wqT = wq_pad.reshape(n_layers, Q_RANK, HEADS * QK_PAD).transpose(0, 2, 1).astype(BF16)
    wqr = _rotate_half_cols(wq[..., NOPE:])
    wqrT = wqr.reshape(n_layers, Q_RANK, HEADS * ROPE).transpose(0, 2, 1).astype(BF16)

    wkv = w_ukv.reshape(n_layers, KV_RANK, HEADS, NOPE + VDIM)
    wk = jnp.pad(wkv[..., :NOPE], ((0, 0), (0, 0), (0, 0), (0, QK_PAD - NOPE)))
    wk = wk.reshape(n_layers, KV_RANK, HEADS * QK_PAD).astype(BF16)
    wvT = wkv[..., NOPE:].reshape(n_layers, KV_RANK, HEADS * VDIM).transpose(0, 2, 1).astype(BF16)

    t = jnp.arange(n_lat, dtype=jnp.int32)
    n = ROPE // 4
    inv = 1.0 / (ROPE_BASE ** (jnp.arange(n, dtype=F32) / n))
    ang_r = (t // GRID_W).astype(F32)[:, None] * inv
    ang_c = (t % GRID_W).astype(F32)[:, None] * inv
    cos32 = jnp.concatenate([jnp.cos(ang_r)] * 2 + [jnp.cos(ang_c)] * 2, -1)
    sin32 = jnp.concatenate([jnp.sin(ang_r)] * 2 + [jnp.sin(ang_c)] * 2, -1)
    cos32 = jnp.concatenate([cos32, jnp.ones((n_tail, ROPE), F32)], 0)
    sin32 = jnp.concatenate([sin32, jnp.zeros((n_tail, ROPE), F32)], 0)
    pad = ((0, 0), (NOPE, QK_PAD - NOPE - ROPE))

    sgu_bias = jnp.broadcast_to(jnp.swapaxes(sgu_b, 1, 2)[:, :, :, None],
                                (n_layers, CHUNK, SGU_HEADS, SGU_HEAD_DIM)).reshape(n_layers, CHUNK, SGU_CH)

    def row(a):
        return a.reshape(n_layers, 1, a.shape[-1])

    return dict(
        norm1_g=row(norm1_g), norm2_g=row(norm2_g), w_in=w_in_x, q_norm_g=row(q_norm_g), wqT=wqT, wqrT=wqrT,
        kv_norm_g=row(kv_norm_g), wk=wk, wvT=wvT,
        cosk=jnp.pad(cos32, pad), sink=jnp.pad(sin32, pad), cosq=cos32.T, sinq=sin32.T,
        conv_w=conv_w, conv_b=row(conv_b), conv_ln_g=row(conv_ln_g), conv_ln_b=row(conv_ln_b),
        sgu_ln_g=row(sgu_ln_g), sgu_ln_b=row(sgu_ln_b), sgu_w=sgu_w.astype(BF16), sgu_bias=sgu_bias,
        w_out=w_out.astype(BF16), w_ff1=w_ff1.astype(BF16), w_ff2=w_ff2.astype(BF16),
        final_g=final_g.reshape(1, -1))


def kernel(x, c, ctx, c_ctx, ada_w, ada_b, norm1_g, norm2_g, w_in, q_norm_g, w_uq, kv_norm_g, w_ukv,
           conv_w, conv_b, conv_ln_g, conv_ln_b, sgu_ln_g, sgu_ln_b, sgu_w, sgu_b, w_out, w_ff1, w_ff2,
           final_g):
    bsz, n_lat, d = x.shape
    n_ctx = ctx.shape[1]
    n_layers = w_in.shape[0]
    assert n_lat % TOK_TILE == 0 and n_lat % GRID_W == 0
    assert n_ctx <= TOK_TILE and n_ctx % SUB_TILE == 0
    assert n_lat % n_ctx == 0
    n_lat_tiles = n_lat // TOK_TILE
    n_tiles = n_lat_tiles + 1

    p = _prepare(n_lat, TOK_TILE, norm1_g, norm2_g, w_in, q_norm_g, w_uq, kv_norm_g, w_ukv, conv_w, conv_b,
                 conv_ln_g, conv_ln_b, sgu_ln_g, sgu_ln_b, sgu_w, sgu_b, w_out, w_ff1, w_ff2, final_g)

    rows = -(-(bsz + 1) // 8) * 8
    cvec = jnp.concatenate([c, c_ctx[None, :], jnp.zeros((rows - bsz - 1, d), F32)], 0)
    mods = _ada_call(cvec, ada_w, ada_b).reshape(n_layers, rows, 6, d)
    m_lat = mods[:, :bsz]
    m_ctx = jnp.broadcast_to(mods[:, bsz][:, None], m_lat.shape)
    modsel = jnp.pad(jnp.stack([m_lat, m_ctx], 2), ((0, 0), (0, 0), (0, 0), (0, 2), (0, 0)))

    xc = jnp.concatenate([x, ctx, jnp.zeros((bsz, TOK_TILE - n_ctx, d), x.dtype)], 1)
    for layer in range(n_layers):
        last = layer == n_layers - 1
        n_act = n_lat_tiles if last else n_tiles
        qT, k, vT, yc, ys = _pre_call(layer, xc, modsel[layer], p, n_lat_tiles, n_ctx)
        aT = _attn_call(qT, k, vT, n_lat, n_ctx, not last)
        xc = _post_call(layer, xc, aT, yc, ys, modsel[layer], p, n_lat_tiles, n_act, last)
    return xc
```

```python
import functools
import math

import jax
import jax.numpy as jnp
from jax import lax
from jax.experimental import pallas as pl
from jax.experimental.pallas import tpu as pltpu

F32 = jnp.float32
BF16 = jnp.bfloat16

GRID_W = 64
HEADS = 8
NOPE = 64
ROPE = 32
VDIM = 64
QK_PAD = 128
Q_RANK = 384
KV_RANK = 256
CONV_CH = 256
CONV_WIDTH = 31
CONV_HALO = 16
SGU_HEADS = 4
SGU_HEAD_DIM = 64
SGU_CH = 256
CHUNK = 128
EPS = 1e-6
ROPE_BASE = 10000.0
ATTN_SCALE = (NOPE + ROPE) ** -0.5
LOG2E = math.log2(math.e)

TOK_TILE = 512
SUB_TILE = 256
Q_TILE = 4096
Q_BLOCK = 512
KV_STEP = 256
KV_UNROLL = 4
R0_KEYS = 16
STALE_MAX_HEADROOM = 64.0
V_EXT = VDIM + 16
FF_CHUNK = 1024
ADA_COLS = 1536
VMEM_LIMIT_BYTES = 56 * 1024 * 1024

ZC_Q = 0
ZC_KV = ZC_Q + Q_RANK
ZC_CONV = ZC_KV + KV_RANK
ZC_SGU = ZC_CONV + 2 * CONV_CH
ZC_KR = ZC_SGU + 2 * SGU_CH
Z_COLS = ZC_KR + QK_PAD


def _cparams(sem):
    return pltpu.CompilerParams(dimension_semantics=sem, vmem_limit_bytes=VMEM_LIMIT_BYTES)


def _sigmoid(v):
    return 1.0 / (1.0 + jnp.exp(-v))


def _layernorm(v, g, b):
    mu = jnp.mean(v, -1, keepdims=True)
    d = v - mu
    var = jnp.mean(d * d, -1, keepdims=True)
    return d * lax.rsqrt(var + EPS) * g + b


def _rms(v, g):
    return v * lax.rsqrt(jnp.mean(v * v, -1, keepdims=True) + EPS) * g


def _dot(a, b):
    return jnp.dot(a, b, preferred_element_type=F32)


def _dot_nt(a, b):
    return lax.dot_general(a, b, (((1,), (1,)), ((), ())), preferred_element_type=F32)


def _dot_tn(a, b):
    return lax.dot_general(a, b, (((0,), (0,)), ((), ())), preferred_element_type=F32)


def _ada_kernel(c_ref, w_ref, b_ref, o_ref):
    c = c_ref[...]
    s = (c * _sigmoid(c)).astype(BF16)
    o_ref[0] = _dot(s, w_ref[0].astype(BF16)) + b_ref[0]


def _ada_call(cvec, ada_w, ada_b):
    n_layers, d, n6 = ada_w.shape
    rows = cvec.shape[0]
    return pl.pallas_call(
        _ada_kernel,
        grid=(n_layers, n6 // ADA_COLS),
        in_specs=[
            pl.BlockSpec((rows, d), lambda l, j: (0, 0)),
            pl.BlockSpec((1, d, ADA_COLS), lambda l, j: (l, 0, j)),
            pl.BlockSpec((1, 1, ADA_COLS), lambda l, j: (l, 0, j)),
        ],
        out_specs=pl.BlockSpec((1, rows, ADA_COLS), lambda l, j: (l, 0, j)),
        out_shape=jax.ShapeDtypeStruct((n_layers, rows, n6), F32),
        compiler_params=_cparams(("arbitrary", "arbitrary")),
        name="adaln",
    )(cvec, ada_w, ada_b.reshape(n_layers, 1, n6))


def _pre_kernel(x_ref, xp_ref, xn_ref, mod_ref, n1g_ref, win_ref, qg_ref, wqT_ref, wqrT_ref,
                kvg_ref, wk_ref, wvT_ref, cosk_ref, sink_ref, cosq_ref, sinq_ref,
                convw_ref, convb_ref, clng_ref, clnb_ref, slng_ref, slnb_ref, sguw_ref, sgub_ref,
                qT_out, k_out, vT_out, yc_out, ys_out, ybuf, yshift, *, n_lat_tiles, n_ctx):
    i = pl.program_id(1)
    tt = x_ref.shape[1]
    subs = [(r, SUB_TILE) for r in range(0, tt, SUB_TILE)]
    sh1 = mod_ref[0, 0, 0:1, :]
    gain1 = n1g_ref[0] * (1.0 + mod_ref[0, 0, 1:2, :])

    def normmod(xt):
        return (_rms(xt, gain1) + sh1).astype(BF16)

    zs = [_dot(normmod(x_ref[0, r:r + n, :]), win_ref[0]) for r, n in subs]

    def glu(zc):
        return zc[:, :CONV_CH] * _sigmoid(zc[:, CONV_CH:])

    wconv_in = win_ref[0, :, ZC_CONV:ZC_CONV + 2 * CONV_CH]
    is_ctx = i == n_lat_tiles
    first = jnp.logical_or(i == 0, is_ctx)
    last = jnp.logical_or(i == n_lat_tiles - 1, is_ctx)
    ybuf[0:CONV_HALO, :] = jnp.where(first, 0.0, glu(_dot(normmod(xp_ref[0]), wconv_in)))
    ybuf[CONV_HALO + tt:, :] = jnp.where(last, 0.0, glu(_dot(normmod(xn_ref[0]), wconv_in)))
    for (r, n), z in zip(subs, zs):
        y = glu(z[:, ZC_CONV:ZC_CONV + 2 * CONV_CH])
        if r >= n_ctx:
            y = jnp.where(is_ctx, 0.0, y)
        ybuf[CONV_HALO + r:CONV_HALO + r + n, :] = y

    qscale = ATTN_SCALE * LOG2E
    for (r, n), z in zip(subs, zs):
        qn = _rms(z[:, ZC_Q:ZC_Q + Q_RANK], qg_ref[0]).astype(BF16)
        qT = _dot_nt(wqT_ref[0], qn)
        qrT = _dot_nt(wqrT_ref[0], qn)
        cq = cosq_ref[:, r:r + n]
        sq = sinq_ref[:, r:r + n]
        for h in range(HEADS):
            r0 = h * QK_PAD
            qT_out[0, r0:r0 + NOPE, r:r + n] = (qT[r0:r0 + NOPE, :] * qscale).astype(BF16)
            rot = qT[r0 + NOPE:r0 + NOPE + ROPE, :] * cq + qrT[h * ROPE:(h + 1) * ROPE, :] * sq
            qT_out[0, r0 + NOPE:r0 + NOPE + ROPE, r:r + n] = (rot * qscale).astype(BF16)
            qT_out[0, r0 + NOPE + ROPE:r0 + QK_PAD, r:r + n] = jnp.zeros((QK_PAD - NOPE - ROPE, n), BF16)

        kvn = _rms(z[:, ZC_KV:ZC_KV + KV_RANK], kvg_ref[0]).astype(BF16)
        kext = _dot(kvn, wk_ref[0])
        kr = z[:, ZC_KR:ZC_KR + QK_PAD]
        krope = kr * cosk_ref[r:r + n, :] + pltpu.roll(kr, QK_PAD // 2, 1) * sink_ref[r:r + n, :]
        for h in range(HEADS):
            c0 = h * QK_PAD
            k_out[0, r:r + n, c0:c0 + QK_PAD] = (kext[:, c0:c0 + QK_PAD] + krope).astype(BF16)
        vT = _dot_nt(wvT_ref[0], kvn)
        for h in range(HEADS):
            vT_out[0, h * V_EXT:h * V_EXT + VDIM, r:r + n] = vT[h * VDIM:(h + 1) * VDIM, :].astype(BF16)
            vT_out[0, h * V_EXT + VDIM:(h + 1) * V_EXT, r:r + n] = jnp.ones((V_EXT - VDIM, n), BF16)

    lane = lax.broadcasted_iota(jnp.int32, (CHUNK, SGU_CH), 1)
    bias = sgub_ref[0]
    for (r, n), z in zip(subs, zs):
        zsg = z[:, ZC_SGU:ZC_SGU + 2 * SGU_CH]
        zg = 0.5 * zsg * (1.0 + jnp.tanh(math.sqrt(2.0 / math.pi) * (zsg + 0.044715 * (zsg * zsg * zsg))))
        u = zg[:, :SGU_CH]
        v = _layernorm(zg[:, SGU_CH:], slng_ref[0], slnb_ref[0]).astype(BF16)
        for c in range(n // CHUNK):
            vc = v[c * CHUNK:(c + 1) * CHUNK, :]
            mixed = _dot(sguw_ref[0, 0], vc)
            for h in range(1, SGU_HEADS):
                mixed = jnp.where(lane >= h * SGU_HEAD_DIM, _dot(sguw_ref[0, h], vc), mixed)
            ys_out[0, r + c * CHUNK:r + (c + 1) * CHUNK, :] = (
                u[c * CHUNK:(c + 1) * CHUNK, :] * (mixed + bias)).astype(BF16)

    cw = convw_ref[0]
    accs = [jnp.zeros((n, CONV_CH), F32) for _, n in subs]
    n_a = (CONV_WIDTH + 8) // 8
    for b in range(8):
        src = ybuf
        if b:
            yshift[...] = ybuf[b:b + yshift.shape[0], :]
            src = yshift
        for a in range(n_a):
            k = 8 * a + b - 1
            if 0 <= k < CONV_WIDTH:
                for j, (r, n) in enumerate(subs):
                    accs[j] = accs[j] + src[r + 8 * a:r + 8 * a + n, :] * cw[k:k + 1, :]
    for j, (r, n) in enumerate(subs):
        yc = _layernorm(accs[j] + convb_ref[0], clng_ref[0], clnb_ref[0])
        yc_out[0, r:r + n, :] = (yc * _sigmoid(yc)).astype(BF16)


def _pre_call(layer, xc, modsel, p, n_lat_tiles, n_ctx):
    bsz, ltot, d = xc.shape
    tt = TOK_TILE
    n_tiles = ltot // tt
    hb = tt // CONV_HALO
    n_hblocks = ltot // CONV_HALO

    def lay(shape):
        nd = len(shape)
        return pl.BlockSpec((1,) + shape, lambda b, i: (layer,) + (0,) * nd)

    in_specs = [
        pl.BlockSpec((1, tt, d), lambda b, i: (b, i, 0)),
        pl.BlockSpec((1, CONV_HALO, d), lambda b, i: (b, jnp.maximum(i * hb - 1, 0), 0)),
        pl.BlockSpec((1, CONV_HALO, d), lambda b, i: (b, jnp.minimum((i + 1) * hb, n_hblocks - 1), 0)),
        pl.BlockSpec((1, 1, 8, d), lambda b, i: (b, i // n_lat_tiles, 0, 0)),
        lay((1, d)),
        lay((d, Z_COLS)),
        lay((1, Q_RANK)),
        lay((HEADS * QK_PAD, Q_RANK)),
        lay((HEADS * ROPE, Q_RANK)),
        lay((1, KV_RANK)),
        lay((KV_RANK, HEADS * QK_PAD)),
        lay((HEADS * VDIM, KV_RANK)),
        pl.BlockSpec((tt, QK_PAD), lambda b, i: (i, 0)),
        pl.BlockSpec((tt, QK_PAD), lambda b, i: (i, 0)),
        pl.BlockSpec((ROPE, tt), lambda b, i: (0, i)),
        pl.BlockSpec((ROPE, tt), lambda b, i: (0, i)),
        lay((CONV_WIDTH, CONV_CH)),
        lay((1, CONV_CH)),
        lay((1, CONV_CH)),
        lay((1, CONV_CH)),
        lay((1, SGU_CH)),
        lay((1, SGU_CH)),
        lay((SGU_HEADS, CHUNK, CHUNK)),
        lay((CHUNK, SGU_CH)),
    ]
    out_specs = [
        pl.BlockSpec((1, HEADS * QK_PAD, tt), lambda b, i: (b, 0, i)),
        pl.BlockSpec((1, tt, HEADS * QK_PAD), lambda b, i: (b, i, 0)),
        pl.BlockSpec((1, HEADS * V_EXT, tt), lambda b, i: (b, 0, i)),
        pl.BlockSpec((1, tt, CONV_CH), lambda b, i: (b, i, 0)),
        pl.BlockSpec((1, tt, SGU_CH), lambda b, i: (b, i, 0)),
    ]
    out_shape = [
        jax.ShapeDtypeStruct((bsz, HEADS * QK_PAD, ltot), BF16),
        jax.ShapeDtypeStruct((bsz, ltot, HEADS * QK_PAD), BF16),
        jax.ShapeDtypeStruct((bsz, HEADS * V_EXT, ltot), BF16),
        jax.ShapeDtypeStruct((bsz, ltot, CONV_CH), BF16),
        jax.ShapeDtypeStruct((bsz, ltot, SGU_CH), BF16),
    ]
    return pl.pallas_call(
        functools.partial(_pre_kernel, n_lat_tiles=n_lat_tiles, n_ctx=n_ctx),
        grid=(bsz, n_tiles),
        in_specs=in_specs,
        out_specs=out_specs,
        out_shape=out_shape,
        scratch_shapes=[pltpu.VMEM((tt + 2 * CONV_HALO, CONV_CH), F32),
                        pltpu.VMEM((tt + 2 * CONV_HALO - 8, CONV_CH), F32)],
        compiler_params=_cparams(("arbitrary", "arbitrary")),
        name="pre",
    )(xc, xc, xc, modsel, p["norm1_g"], p["w_in"], p["q_norm_g"], p["wqT"], p["wqrT"],
      p["kv_norm_g"], p["wk"], p["wvT"], p["cosk"], p["sink"], p["cosq"], p["sinq"],
      p["conv_w"], p["conv_b"], p["conv_ln_g"], p["conv_ln_b"], p["sgu_ln_g"], p["sgu_ln_b"],
      p["sgu_w"], p["sgu_bias"])


def _softmax_update(s, vt, m, acc):
    m_new = jnp.maximum(m, jnp.max(s, axis=0, keepdims=True))
    alpha = jnp.exp2(m - m_new)
    pr = jnp.exp2(s - m_new).astype(BF16)
    return m_new, alpha * acc + _dot(vt, pr)


def _softmax_finish(acc):
    return (acc[0:VDIM, :] / acc[VDIM:VDIM + 1, :]).astype(BF16)


def _attn_lat_kernel(qT_ref, k_ref, vT_ref, o_ref, acc_scr, *, n_lat, n_ctx):
    tq = qT_ref.shape[2]
    n_cb = tq // Q_BLOCK
    n_steps = n_lat // KV_STEP

    def cols(c):
        return slice(c * Q_BLOCK, (c + 1) * Q_BLOCK)

    def scores(o, size, c):
        return _dot(k_ref[0, pl.ds(o, size), :], qT_ref[0, :, cols(c)])

    def run(blocks, carry):
        rs, excess = list(carry[0]), list(carry[1])
        n = len(blocks)
        s, pa = {}, {}

        def soft(i):
            c = blocks[i][2]
            si = s.pop(i)
            mx = jnp.max(si, axis=0, keepdims=True)
            r_new = jnp.maximum(rs[c], mx)
            pa[i] = (jnp.exp2(si - rs[c]).astype(BF16), jnp.exp2(rs[c] - r_new))
            excess[c] = jnp.maximum(excess[c], mx - rs[c])
            rs[c] = r_new

        for i in range(min(2, n)):
            s[i] = scores(*blocks[i])
        soft(0)
        for i, (o, size, c) in enumerate(blocks):
            if i + 2 < n:
                s[i + 2] = scores(*blocks[i + 2])
            if i + 1 < n:
                soft(i + 1)
            p, alpha = pa.pop(i)
            acc_scr[:, cols(c)] = (acc_scr[:, cols(c)] + _dot(vT_ref[0, :, pl.ds(o, size)], p)) * alpha
        return tuple(rs), tuple(excess)

    def latent_steps(j, carry):
        offs = [pl.multiple_of((j * KV_UNROLL + u) * KV_STEP, KV_STEP) for u in range(KV_UNROLL)]
        return run([(o, KV_STEP, c) for o in offs for c in range(n_cb)], carry)

    acc_scr[...] = jnp.zeros(acc_scr.shape, F32)
    carry = (tuple(jnp.max(scores(n_lat, R0_KEYS, c), axis=0, keepdims=True) for c in range(n_cb)),
             tuple(jnp.zeros((1, Q_BLOCK), F32) for _ in range(n_cb)))
    carry = run([(n_lat, n_ctx, c) for c in range(n_cb)], carry)
    _, excess = lax.fori_loop(0, n_steps // KV_UNROLL, latent_steps, carry)
    o_ref[0] = _softmax_finish(acc_scr[...])

    @pl.when(jnp.max(jnp.concatenate(excess, axis=1)) > STALE_MAX_HEADROOM)
    def _():
        def exact_step(o, size, carry):
            s = _dot(k_ref[0, pl.ds(o, size), :], qT_ref[0])
            return _softmax_update(s, vT_ref[0, :, pl.ds(o, size)], *carry)

        carry = lax.fori_loop(
            0, n_steps, lambda t, cr: exact_step(pl.multiple_of(t * KV_STEP, KV_STEP), KV_STEP, cr),
            (jnp.full((1, tq), -jnp.inf, F32), jnp.zeros((V_EXT, tq), F32)))
        _, acc = exact_step(n_lat, n_ctx, carry)
        o_ref[0] = _softmax_finish(acc)


def _attn_ctx_kernel(qT_ref, k_ref, vT_ref, a_hbm_ref, o_ref):
    del a_hbm_ref
    s = _dot(k_ref[0], qT_ref[0])
    tq = s.shape[1]
    _, acc = _softmax_update(s, vT_ref[0], jnp.full((1, tq), -jnp.inf, F32), jnp.zeros((V_EXT, tq), F32))
    o_ref[0, :, 0:tq] = _softmax_finish(acc)
    if o_ref.shape[2] > tq:
        o_ref[0, :, tq:] = jnp.zeros((o_ref.shape[1], o_ref.shape[2] - tq), BF16)


def _attn_call(qT, k, vT, n_lat, n_ctx, with_ctx):
    bsz, _, ltot = qT.shape
    tq = Q_TILE
    assert n_lat % (KV_STEP * KV_UNROLL) == 0 and n_lat % tq == 0 and tq % Q_BLOCK == 0
    aT = pl.pallas_call(
        functools.partial(_attn_lat_kernel, n_lat=n_lat, n_ctx=n_ctx),
        grid=(bsz, HEADS, n_lat // tq),
        in_specs=[
            pl.BlockSpec((1, QK_PAD, tq), lambda b, h, q: (b, h, q)),
            pl.BlockSpec((1, ltot, QK_PAD), lambda b, h, q: (b, 0, h)),
            pl.BlockSpec((1, V_EXT, ltot), lambda b, h, q: (b, h, 0)),
        ],
        out_specs=pl.BlockSpec((1, VDIM, tq), lambda b, h, q: (b, h, q)),
        out_shape=jax.ShapeDtypeStruct((bsz, HEADS * VDIM, ltot), BF16),
        scratch_shapes=[pltpu.VMEM((V_EXT, tq), F32)],
        compiler_params=_cparams(("arbitrary", "arbitrary", "arbitrary")),
        name="attn_lat",
    )(qT, k, vT)
    if not with_ctx:
        return aT
    cb = n_lat // n_ctx
    return pl.pallas_call(
        _attn_ctx_kernel,
        grid=(bsz, HEADS),
        in_specs=[
            pl.BlockSpec((1, QK_PAD, n_ctx), lambda b, h: (b, h, cb)),
            pl.BlockSpec((1, n_ctx, QK_PAD), lambda b, h: (b, cb, h)),
            pl.BlockSpec((1, V_EXT, n_ctx), lambda b, h: (b, h, cb)),
            pl.BlockSpec(memory_space=pl.ANY),
        ],
        out_specs=pl.BlockSpec((1, VDIM, TOK_TILE), lambda b, h: (b, h, n_lat // TOK_TILE)),
        out_shape=jax.ShapeDtypeStruct(aT.shape, aT.dtype),
        input_output_aliases={3: 0},
        compiler_params=_cparams(("arbitrary", "arbitrary")),
        name="attn_ctx",
    )(qT, k, vT, aT)


def _post_kernel(x_ref, aT_ref, yc_ref, ys_ref, mod_ref, n2g_ref, wout_ref, w1_ref, w2_ref, fg_ref,
                 o_ref, *, final):
    tt, d = x_ref.shape[1], x_ref.shape[2]
    d_ff = w1_ref.shape[2]
    n_attn = aT_ref.shape[1]
    subs = [(r, SUB_TILE) for r in range(0, tt, SUB_TILE)]
    g1 = mod_ref[0, 0, 2:3, :]
    sh2 = mod_ref[0, 0, 3:4, :]
    gain2 = n2g_ref[0] * (1.0 + mod_ref[0, 0, 4:5, :])
    g2 = mod_ref[0, 0, 5:6, :]
    x1s, hs = [], []
    for r, n in subs:
        y = _dot_tn(aT_ref[0, :, r:r + n], wout_ref[0, 0:n_attn, :])
        y = y + _dot(yc_ref[0, r:r + n, :], wout_ref[0, n_attn:n_attn + CONV_CH, :])
        y = y + _dot(ys_ref[0, r:r + n, :], wout_ref[0, n_attn + CONV_CH:d, :])
        x1 = x_ref[0, r:r + n, :] + g1 * y
        x1s.append(x1)
        hs.append((_rms(x1, gain2) + sh2).astype(BF16))
    for (r, n), x1, h in zip(subs, x1s, hs):
        ff = jnp.zeros(x1.shape, F32)
        for c in range(d_ff // FF_CHUNK):
            f = jnp.maximum(_dot(h, w1_ref[0, :, c * FF_CHUNK:(c + 1) * FF_CHUNK]), 0.0)
            ff = ff + _dot((f * f).astype(BF16), w2_ref[0, c * FF_CHUNK:(c + 1) * FF_CHUNK, :])
        x2 = x1 + g2 * ff
        if final:
            x2 = _rms(x2, fg_ref[...])
        o_ref[0, r:r + n, :] = x2


def _post_call(layer, xc, aT, yc, ys, modsel, p, n_lat_tiles, n_out_tiles, final):
    bsz, _, d = xc.shape
    tt = TOK_TILE
    d_ff = p["w_ff1"].shape[2]

    def lay(shape):
        nd = len(shape)
        return pl.BlockSpec((1,) + shape, lambda b, i: (layer,) + (0,) * nd,
                            pipeline_mode=pl.Buffered(1))

    return pl.pallas_call(
        functools.partial(_post_kernel, final=final),
        grid=(bsz, n_out_tiles),
        in_specs=[
            pl.BlockSpec((1, tt, d), lambda b, i: (b, i, 0)),
            pl.BlockSpec((1, HEADS * VDIM, tt), lambda b, i: (b, 0, i)),
            pl.BlockSpec((1, tt, CONV_CH), lambda b, i: (b, i, 0)),
            pl.BlockSpec((1, tt, SGU_CH), lambda b, i: (b, i, 0)),
            pl.BlockSpec((1, 1, 8, d), lambda b, i: (b, i // n_lat_tiles, 0, 0)),
            pl.BlockSpec((1, 1, d), lambda b, i: (layer, 0, 0)),
            lay((d, d)),
            lay((d, d_ff)),
            lay((d_ff, d)),
            pl.BlockSpec((1, d), lambda b, i: (0, 0)),
        ],
        out_specs=pl.BlockSpec((1, tt, d), lambda b, i: (b, i, 0)),
        out_shape=jax.ShapeDtypeStruct((bsz, n_out_tiles * tt, d), F32),
        compiler_params=_cparams(("arbitrary", "arbitrary")),
        name="post",
    )(xc, aT, yc, ys, modsel, p["norm2_g"], p["w_out"], p["w_ff1"], p["w_ff2"], p["final_g"])


def _rotate_half_cols(w):
    n = ROPE // 4
    return jnp.concatenate([-w[..., n:2 * n], w[..., 0:n], -w[..., 3 * n:4 * n], w[..., 2 * n:3 * n]], -1)


def _prepare(n_lat, n_tail, norm1_g, norm2_g, w_in, q_norm_g, w_uq, kv_norm_g, w_ukv, conv_w, conv_b,
             conv_ln_g, conv_ln_b, sgu_ln_g, sgu_ln_b, sgu_w, sgu_b, w_out, w_ff1, w_ff2, final_g):
    n_layers, d, _ = w_in.shape
    o_q, o_kv, o_kr = 0, Q_RANK, Q_RANK + KV_RANK
    o_conv = o_kr + ROPE
    o_sgu = o_conv + 2 * CONV_CH
    w_kr = w_in[:, :, o_kr:o_kr + ROPE]

    gap = jnp.zeros((n_layers, d, QK_PAD // 2 - ROPE), w_in.dtype)
    kr_slab = jnp.concatenate([_rotate_half_cols(w_kr), gap, w_kr, gap], -1)
    w_in_x = jnp.concatenate([
        w_in[:, :, o_q:o_q + Q_RANK], w_in[:, :, o_kv:o_kv + KV_RANK],
        w_in[:, :, o_conv:o_conv + 2 * CONV_CH], w_in[:, :, o_sgu:o_sgu + 2 * SGU_CH],
        kr_slab], -1).astype(BF16)

    wq = w_uq.reshape(n_layers, Q_RANK, HEADS, NOPE + ROPE)
    wq_pad = jnp.pad(wq, ((0, 0), (0, 0), (0, 0), (0, QK_PAD - NOPE - ROPE)))
    wqT = wq_pad.reshape(n_layers, Q_RANK, HEADS * QK_PAD).transpose(0, 2, 1).astype(BF16)
    wqr = _rotate_half_cols(wq[..., NOPE:])
    wqrT = wqr.reshape(n_layers, Q_RANK, HEADS * ROPE).transpose(0, 2, 1).astype(BF16)

    wkv = w_ukv.reshape(n_layers, KV_RANK, HEADS, NOPE + VDIM)
    wk = jnp.pad(wkv[..., :NOPE], ((0, 0), (0, 0), (0, 0), (0, QK_PAD - NOPE)))
    wk = wk.reshape(n_layers, KV_RANK, HEADS * QK_PAD).astype(BF16)
    wvT = wkv[..., NOPE:].reshape(n_layers, KV_RANK, HEADS * VDIM).transpose(0, 2, 1).astype(BF16)

    t = jnp.arange(n_lat, dtype=jnp.int32)
    n = ROPE // 4
    inv = 1.0 / (ROPE_BASE ** (jnp.arange(n, dtype=F32) / n))
    ang_r = (t // GRID_W).astype(F32)[:, None] * inv
    ang_c = (t % GRID_W).astype(F32)[:, None] * inv
    cos32 = jnp.concatenate([jnp.cos(ang_r)] * 2 + [jnp.cos(ang_c)] * 2, -1)
    sin32 = jnp.concatenate([jnp.sin(ang_r)] * 2 + [jnp.sin(ang_c)] * 2, -1)
    cos32 = jnp.concatenate([cos32, jnp.ones((n_tail, ROPE), F32)], 0)
    sin32 = jnp.concatenate([sin32, jnp.zeros((n_tail, ROPE), F32)], 0)
    pad = ((0, 0), (NOPE, QK_PAD - NOPE - ROPE))

    sgu_bias = jnp.broadcast_to(jnp.swapaxes(sgu_b, 1, 2)[:, :, :, None],
                                (n_layers, CHUNK, SGU_HEADS, SGU_HEAD_DIM)).reshape(n_layers, CHUNK, SGU_CH)

    def row(a):
        return a.reshape(n_layers, 1, a.shape[-1])

    return dict(
        norm1_g=row(norm1_g), norm2_g=row(norm2_g), w_in=w_in_x, q_norm_g=row(q_norm_g), wqT=wqT, wqrT=wqrT,
        kv_norm_g=row(kv_norm_g), wk=wk, wvT=wvT,
        cosk=jnp.pad(cos32, pad), sink=jnp.pad(sin32, pad), cosq=cos32.T, sinq=sin32.T,
        conv_w=conv_w, conv_b=row(conv_b), conv_ln_g=row(conv_ln_g), conv_ln_b=row(conv_ln_b),
        sgu_ln_g=row(sgu_ln_g), sgu_ln_b=row(sgu_ln_b), sgu_w=sgu_w.astype(BF16), sgu_bias=sgu_bias,
        w_out=w_out.astype(BF16), w_ff1=w_ff1.astype(BF16), w_ff2=w_ff2.astype(BF16),
        final_g=final_g.reshape(1, -1))


def kernel(x, c, ctx, c_ctx, ada_w, ada_b, norm1_g, norm2_g, w_in, q_norm_g, w_uq, kv_norm_g, w_ukv,
           conv_w, conv_b, conv_ln_g, conv_ln_b, sgu_ln_g, sgu_ln_b, sgu_w, sgu_b, w_out, w_ff1, w_ff2,
           final_g):
    bsz, n_lat, d = x.shape
    n_ctx = ctx.shape[1]
    n_layers = w_in.shape[0]
    assert n_lat % TOK_TILE == 0 and n_lat % GRID_W == 0
    assert n_ctx <= TOK_TILE and n_ctx % SUB_TILE == 0
    assert n_lat % n_ctx == 0
    n_lat_tiles = n_lat // TOK_TILE
    n_tiles = n_lat_tiles + 1

    p = _prepare(n_lat, TOK_TILE, norm1_g, norm2_g, w_in, q_norm_g, w_uq, kv_norm_g, w_ukv, conv_w, conv_b,
                 conv_ln_g, conv_ln_b, sgu_ln_g, sgu_ln_b, sgu_w, sgu_b, w_out, w_ff1, w_ff2, final_g)

    rows = -(-(bsz + 1) // 8) * 8
    cvec = jnp.concatenate([c, c_ctx[None, :], jnp.zeros((rows - bsz - 1, d), F32)], 0)
    mods = _ada_call(cvec, ada_w, ada_b).reshape(n_layers, rows, 6, d)
    m_lat = mods[:, :bsz]
    m_ctx = jnp.broadcast_to(mods[:, bsz][:, None], m_lat.shape)
    modsel = jnp.pad(jnp.stack([m_lat, m_ctx], 2), ((0, 0), (0, 0), (0, 0), (0, 2), (0, 0)))

    xc = jnp.concatenate([x, ctx, jnp.zeros((bsz, TOK_TILE - n_ctx, d), x.dtype)], 1)
    for layer in range(n_layers):
        last = layer == n_layers - 1
        n_act = n_lat_tiles if last else n_tiles
        qT, k, vT, yc, ys = _pre_call(layer, xc, modsel[layer], p, n_lat_tiles, n_ctx)
        aT = _attn_call(qT, k, vT, n_lat, n_ctx, not last)
        xc = _post_call(layer, xc, aT, yc, ys, modsel[layer], p, n_lat_tiles, n_act, last)
    return xc
```

```python
import functools
import math

import jax
import jax.numpy as jnp
from jax import lax
from jax.experimental import pallas as pl
from jax.experimental.pallas import tpu as pltpu

F32 = jnp.float32
BF16 = jnp.bfloat16

GRID_W = 64
HEADS = 8
NOPE = 64
ROPE = 32
VDIM = 64
QK_PAD = 128
REF_LANE = NOPE + ROPE
REF_ROWS = 16
Q_RANK = 384
KV_RANK = 256
CONV_CH = 256
CONV_WIDTH = 31
CONV_HALO = 16
SGU_HEADS = 4
SGU_HEAD_DIM = 64
SGU_CH = 256
CHUNK = 128
EPS = 1e-6
ROPE_BASE = 10000.0
ATTN_SCALE = (NOPE + ROPE) ** -0.5
LOG2E = math.log2(math.e)

TOK_TILE = 512
SUB_TILE = 256
Q_TILE = 4096
Q_BLOCK = 512
KV_STEP = 256
KV_UNROLL = 4
R0_KEYS = 16
STALE_MAX_HEADROOM = 64.0
V_EXT = VDIM + 16
FF_CHUNK = 1024
ADA_COLS = 1536
VMEM_LIMIT_BYTES = 56 * 1024 * 1024

ZC_Q = 0
ZC_KV = ZC_Q + Q_RANK
ZC_CONV = ZC_KV + KV_RANK
ZC_SGU = ZC_CONV + 2 * CONV_CH
ZC_KR = ZC_SGU + 2 * SGU_CH
Z_COLS = ZC_KR + QK_PAD


def _cparams(sem):
    return pltpu.CompilerParams(dimension_semantics=sem, vmem_limit_bytes=VMEM_LIMIT_BYTES)


def _sigmoid(v):
    return 1.0 / (1.0 + jnp.exp(-v))


def _layernorm(v, g, b):
    mu = jnp.mean(v, -1, keepdims=True)
    d = v - mu
    var = jnp.mean(d * d, -1, keepdims=True)
    return d * lax.rsqrt(var + EPS) * g + b


def _rms(v, g):
    return v * lax.rsqrt(jnp.mean(v * v, -1, keepdims=True) + EPS) * g


def _dot(a, b):
    return jnp.dot(a, b, preferred_element_type=F32)


def _dot_nt(a, b):
    return lax.dot_general(a, b, (((1,), (1,)), ((), ())), preferred_element_type=F32)


def _dot_tn(a, b):
    return lax.dot_general(a, b, (((0,), (0,)), ((), ())), preferred_element_type=F32)


def _ada_kernel(c_ref, w_ref, b_ref, o_ref):
    c = c_ref[...]
    s = (c * _sigmoid(c)).astype(BF16)
    o_ref[0] = _dot(s, w_ref[0].astype(BF16)) + b_ref[0]


def _ada_call(cvec, ada_w, ada_b):
    n_layers, d, n6 = ada_w.shape
    rows = cvec.shape[0]
    return pl.pallas_call(
        _ada_kernel,
        grid=(n_layers, n6 // ADA_COLS),
        in_specs=[
            pl.BlockSpec((rows, d), lambda l, j: (0, 0)),
            pl.BlockSpec((1, d, ADA_COLS), lambda l, j: (l, 0, j)),
            pl.BlockSpec((1, 1, ADA_COLS), lambda l, j: (l, 0, j)),
        ],
        out_specs=pl.BlockSpec((1, rows, ADA_COLS), lambda l, j: (l, 0, j)),
        out_shape=jax.ShapeDtypeStruct((n_layers, rows, n6), F32),
        compiler_params=_cparams(("arbitrary", "arbitrary")),
        name="adaln",
    )(cvec, ada_w, ada_b.reshape(n_layers, 1, n6))


def _pre_kernel(x_ref, xp_ref, xn_ref, mod_ref, n1g_ref, win_ref, qg_ref, wqT_ref, wqrT_ref,
                kvg_ref, wk_ref, wvT_ref, cosk_ref, sink_ref, cosq_ref, sinq_ref,
                convw_ref, convb_ref, clng_ref, clnb_ref, slng_ref, slnb_ref, sguw_ref, sgub_ref,
                qT_out, k_out, vT_out, yc_out, ys_out, ybuf, yshift, *, n_lat_tiles, n_ctx):
    i = pl.program_id(1)
    tt = x_ref.shape[1]
    subs = [(r, SUB_TILE) for r in range(0, tt, SUB_TILE)]
    sh1 = mod_ref[0, 0, 0:1, :]
    gain1 = n1g_ref[0] * (1.0 + mod_ref[0, 0, 1:2, :])

    def normmod(xt):
        return (_rms(xt, gain1) + sh1).astype(BF16)

    zs = [_dot(normmod(x_ref[0, r:r + n, :]), win_ref[0]) for r, n in subs]

    def glu(zc):
        return zc[:, :CONV_CH] * _sigmoid(zc[:, CONV_CH:])

    wconv_in = win_ref[0, :, ZC_CONV:ZC_CONV + 2 * CONV_CH]
    is_ctx = i == n_lat_tiles
    first = jnp.logical_or(i == 0, is_ctx)
    last = jnp.logical_or(i == n_lat_tiles - 1, is_ctx)
    ybuf[0:CONV_HALO, :] = jnp.where(first, 0.0, glu(_dot(normmod(xp_ref[0]), wconv_in)))
    ybuf[CONV_HALO + tt:, :] = jnp.where(last, 0.0, glu(_dot(normmod(xn_ref[0]), wconv_in)))
    for (r, n), z in zip(subs, zs):
        y = glu(z[:, ZC_CONV:ZC_CONV + 2 * CONV_CH])
        if r >= n_ctx:
            y = jnp.where(is_ctx, 0.0, y)
        ybuf[CONV_HALO + r:CONV_HALO + r + n, :] = y

    qscale = ATTN_SCALE * LOG2E
    for (r, n), z in zip(subs, zs):
        qn = _rms(z[:, ZC_Q:ZC_Q + Q_RANK], qg_ref[0]).astype(BF16)
        qT = _dot_nt(wqT_ref[0], qn)
        qrT = _dot_nt(wqrT_ref[0], qn)
        cq = cosq_ref[:, r:r + n]
        sq = sinq_ref[:, r:r + n]
        for h in range(HEADS):
            r0 = h * QK_PAD
            qT_out[0, r0:r0 + NOPE, r:r + n] = (qT[r0:r0 + NOPE, :] * qscale).astype(BF16)
            rot = qT[r0 + NOPE:r0 + NOPE + ROPE, :] * cq + qrT[h * ROPE:(h + 1) * ROPE, :] * sq
            qT_out[0, r0 + NOPE:r0 + NOPE + ROPE, r:r + n] = (rot * qscale).astype(BF16)
            qT_out[0, r0 + NOPE + ROPE:r0 + QK_PAD, r:r + n] = jnp.zeros((QK_PAD - NOPE - ROPE, n), BF16)

        kvn = _rms(z[:, ZC_KV:ZC_KV + KV_RANK], kvg_ref[0]).astype(BF16)
        kext = _dot(kvn, wk_ref[0])
        kr = z[:, ZC_KR:ZC_KR + QK_PAD]
        krope = kr * cosk_ref[r:r + n, :] + pltpu.roll(kr, QK_PAD // 2, 1) * sink_ref[r:r + n, :]
        krope = krope + (lax.broadcasted_iota(jnp.int32, (1, QK_PAD), 1) == REF_LANE).astype(F32)
        for h in range(HEADS):
            c0 = h * QK_PAD
            k_out[0, r:r + n, c0:c0 + QK_PAD] = (kext[:, c0:c0 + QK_PAD] + krope).astype(BF16)
        vT = _dot_nt(wvT_ref[0], kvn)
        for h in range(HEADS):
            vT_out[0, h * V_EXT:h * V_EXT + VDIM, r:r + n] = vT[h * VDIM:(h + 1) * VDIM, :].astype(BF16)
            vT_out[0, h * V_EXT + VDIM:(h + 1) * V_EXT, r:r + n] = jnp.ones((V_EXT - VDIM, n), BF16)

    lane = lax.broadcasted_iota(jnp.int32, (CHUNK, SGU_CH), 1)
    bias = sgub_ref[0]
    for (r, n), z in zip(subs, zs):
        zsg = z[:, ZC_SGU:ZC_SGU + 2 * SGU_CH]
        zg = 0.5 * zsg * (1.0 + jnp.tanh(math.sqrt(2.0 / math.pi) * (zsg + 0.044715 * (zsg * zsg * zsg))))
        u = zg[:, :SGU_CH]
        v = _layernorm(zg[:, SGU_CH:], slng_ref[0], slnb_ref[0]).astype(BF16)
        for c in range(n // CHUNK):
            vc = v[c * CHUNK:(c + 1) * CHUNK, :]
            mixed = _dot(sguw_ref[0, 0], vc)
            for h in range(1, SGU_HEADS):
                mixed = jnp.where(lane >= h * SGU_HEAD_DIM, _dot(sguw_ref[0, h], vc), mixed)
            ys_out[0, r + c * CHUNK:r + (c + 1) * CHUNK, :] = (
                u[c * CHUNK:(c + 1) * CHUNK, :] * (mixed + bias)).astype(BF16)

    cw = convw_ref[0]
    accs = [jnp.zeros((n, CONV_CH), F32) for _, n in subs]
    n_a = (CONV_WIDTH + 8) // 8
    for b in range(8):
        src = ybuf
        if b:
            yshift[...] = ybuf[b:b + yshift.shape[0], :]
            src = yshift
        for a in range(n_a):
            k = 8 * a + b - 1
            if 0 <= k < CONV_WIDTH:
                for j, (r, n) in enumerate(subs):
                    accs[j] = accs[j] + src[r + 8 * a:r + 8 * a + n, :] * cw[k:k + 1, :]
    for j, (r, n) in enumerate(subs):
        yc = _layernorm(accs[j] + convb_ref[0], clng_ref[0], clnb_ref[0])
        yc_out[0, r:r + n, :] = (yc * _sigmoid(yc)).astype(BF16)


def _pre_call(layer, xc, modsel, p, n_lat_tiles, n_ctx):
    bsz, ltot, d = xc.shape
    tt = TOK_TILE
    n_tiles = ltot // tt
    hb = tt // CONV_HALO
    n_hblocks = ltot // CONV_HALO

    def lay(shape):
        nd = len(shape)
        return pl.BlockSpec((1,) + shape, lambda b, i: (layer,) + (0,) * nd)

    in_specs = [
        pl.BlockSpec((1, tt, d), lambda b, i: (b, i, 0)),
        pl.BlockSpec((1, CONV_HALO, d), lambda b, i: (b, jnp.maximum(i * hb - 1, 0), 0)),
        pl.BlockSpec((1, CONV_HALO, d), lambda b, i: (b, jnp.minimum((i + 1) * hb, n_hblocks - 1), 0)),
        pl.BlockSpec((1, 1, 8, d), lambda b, i: (b, i // n_lat_tiles, 0, 0)),
        lay((1, d)),
        lay((d, Z_COLS)),
        lay((1, Q_RANK)),
        lay((HEADS * QK_PAD, Q_RANK)),
        lay((HEADS * ROPE, Q_RANK)),
        lay((1, KV_RANK)),
        lay((KV_RANK, HEADS * QK_PAD)),
        lay((HEADS * VDIM, KV_RANK)),
        pl.BlockSpec((tt, QK_PAD), lambda b, i: (i, 0)),
        pl.BlockSpec((tt, QK_PAD), lambda b, i: (i, 0)),
        pl.BlockSpec((ROPE, tt), lambda b, i: (0, i)),
        pl.BlockSpec((ROPE, tt), lambda b, i: (0, i)),
        lay((CONV_WIDTH, CONV_CH)),
        lay((1, CONV_CH)),
        lay((1, CONV_CH)),
        lay((1, CONV_CH)),
        lay((1, SGU_CH)),
        lay((1, SGU_CH)),
        lay((SGU_HEADS, CHUNK, CHUNK)),
        lay((CHUNK, SGU_CH)),
    ]
    out_specs = [
        pl.BlockSpec((1, HEADS * QK_PAD, tt), lambda b, i: (b, 0, i)),
        pl.BlockSpec((1, tt, HEADS * QK_PAD), lambda b, i: (b, i, 0)),
        pl.BlockSpec((1, HEADS * V_EXT, tt), lambda b, i: (b, 0, i)),
        pl.BlockSpec((1, tt, CONV_CH), lambda b, i: (b, i, 0)),
        pl.BlockSpec((1, tt, SGU_CH), lambda b, i: (b, i, 0)),
    ]
    out_shape = [
        jax.ShapeDtypeStruct((bsz, HEADS * QK_PAD, ltot), BF16),
        jax.ShapeDtypeStruct((bsz, ltot, HEADS * QK_PAD), BF16),
        jax.ShapeDtypeStruct((bsz, HEADS * V_EXT, ltot), BF16),
        jax.ShapeDtypeStruct((bsz, ltot, CONV_CH), BF16),
        jax.ShapeDtypeStruct((bsz, ltot, SGU_CH), BF16),
    ]
    return pl.pallas_call(
        functools.partial(_pre_kernel, n_lat_tiles=n_lat_tiles, n_ctx=n_ctx),
        grid=(bsz, n_tiles),
        in_specs=in_specs,
        out_specs=out_specs,
        out_shape=out_shape,
        scratch_shapes=[pltpu.VMEM((tt + 2 * CONV_HALO, CONV_CH), F32),
                        pltpu.VMEM((tt + 2 * CONV_HALO - 8, CONV_CH), F32)],
        compiler_params=_cparams(("arbitrary", "arbitrary")),
        name="pre",
    )(xc, xc, xc, modsel, p["norm1_g"], p["w_in"], p["q_norm_g"], p["wqT"], p["wqrT"],
      p["kv_norm_g"], p["wk"], p["wvT"], p["cosk"], p["sink"], p["cosq"], p["sinq"],
      p["conv_w"], p["conv_b"], p["conv_ln_g"], p["conv_ln_b"], p["sgu_ln_g"], p["sgu_ln_b"],
      p["sgu_w"], p["sgu_bias"])


def _softmax_update(s, vt, m, acc):
    m_new = jnp.maximum(m, jnp.max(s, axis=0, keepdims=True))
    alpha = jnp.exp2(m - m_new)
    pr = jnp.exp2(s - m_new).astype(BF16)
    return m_new, alpha * acc + _dot(vt, pr)


def _bf16_exact(v):
    return v.astype(BF16).astype(F32)


def _softmax_finish(acc):
    return (acc[0:VDIM, :] / acc[VDIM:VDIM + 1, :]).astype(BF16)


def _attn_lat_kernel(qT_ref, k_ref, vT_ref, o_ref, acc_scr, *, n_lat, n_ctx):
    tq = qT_ref.shape[2]
    n_cb = tq // Q_BLOCK
    n_steps = n_lat // KV_STEP

    def cols(c):
        return slice(c * Q_BLOCK, (c + 1) * Q_BLOCK)

    def scores(o, size, c, r=None):
        q = qT_ref[0, :, cols(c)]
        if r is not None:
            first_row = lax.broadcasted_iota(jnp.int32, (REF_ROWS, Q_BLOCK), 0) == 0
            q = jnp.concatenate([q[0:REF_LANE], jnp.where(first_row, -r, 0.0).astype(BF16),
                                 q[REF_LANE + REF_ROWS:]], axis=0)
        return _dot(k_ref[0, pl.ds(o, size), :], q)

    def run(blocks, carry):
        rs, excess = list(carry[0]), list(carry[1])
        n = len(blocks)
        s, pa = {}, {}

        def soft(i):
            c = blocks[i][2]
            si = s.pop(i)
            mx = jnp.max(si, axis=0, keepdims=True)
            r_new = _bf16_exact(rs[c] + jnp.maximum(mx, 0.0))
            pa[i] = (jnp.exp2(si).astype(BF16), jnp.exp2(rs[c] - r_new))
            excess[c] = jnp.maximum(excess[c], mx)
            rs[c] = r_new

        for i in range(min(2, n)):
            s[i] = scores(*blocks[i], rs[blocks[i][2]])
        soft(0)
        for i, (o, size, c) in enumerate(blocks):
            if i + 2 < n:
                s[i + 2] = scores(*blocks[i + 2], rs[blocks[i + 2][2]])
            if i + 1 < n:
                soft(i + 1)
            p, alpha = pa.pop(i)
            acc_scr[:, cols(c)] = (acc_scr[:, cols(c)] + _dot(vT_ref[0, :, pl.ds(o, size)], p)) * alpha
        return tuple(rs), tuple(excess)

    def latent_steps(j, carry):
        offs = [pl.multiple_of((j * KV_UNROLL + u) * KV_STEP, KV_STEP) for u in range(KV_UNROLL)]
        return run([(o, KV_STEP, c) for o in offs for c in range(n_cb)], carry)

    acc_scr[...] = jnp.zeros(acc_scr.shape, F32)
    carry = (tuple(_bf16_exact(jnp.max(scores(n_lat, R0_KEYS, c), axis=0, keepdims=True)) for c in range(n_cb)),
             tuple(jnp.zeros((1, Q_BLOCK), F32) for _ in range(n_cb)))
    carry = run([(n_lat, n_ctx, c) for c in range(n_cb)], carry)
    _, excess = lax.fori_loop(0, n_steps // KV_UNROLL, latent_steps, carry)
    o_ref[0] = _softmax_finish(acc_scr[...])

    @pl.when(jnp.max(jnp.concatenate(excess, axis=1)) > STALE_MAX_HEADROOM)
    def _():
        def exact_step(o, size, carry):
            s = _dot(k_ref[0, pl.ds(o, size), :], qT_ref[0])
            return _softmax_update(s, vT_ref[0, :, pl.ds(o, size)], *carry)

        carry = lax.fori_loop(
            0, n_steps, lambda t, cr: exact_step(pl.multiple_of(t * KV_STEP, KV_STEP), KV_STEP, cr),
            (jnp.full((1, tq), -jnp.inf, F32), jnp.zeros((V_EXT, tq), F32)))
        _, acc = exact_step(n_lat, n_ctx, carry)
        o_ref[0] = _softmax_finish(acc)


def _attn_ctx_kernel(qT_ref, k_ref, vT_ref, a_hbm_ref, o_ref):
    del a_hbm_ref
    s = _dot(k_ref[0], qT_ref[0])
    tq = s.shape[1]
    _, acc = _softmax_update(s, vT_ref[0], jnp.full((1, tq), -jnp.inf, F32), jnp.zeros((V_EXT, tq), F32))
    o_ref[0, :, 0:tq] = _softmax_finish(acc)
    if o_ref.shape[2] > tq:
        o_ref[0, :, tq:] = jnp.zeros((o_ref.shape[1], o_ref.shape[2] - tq), BF16)


def _attn_call(qT, k, vT, n_lat, n_ctx, with_ctx):
    bsz, _, ltot = qT.shape
    tq = Q_TILE
    assert n_lat % (KV_STEP * KV_UNROLL) == 0 and n_lat % tq == 0 and tq % Q_BLOCK == 0
    aT = pl.pallas_call(
        functools.partial(_attn_lat_kernel, n_lat=n_lat, n_ctx=n_ctx),
        grid=(bsz, HEADS, n_lat // tq),
        in_specs=[
            pl.BlockSpec((1, QK_PAD, tq), lambda b, h, q: (b, h, q)),
            pl.BlockSpec((1, ltot, QK_PAD), lambda b, h, q: (b, 0, h)),
            pl.BlockSpec((1, V_EXT, ltot), lambda b, h, q: (b, h, 0)),
        ],
        out_specs=pl.BlockSpec((1, VDIM, tq), lambda b, h, q: (b, h, q)),
        out_shape=jax.ShapeDtypeStruct((bsz, HEADS * VDIM, ltot), BF16),
        scratch_shapes=[pltpu.VMEM((V_EXT, tq), F32)],
        compiler_params=_cparams(("arbitrary", "arbitrary", "arbitrary")),
        name="attn_lat",
    )(qT, k, vT)
    if not with_ctx:
        return aT
    cb = n_lat // n_ctx
    return pl.pallas_call(
        _attn_ctx_kernel,
        grid=(bsz, HEADS),
        in_specs=[
            pl.BlockSpec((1, QK_PAD, n_ctx), lambda b, h: (b, h, cb)),
            pl.BlockSpec((1, n_ctx, QK_PAD), lambda b, h: (b, cb, h)),
            pl.BlockSpec((1, V_EXT, n_ctx), lambda b, h: (b, h, cb)),
            pl.BlockSpec(memory_space=pl.ANY),
        ],
        out_specs=pl.BlockSpec((1, VDIM, TOK_TILE), lambda b, h: (b, h, n_lat // TOK_TILE)),
        out_shape=jax.ShapeDtypeStruct(aT.shape, aT.dtype),
        input_output_aliases={3: 0},
        compiler_params=_cparams(("arbitrary", "arbitrary")),
        name="attn_ctx",
    )(qT, k, vT, aT)


def _post_kernel(x_ref, aT_ref, yc_ref, ys_ref, mod_ref, n2g_ref, wout_ref, w1_ref, w2_ref, fg_ref,
                 o_ref, *, final):
    tt, d = x_ref.shape[1], x_ref.shape[2]
    d_ff = w1_ref.shape[2]
    n_attn = aT_ref.shape[1]
    subs = [(r, SUB_TILE) for r in range(0, tt, SUB_TILE)]
    g1 = mod_ref[0, 0, 2:3, :]
    sh2 = mod_ref[0, 0, 3:4, :]
    gain2 = n2g_ref[0] * (1.0 + mod_ref[0, 0, 4:5, :])
    g2 = mod_ref[0, 0, 5:6, :]
    x1s, hs = [], []
    for r, n in subs:
        y = _dot_tn(aT_ref[0, :, r:r + n], wout_ref[0, 0:n_attn, :])
        y = y + _dot(yc_ref[0, r:r + n, :], wout_ref[0, n_attn:n_attn + CONV_CH, :])
        y = y + _dot(ys_ref[0, r:r + n, :], wout_ref[0, n_attn + CONV_CH:d, :])
        x1 = x_ref[0, r:r + n, :] + g1 * y
        x1s.append(x1)
        hs.append((_rms(x1, gain2) + sh2).astype(BF16))
    for (r, n), x1, h in zip(subs, x1s, hs):
        ff = jnp.zeros(x1.shape, F32)
        for c in range(d_ff // FF_CHUNK):
            f = jnp.maximum(_dot(h, w1_ref[0, :, c * FF_CHUNK:(c + 1) * FF_CHUNK]), 0.0)
            ff = ff + _dot((f * f).astype(BF16), w2_ref[0, c * FF_CHUNK:(c + 1) * FF_CHUNK, :])
        x2 = x1 + g2 * ff
        if final:
            x2 = _rms(x2, fg_ref[...])
        o_ref[0, r:r + n, :] = x2


def _post_call(layer, xc, aT, yc, ys, modsel, p, n_lat_tiles, n_out_tiles, final):
    bsz, _, d = xc.shape
    tt = TOK_TILE
    d_ff = p["w_ff1"].shape[2]

    def lay(shape):
        nd = len(shape)
        return pl.BlockSpec((1,) + shape, lambda b, i: (layer,) + (0,) * nd,
                            pipeline_mode=pl.Buffered(1))

    return pl.pallas_call(
        functools.partial(_post_kernel, final=final),
        grid=(bsz, n_out_tiles),
        in_specs=[
            pl.BlockSpec((1, tt, d), lambda b, i: (b, i, 0)),
            pl.BlockSpec((1, HEADS * VDIM, tt), lambda b, i: (b, 0, i)),
            pl.BlockSpec((1, tt, CONV_CH), lambda b, i: (b, i, 0)),
            pl.BlockSpec((1, tt, SGU_CH), lambda b, i: (b, i, 0)),
            pl.BlockSpec((1, 1, 8, d), lambda b, i: (b, i // n_lat_tiles, 0, 0)),
            pl.BlockSpec((1, 1, d), lambda b, i: (layer, 0, 0)),
            lay((d, d)),
            lay((d, d_ff)),
            lay((d_ff, d)),
            pl.BlockSpec((1, d), lambda b, i: (0, 0)),
        ],
        out_specs=pl.BlockSpec((1, tt, d), lambda b, i: (b, i, 0)),
        out_shape=jax.ShapeDtypeStruct((bsz, n_out_tiles * tt, d), F32),
        compiler_params=_cparams(("arbitrary", "arbitrary")),
        name="post",
    )(xc, aT, yc, ys, modsel, p["norm2_g"], p["w_out"], p["w_ff1"], p["w_ff2"], p["final_g"])


def _rotate_half_cols(w):
    n = ROPE // 4
    return jnp.concatenate([-w[..., n:2 * n], w[..., 0:n], -w[..., 3 * n:4 * n], w[..., 2 * n:3 * n]], -1)


def _prepare(n_lat, n_tail, norm1_g, norm2_g, w_in, q_norm_g, w_uq, kv_norm_g, w_ukv, conv_w, conv_b,
             conv_ln_g, conv_ln_b, sgu_ln_g, sgu_ln_b, sgu_w, sgu_b, w_out, w_ff1, w_ff2, final_g):
    n_layers, d, _ = w_in.shape
    o_q, o_kv, o_kr = 0, Q_RANK, Q_RANK + KV_RANK
    o_conv = o_kr + ROPE
    o_sgu = o_conv + 2 * CONV_CH
    w_kr = w_in[:, :, o_kr:o_kr + ROPE]

    gap = jnp.zeros((n_layers, d, QK_PAD // 2 - ROPE), w_in.dtype)
    kr_slab = jnp.concatenate([_rotate_half_cols(w_kr), gap, w_kr, gap], -1)
    w_in_x = jnp.concatenate([
        w_in[:, :, o_q:o_q + Q_RANK], w_in[:, :, o_kv:o_kv + KV_RANK],
        w_in[:, :, o_conv:o_conv + 2 * CONV_CH], w_in[:, :, o_sgu:o_sgu + 2 * SGU_CH],
        kr_slab], -1).astype(BF16)

    wq = w_uq.reshape(n_layers, Q_RANK, HEADS, NOPE + ROPE)
    wq_pad = jnp.pad(wq, ((0, 0), (0, 0), (0, 0), (0, QK_PAD - NOPE - ROPE)))
    wqT = wq_pad.reshape(n_layers, Q_RANK, HEADS * QK_PAD).transpose(0, 2, 1).astype(BF16)
    wqr = _rotate_half_cols(wq[..., NOPE:])
    wqrT = wqr.reshape(n_layers, Q_RANK, HEADS * ROPE).transpose(0, 2, 1).astype(BF16)

    wkv = w_ukv.reshape(n_layers, KV_RANK, HEADS, NOPE + VDIM)
    wk = jnp.pad(wkv[..., :NOPE], ((0, 0), (0, 0), (0, 0), (0, QK_PAD - NOPE)))
    wk = wk.reshape(n_layers, KV_RANK, HEADS * QK_PAD).astype(BF16)
    wvT = wkv[..., NOPE:].reshape(n_layers, KV_RANK, HEADS * VDIM).transpose(0, 2, 1).astype(BF16)

    t = jnp.arange(n_lat, dtype=jnp.int32)
    n = ROPE // 4
    inv = 1.0 / (ROPE_BASE ** (jnp.arange(n, dtype=F32) / n))
    ang_r = (t // GRID_W).astype(F32)[:, None] * inv
    ang_c = (t % GRID_W).astype(F32)[:, None] * inv
    cos32 = jnp.concatenate([jnp.cos(ang_r)] * 2 + [jnp.cos(ang_c)] * 2, -1)
    sin32 = jnp.concatenate([jnp.sin(ang_r)] * 2 + [jnp.sin(ang_c)] * 2, -1)
    cos32 = jnp.concatenate([cos32, jnp.ones((n_tail, ROPE), F32)], 0)
    sin32 = jnp.concatenate([sin32, jnp.zeros((n_tail, ROPE), F32)], 0)
    pad = ((0, 0), (NOPE, QK_PAD - NOPE - ROPE))

    sgu_bias = jnp.broadcast_to(jnp.swapaxes(sgu_b, 1, 2)[:, :, :, None],
                                (n_layers, CHUNK, SGU_HEADS, SGU_HEAD_DIM)).reshape(n_layers, CHUNK, SGU_CH)

    def row(a):
        return a.reshape(n_layers, 1, a.shape[-1])

    return dict(
        norm1_g=row(norm1_g), norm2_g=row(norm2_g), w_in=w_in_x, q_norm_g=row(q_norm_g), wqT=wqT, wqrT=wqrT,
        kv_norm_g=row(kv_norm_g), wk=wk, wvT=wvT,
        cosk=jnp.pad(cos32, pad), sink=jnp.pad(sin32, pad), cosq=cos32.T, sinq=sin32.T,
        conv_w=conv_w, conv_b=row(conv_b), conv_ln_g=row(conv_ln_g), conv_ln_b=row(conv_ln_b),
        sgu_ln_g=row(sgu_ln_g), sgu_ln_b=row(sgu_ln_b), sgu_w=sgu_w.astype(BF16), sgu_bias=sgu_bias,
        w_out=w_out.astype(BF16), w_ff1=w_ff1.astype(BF16), w_ff2=w_ff2.astype(BF16),
        final_g=final_g.reshape(1, -1))


def kernel(x, c, ctx, c_ctx, ada_w, ada_b, norm1_g, norm2_g, w_in, q_norm_g, w_uq, kv_norm_g, w_ukv,
           conv_w, conv_b, conv_ln_g, conv_ln_b, sgu_ln_g, sgu_ln_b, sgu_w, sgu_b, w_out, w_ff1, w_ff2,
           final_g):
    bsz, n_lat, d = x.shape
    n_ctx = ctx.shape[1]
    n_layers = w_in.shape[0]
    assert n_lat % TOK_TILE == 0 and n_lat % GRID_W == 0
    assert n_ctx <= TOK_TILE and n_ctx % SUB_TILE == 0
    assert n_lat % n_ctx == 0
    n_lat_tiles = n_lat // TOK_TILE
    n_tiles = n_lat_tiles + 1

    p = _prepare(n_lat, TOK_TILE, norm1_g, norm2_g, w_in, q_norm_g, w_uq, kv_norm_g, w_ukv, conv_w, conv_b,
                 conv_ln_g, conv_ln_b, sgu_ln_g, sgu_ln_b, sgu_w, sgu_b, w_out, w_ff1, w_ff2, final_g)

    rows = -(-(bsz + 1) // 8) * 8
    cvec = jnp.concatenate([c, c_ctx[None, :], jnp.zeros((rows - bsz - 1, d), F32)], 0)
    mods = _ada_call(cvec, ada_w, ada_b).reshape(n_layers, rows, 6, d)
    m_lat = mods[:, :bsz]
    m_ctx = jnp.broadcast_to(mods[:, bsz][:, None], m_lat.shape)
    modsel = jnp.pad(jnp.stack([m_lat, m_ctx], 2), ((0, 0), (0, 0), (0, 0), (0, 2), (0, 0)))

    xc = jnp.concatenate([x, ctx, jnp.zeros((bsz, TOK_TILE - n_ctx, d), x.dtype)], 1)
    for layer in range(n_layers):
        last = layer == n_layers - 1
        n_act = n_lat_tiles if last else n_tiles
        qT, k, vT, yc, ys = _pre_call(layer, xc, modsel[layer], p, n_lat_tiles, n_ctx)
        aT = _attn_call(qT, k, vT, n_lat, n_ctx, not last)
        xc = _post_call(layer, xc, aT, yc, ys, modsel[layer], p, n_lat_tiles, n_act, last)
    return xc
```

```python
import functools
import math

import jax
import jax.numpy as jnp
from jax import lax
from jax.experimental import pallas as pl
from jax.experimental.pallas import tpu as pltpu

F32 = jnp.float32
BF16 = jnp.bfloat16

GRID_W = 64
HEADS = 8
NOPE = 64
ROPE = 32
VDIM = 64
QK_PAD = 128
REF_LANE = NOPE + ROPE
REF_ROWS = 16
Q_RANK = 384
KV_RANK = 256
CONV_CH = 256
CONV_WIDTH = 31
CONV_HALO = 16
SGU_HEADS = 4
SGU_HEAD_DIM = 64
SGU_CH = 256
CHUNK = 128
EPS = 1e-6
ROPE_BASE = 10000.0
ATTN_SCALE = (NOPE + ROPE) ** -0.5
LOG2E = math.log2(math.e)

TOK_TILE = 512
SUB_TILE = 256
Q_TILE = 4096
Q_BLOCK = 512
KV_STEP = 256
KV_UNROLL = 8
R0_KEYS = 16
STALE_MAX_HEADROOM = 64.0
V_EXT = VDIM + 16
FF_CHUNK = 1024
ADA_COLS = 1536
VMEM_LIMIT_BYTES = 56 * 1024 * 1024

ZC_Q = 0
ZC_KV = ZC_Q + Q_RANK
ZC_CONV = ZC_KV + KV_RANK
ZC_SGU = ZC_CONV + 2 * CONV_CH
ZC_KR = ZC_SGU + 2 * SGU_CH
Z_COLS = ZC_KR + QK_PAD


def _cparams(sem):
    return pltpu.CompilerParams(dimension_semantics=sem, vmem_limit_bytes=VMEM_LIMIT_BYTES)


def _sigmoid(v):
    return 1.0 / (1.0 + jnp.exp(-v))


def _layernorm(v, g, b):
    mu = jnp.mean(v, -1, keepdims=True)
    d = v - mu
    var = jnp.mean(d * d, -1, keepdims=True)
    return d * lax.rsqrt(var + EPS) * g + b


def _rms(v, g):
    return v * lax.rsqrt(jnp.mean(v * v, -1, keepdims=True) + EPS) * g


def _dot(a, b):
    return jnp.dot(a, b, preferred_element_type=F32)


def _dot_nt(a, b):
    return lax.dot_general(a, b, (((1,), (1,)), ((), ())), preferred_element_type=F32)


def _dot_tn(a, b):
    return lax.dot_general(a, b, (((0,), (0,)), ((), ())), preferred_element_type=F32)


def _ada_kernel(c_ref, w_ref, b_ref, o_ref):
    c = c_ref[...]
    s = (c * _sigmoid(c)).astype(BF16)
    o_ref[0] = _dot(s, w_ref[0].astype(BF16)) + b_ref[0]


def _ada_call(cvec, ada_w, ada_b):
    n_layers, d, n6 = ada_w.shape
    rows = cvec.shape[0]
    return pl.pallas_call(
        _ada_kernel,
        grid=(n_layers, n6 // ADA_COLS),
        in_specs=[
            pl.BlockSpec((rows, d), lambda l, j: (0, 0)),
            pl.BlockSpec((1, d, ADA_COLS), lambda l, j: (l, 0, j)),
            pl.BlockSpec((1, 1, ADA_COLS), lambda l, j: (l, 0, j)),
        ],
        out_specs=pl.BlockSpec((1, rows, ADA_COLS), lambda l, j: (l, 0, j)),
        out_shape=jax.ShapeDtypeStruct((n_layers, rows, n6), F32),
        compiler_params=_cparams(("arbitrary", "arbitrary")),
        name="adaln",
    )(cvec, ada_w, ada_b.reshape(n_layers, 1, n6))


def _pre_kernel(x_ref, xp_ref, xn_ref, mod_ref, n1g_ref, win_ref, qg_ref, wqT_ref, wqrT_ref,
                kvg_ref, wk_ref, wvT_ref, cosk_ref, sink_ref, cosq_ref, sinq_ref,
                convw_ref, convb_ref, clng_ref, clnb_ref, slng_ref, slnb_ref, sguw_ref, sgub_ref,
                qT_out, k_out, vT_out, yc_out, ys_out, ybuf, yshift, *, n_lat_tiles, n_ctx):
    i = pl.program_id(1)
    tt = x_ref.shape[1]
    subs = [(r, SUB_TILE) for r in range(0, tt, SUB_TILE)]
    sh1 = mod_ref[0, 0, 0:1, :]
    gain1 = n1g_ref[0] * (1.0 + mod_ref[0, 0, 1:2, :])

    def normmod(xt):
        return (_rms(xt, gain1) + sh1).astype(BF16)

    zs = [_dot(normmod(x_ref[0, r:r + n, :]), win_ref[0]) for r, n in subs]

    def glu(zc):
        return zc[:, :CONV_CH] * _sigmoid(zc[:, CONV_CH:])

    wconv_in = win_ref[0, :, ZC_CONV:ZC_CONV + 2 * CONV_CH]
    is_ctx = i == n_lat_tiles
    first = jnp.logical_or(i == 0, is_ctx)
    last = jnp.logical_or(i == n_lat_tiles - 1, is_ctx)
    ybuf[0:CONV_HALO, :] = jnp.where(first, 0.0, glu(_dot(normmod(xp_ref[0]), wconv_in)))
    ybuf[CONV_HALO + tt:, :] = jnp.where(last, 0.0, glu(_dot(normmod(xn_ref[0]), wconv_in)))
    for (r, n), z in zip(subs, zs):
        y = glu(z[:, ZC_CONV:ZC_CONV + 2 * CONV_CH])
        if r >= n_ctx:
            y = jnp.where(is_ctx, 0.0, y)
        ybuf[CONV_HALO + r:CONV_HALO + r + n, :] = y

    qscale = ATTN_SCALE * LOG2E
    for (r, n), z in zip(subs, zs):
        qn = _rms(z[:, ZC_Q:ZC_Q + Q_RANK], qg_ref[0]).astype(BF16)
        qT = _dot_nt(wqT_ref[0], qn)
        qrT = _dot_nt(wqrT_ref[0], qn)
        cq = cosq_ref[:, r:r + n]
        sq = sinq_ref[:, r:r + n]
        for h in range(HEADS):
            r0 = h * QK_PAD
            qT_out[0, r0:r0 + NOPE, r:r + n] = (qT[r0:r0 + NOPE, :] * qscale).astype(BF16)
            rot = qT[r0 + NOPE:r0 + NOPE + ROPE, :] * cq + qrT[h * ROPE:(h + 1) * ROPE, :] * sq
            qT_out[0, r0 + NOPE:r0 + NOPE + ROPE, r:r + n] = (rot * qscale).astype(BF16)
            qT_out[0, r0 + NOPE + ROPE:r0 + QK_PAD, r:r + n] = jnp.zeros((QK_PAD - NOPE - ROPE, n), BF16)

        kvn = _rms(z[:, ZC_KV:ZC_KV + KV_RANK], kvg_ref[0]).astype(BF16)
        kext = _dot(kvn, wk_ref[0])
        kr = z[:, ZC_KR:ZC_KR + QK_PAD]
        krope = kr * cosk_ref[r:r + n, :] + pltpu.roll(kr, QK_PAD // 2, 1) * sink_ref[r:r + n, :]
        krope = krope + (lax.broadcasted_iota(jnp.int32, (1, QK_PAD), 1) == REF_LANE).astype(F32)
        for h in range(HEADS):
            c0 = h * QK_PAD
            k_out[0, r:r + n, c0:c0 + QK_PAD] = (kext[:, c0:c0 + QK_PAD] + krope).astype(BF16)
        vT = _dot_nt(wvT_ref[0], kvn)
        for h in range(HEADS):
            vT_out[0, h * V_EXT:h * V_EXT + VDIM, r:r + n] = vT[h * VDIM:(h + 1) * VDIM, :].astype(BF16)
            vT_out[0, h * V_EXT + VDIM:(h + 1) * V_EXT, r:r + n] = jnp.ones((V_EXT - VDIM, n), BF16)

    lane = lax.broadcasted_iota(jnp.int32, (CHUNK, SGU_CH), 1)
    bias = sgub_ref[0]
    for (r, n), z in zip(subs, zs):
        zsg = z[:, ZC_SGU:ZC_SGU + 2 * SGU_CH]
        zg = 0.5 * zsg * (1.0 + jnp.tanh(math.sqrt(2.0 / math.pi) * (zsg + 0.044715 * (zsg * zsg * zsg))))
        u = zg[:, :SGU_CH]
        v = _layernorm(zg[:, SGU_CH:], slng_ref[0], slnb_ref[0]).astype(BF16)
        for c in range(n // CHUNK):
            vc = v[c * CHUNK:(c + 1) * CHUNK, :]
            mixed = _dot(sguw_ref[0, 0], vc)
            for h in range(1, SGU_HEADS):
                mixed = jnp.where(lane >= h * SGU_HEAD_DIM, _dot(sguw_ref[0, h], vc), mixed)
            ys_out[0, r + c * CHUNK:r + (c + 1) * CHUNK, :] = (
                u[c * CHUNK:(c + 1) * CHUNK, :] * (mixed + bias)).astype(BF16)

    cw = convw_ref[0]
    accs = [jnp.zeros((n, CONV_CH), F32) for _, n in subs]
    n_a = (CONV_WIDTH + 8) // 8
    for b in range(8):
        src = ybuf
        if b:
            yshift[...] = ybuf[b:b + yshift.shape[0], :]
            src = yshift
        for a in range(n_a):
            k = 8 * a + b - 1
            if 0 <= k < CONV_WIDTH:
                for j, (r, n) in enumerate(subs):
                    accs[j] = accs[j] + src[r + 8 * a:r + 8 * a + n, :] * cw[k:k + 1, :]
    for j, (r, n) in enumerate(subs):
        yc = _layernorm(accs[j] + convb_ref[0], clng_ref[0], clnb_ref[0])
        yc_out[0, r:r + n, :] = (yc * _sigmoid(yc)).astype(BF16)


def _pre_call(layer, xc, modsel, p, n_lat_tiles, n_ctx):
    bsz, ltot, d = xc.shape
    tt = TOK_TILE
    n_tiles = ltot // tt
    hb = tt // CONV_HALO
    n_hblocks = ltot // CONV_HALO

    def lay(shape):
        nd = len(shape)
        return pl.BlockSpec((1,) + shape, lambda b, i: (layer,) + (0,) * nd)

    in_specs = [
        pl.BlockSpec((1, tt, d), lambda b, i: (b, i, 0)),
        pl.BlockSpec((1, CONV_HALO, d), lambda b, i: (b, jnp.maximum(i * hb - 1, 0), 0)),
        pl.BlockSpec((1, CONV_HALO, d), lambda b, i: (b, jnp.minimum((i + 1) * hb, n_hblocks - 1), 0)),
        pl.BlockSpec((1, 1, 8, d), lambda b, i: (b, i // n_lat_tiles, 0, 0)),
        lay((1, d)),
        lay((d, Z_COLS)),
        lay((1, Q_RANK)),
        lay((HEADS * QK_PAD, Q_RANK)),
        lay((HEADS * ROPE, Q_RANK)),
        lay((1, KV_RANK)),
        lay((KV_RANK, HEADS * QK_PAD)),
        lay((HEADS * VDIM, KV_RANK)),
        pl.BlockSpec((tt, QK_PAD), lambda b, i: (i, 0)),
        pl.BlockSpec((tt, QK_PAD), lambda b, i: (i, 0)),
        pl.BlockSpec((ROPE, tt), lambda b, i: (0, i)),
        pl.BlockSpec((ROPE, tt), lambda b, i: (0, i)),
        lay((CONV_WIDTH, CONV_CH)),
        lay((1, CONV_CH)),
        lay((1, CONV_CH)),
        lay((1, CONV_CH)),
        lay((1, SGU_CH)),
        lay((1, SGU_CH)),
        lay((SGU_HEADS, CHUNK, CHUNK)),
        lay((CHUNK, SGU_CH)),
    ]
    out_specs = [
        pl.BlockSpec((1, HEADS * QK_PAD, tt), lambda b, i: (b, 0, i)),
        pl.BlockSpec((1, tt, HEADS * QK_PAD), lambda b, i: (b, i, 0)),
        pl.BlockSpec((1, HEADS * V_EXT, tt), lambda b, i: (b, 0, i)),
        pl.BlockSpec((1, tt, CONV_CH), lambda b, i: (b, i, 0)),
        pl.BlockSpec((1, tt, SGU_CH), lambda b, i: (b, i, 0)),
    ]
    out_shape = [
        jax.ShapeDtypeStruct((bsz, HEADS * QK_PAD, ltot), BF16),
        jax.ShapeDtypeStruct((bsz, ltot, HEADS * QK_PAD), BF16),
        jax.ShapeDtypeStruct((bsz, HEADS * V_EXT, ltot), BF16),
        jax.ShapeDtypeStruct((bsz, ltot, CONV_CH), BF16),
        jax.ShapeDtypeStruct((bsz, ltot, SGU_CH), BF16),
    ]
    return pl.pallas_call(
        functools.partial(_pre_kernel, n_lat_tiles=n_lat_tiles, n_ctx=n_ctx),
        grid=(bsz, n_tiles),
        in_specs=in_specs,
        out_specs=out_specs,
        out_shape=out_shape,
        scratch_shapes=[pltpu.VMEM((tt + 2 * CONV_HALO, CONV_CH), F32),
                        pltpu.VMEM((tt + 2 * CONV_HALO - 8, CONV_CH), F32)],
        compiler_params=_cparams(("arbitrary", "arbitrary")),
        name="pre",
    )(xc, xc, xc, modsel, p["norm1_g"], p["w_in"], p["q_norm_g"], p["wqT"], p["wqrT"],
      p["kv_norm_g"], p["wk"], p["wvT"], p["cosk"], p["sink"], p["cosq"], p["sinq"],
      p["conv_w"], p["conv_b"], p["conv_ln_g"], p["conv_ln_b"], p["sgu_ln_g"], p["sgu_ln_b"],
      p["sgu_w"], p["sgu_bias"])


def _softmax_update(s, vt, m, acc):
    m_new = jnp.maximum(m, jnp.max(s, axis=0, keepdims=True))
    alpha = jnp.exp2(m - m_new)
    pr = jnp.exp2(s - m_new).astype(BF16)
    return m_new, alpha * acc + _dot(vt, pr)


def _bf16_exact(v):
    return v.astype(BF16).astype(F32)


def _softmax_finish(acc):
    return (acc[0:VDIM, :] / acc[VDIM:VDIM + 1, :]).astype(BF16)


def _attn_lat_kernel(qT_ref, k_ref, vT_ref, o_ref, acc_scr, *, n_lat, n_ctx):
    tq = qT_ref.shape[2]
    n_cb = tq // Q_BLOCK
    n_steps = n_lat // KV_STEP

    def cols(c):
        return slice(c * Q_BLOCK, (c + 1) * Q_BLOCK)

    def scores(o, size, c, r=None):
        q = qT_ref[0, :, cols(c)]
        if r is not None:
            first_row = lax.broadcasted_iota(jnp.int32, (REF_ROWS, Q_BLOCK), 0) == 0
            q = jnp.concatenate([q[0:REF_LANE], jnp.where(first_row, -r, 0.0).astype(BF16),
                                 q[REF_LANE + REF_ROWS:]], axis=0)
        return _dot(k_ref[0, pl.ds(o, size), :], q)

    def run(blocks, carry):
        rs, excess = list(carry[0]), list(carry[1])
        n = len(blocks)
        s, pa = {}, {}

        def soft(i):
            c = blocks[i][2]
            si = s.pop(i)
            mx = jnp.max(si, axis=0, keepdims=True)
            r_new = _bf16_exact(rs[c] + jnp.maximum(mx, 0.0))
            pa[i] = (jnp.exp2(si).astype(BF16), jnp.exp2(rs[c] - r_new))
            excess[c] = jnp.maximum(excess[c], mx)
            rs[c] = r_new

        for i in range(min(2, n)):
            s[i] = scores(*blocks[i], rs[blocks[i][2]])
        soft(0)
        for i, (o, size, c) in enumerate(blocks):
            if i + 2 < n:
                s[i + 2] = scores(*blocks[i + 2], rs[blocks[i + 2][2]])
            if i + 1 < n:
                soft(i + 1)
            p, alpha = pa.pop(i)
            acc_scr[:, cols(c)] = (acc_scr[:, cols(c)] + _dot(vT_ref[0, :, pl.ds(o, size)], p)) * alpha
        return tuple(rs), tuple(excess)

    def latent_steps(j, carry):
        offs = [pl.multiple_of((j * KV_UNROLL + u) * KV_STEP, KV_STEP) for u in range(KV_UNROLL)]
        return run([(o, KV_STEP, c) for o in offs for c in range(n_cb)], carry)

    acc_scr[...] = jnp.zeros(acc_scr.shape, F32)
    carry = (tuple(_bf16_exact(jnp.max(scores(n_lat, R0_KEYS, c), axis=0, keepdims=True)) for c in range(n_cb)),
             tuple(jnp.zeros((1, Q_BLOCK), F32) for _ in range(n_cb)))
    carry = run([(n_lat, n_ctx, c) for c in range(n_cb)], carry)
    _, excess = lax.fori_loop(0, n_steps // KV_UNROLL, latent_steps, carry)
    o_ref[0] = _softmax_finish(acc_scr[...])

    @pl.when(jnp.max(jnp.concatenate(excess, axis=1)) > STALE_MAX_HEADROOM)
    def _():
        def exact_step(o, size, carry):
            s = _dot(k_ref[0, pl.ds(o, size), :], qT_ref[0])
            return _softmax_update(s, vT_ref[0, :, pl.ds(o, size)], *carry)

        carry = lax.fori_loop(
            0, n_steps, lambda t, cr: exact_step(pl.multiple_of(t * KV_STEP, KV_STEP), KV_STEP, cr),
            (jnp.full((1, tq), -jnp.inf, F32), jnp.zeros((V_EXT, tq), F32)))
        _, acc = exact_step(n_lat, n_ctx, carry)
        o_ref[0] = _softmax_finish(acc)


def _attn_ctx_kernel(qT_ref, k_ref, vT_ref, a_hbm_ref, o_ref):
    del a_hbm_ref
    s = _dot(k_ref[0], qT_ref[0])
    tq = s.shape[1]
    _, acc = _softmax_update(s, vT_ref[0], jnp.full((1, tq), -jnp.inf, F32), jnp.zeros((V_EXT, tq), F32))
    o_ref[0, :, 0:tq] = _softmax_finish(acc)
    if o_ref.shape[2] > tq:
        o_ref[0, :, tq:] = jnp.zeros((o_ref.shape[1], o_ref.shape[2] - tq), BF16)


def _attn_call(qT, k, vT, n_lat, n_ctx, with_ctx):
    bsz, _, ltot = qT.shape
    tq = Q_TILE
    assert n_lat % (KV_STEP * KV_UNROLL) == 0 and n_lat % tq == 0 and tq % Q_BLOCK == 0
    aT = pl.pallas_call(
        functools.partial(_attn_lat_kernel, n_lat=n_lat, n_ctx=n_ctx),
        grid=(bsz, HEADS, n_lat // tq),
        in_specs=[
            pl.BlockSpec((1, QK_PAD, tq), lambda b, h, q: (b, h, q)),
            pl.BlockSpec((1, ltot, QK_PAD), lambda b, h, q: (b, 0, h)),
            pl.BlockSpec((1, V_EXT, ltot), lambda b, h, q: (b, h, 0)),
        ],
        out_specs=pl.BlockSpec((1, VDIM, tq), lambda b, h, q: (b, h, q)),
        out_shape=jax.ShapeDtypeStruct((bsz, HEADS * VDIM, ltot), BF16),
        scratch_shapes=[pltpu.VMEM((V_EXT, tq), F32)],
        compiler_params=_cparams(("arbitrary", "arbitrary", "arbitrary")),
        name="attn_lat",
    )(qT, k, vT)
    if not with_ctx:
        return aT
    cb = n_lat // n_ctx
    return pl.pallas_call(
        _attn_ctx_kernel,
        grid=(bsz, HEADS),
        in_specs=[
            pl.BlockSpec((1, QK_PAD, n_ctx), lambda b, h: (b, h, cb)),
            pl.BlockSpec((1, n_ctx, QK_PAD), lambda b, h: (b, cb, h)),
            pl.BlockSpec((1, V_EXT, n_ctx), lambda b, h: (b, h, cb)),
            pl.BlockSpec(memory_space=pl.ANY),
        ],
        out_specs=pl.BlockSpec((1, VDIM, TOK_TILE), lambda b, h: (b, h, n_lat // TOK_TILE)),
        out_shape=jax.ShapeDtypeStruct(aT.shape, aT.dtype),
        input_output_aliases={3: 0},
        compiler_params=_cparams(("arbitrary", "arbitrary")),
        name="attn_ctx",
    )(qT, k, vT, aT)


def _post_kernel(x_ref, aT_ref, yc_ref, ys_ref, mod_ref, n2g_ref, wout_ref, w1_ref, w2_ref, fg_ref,
                 o_ref, *, final):
    tt, d = x_ref.shape[1], x_ref.shape[2]
    d_ff = w1_ref.shape[2]
    n_attn = aT_ref.shape[1]
    subs = [(r, SUB_TILE) for r in range(0, tt, SUB_TILE)]
    g1 = mod_ref[0, 0, 2:3, :]
    sh2 = mod_ref[0, 0, 3:4, :]
    gain2 = n2g_ref[0] * (1.0 + mod_ref[0, 0, 4:5, :])
    g2 = mod_ref[0, 0, 5:6, :]
    x1s, hs = [], []
    for r, n in subs:
        y = _dot_tn(aT_ref[0, :, r:r + n], wout_ref[0, 0:n_attn, :])
        y = y + _dot(yc_ref[0, r:r + n, :], wout_ref[0, n_attn:n_attn + CONV_CH, :])
        y = y + _dot(ys_ref[0, r:r + n, :], wout_ref[0, n_attn + CONV_CH:d, :])
        x1 = x_ref[0, r:r + n, :] + g1 * y
        x1s.append(x1)
        hs.append((_rms(x1, gain2) + sh2).astype(BF16))
    for (r, n), x1, h in zip(subs, x1s, hs):
        ff = jnp.zeros(x1.shape, F32)
        for c in range(d_ff // FF_CHUNK):
            f = jnp.maximum(_dot(h, w1_ref[0, :, c * FF_CHUNK:(c + 1) * FF_CHUNK]), 0.0)
            ff = ff + _dot((f * f).astype(BF16), w2_ref[0, c * FF_CHUNK:(c + 1) * FF_CHUNK, :])
        x2 = x1 + g2 * ff
        if final:
            x2 = _rms(x2, fg_ref[...])
        o_ref[0, r:r + n, :] = x2


def _post_call(layer, xc, aT, yc, ys, modsel, p, n_lat_tiles, n_out_tiles, final):
    bsz, _, d = xc.shape
    tt = TOK_TILE
    d_ff = p["w_ff1"].shape[2]

    def lay(shape):
        nd = len(shape)
        return pl.BlockSpec((1,) + shape, lambda b, i: (layer,) + (0,) * nd,
                            pipeline_mode=pl.Buffered(1))

    return pl.pallas_call(
        functools.partial(_post_kernel, final=final),
        grid=(bsz, n_out_tiles),
        in_specs=[
            pl.BlockSpec((1, tt, d), lambda b, i: (b, i, 0)),
            pl.BlockSpec((1, HEADS * VDIM, tt), lambda b, i: (b, 0, i)),
            pl.BlockSpec((1, tt, CONV_CH), lambda b, i: (b, i, 0)),
            pl.BlockSpec((1, tt, SGU_CH), lambda b, i: (b, i, 0)),
            pl.BlockSpec((1, 1, 8, d), lambda b, i: (b, i // n_lat_tiles, 0, 0)),
            pl.BlockSpec((1, 1, d), lambda b, i: (layer, 0, 0)),
            lay((d, d)),
            lay((d, d_ff)),
            lay((d_ff, d)),
            pl.BlockSpec((1, d), lambda b, i: (0, 0)),
        ],
        out_specs=pl.BlockSpec((1, tt, d), lambda b, i: (b, i, 0)),
        out_shape=jax.ShapeDtypeStruct((bsz, n_out_tiles * tt, d), F32),
        compiler_params=_cparams(("arbitrary", "arbitrary")),
        name="post",
    )(xc, aT, yc, ys, modsel, p["norm2_g"], p["w_out"], p["w_ff1"], p["w_ff2"], p["final_g"])


def _rotate_half_cols(w):
    n = ROPE // 4
    return jnp.concatenate([-w[..., n:2 * n], w[..., 0:n], -w[..., 3 * n:4 * n], w[..., 2 * n:3 * n]], -1)


def _prepare(n_lat, n_tail, norm1_g, norm2_g, w_in, q_norm_g, w_uq, kv_norm_g, w_ukv, conv_w, conv_b,
             conv_ln_g, conv_ln_b, sgu_ln_g, sgu_ln_b, sgu_w, sgu_b, w_out, w_ff1, w_ff2, final_g):
    n_layers, d, _ = w_in.shape
    o_q, o_kv, o_kr = 0, Q_RANK, Q_RANK + KV_RANK
    o_conv = o_kr + ROPE
    o_sgu = o_conv + 2 * CONV_CH
    w_kr = w_in[:, :, o_kr:o_kr + ROPE]

    gap = jnp.zeros((n_layers, d, QK_PAD // 2 - ROPE), w_in.dtype)
    kr_slab = jnp.concatenate([_rotate_half_cols(w_kr), gap, w_kr, gap], -1)
    w_in_x = jnp.concatenate([
        w_in[:, :, o_q:o_q + Q_RANK], w_in[:, :, o_kv:o_kv + KV_RANK],
        w_in[:, :, o_conv:o_conv + 2 * CONV_CH], w_in[:, :, o_sgu:o_sgu + 2 * SGU_CH],
        kr_slab], -1).astype(BF16)

    wq = w_uq.reshape(n_layers, Q_RANK, HEADS, NOPE + ROPE)
    wq_pad = jnp.pad(wq, ((0, 0), (0, 0), (0, 0), (0, QK_PAD - NOPE - ROPE)))
    wqT = wq_pad.reshape(n_layers, Q_RANK, HEADS * QK_PAD).transpose(0, 2, 1).astype(BF16)
    wqr = _rotate_half_cols(wq[..., NOPE:])
    wqrT = wqr.reshape(n_layers, Q_RANK, HEADS * ROPE).transpose(0, 2, 1).astype(BF16)

    wkv = w_ukv.reshape(n_layers, KV_RANK, HEADS, NOPE + VDIM)
    wk = jnp.pad(wkv[..., :NOPE], ((0, 0), (0, 0), (0, 0), (0, QK_PAD - NOPE)))
    wk = wk.reshape(n_layers, KV_RANK, HEADS * QK_PAD).astype(BF16)
    wvT = wkv[..., NOPE:].reshape(n_layers, KV_RANK, HEADS * VDIM).transpose(0, 2, 1).astype(BF16)

    t = jnp.arange(n_lat, dtype=jnp.int32)
    n = ROPE // 4
    inv = 1.0 / (ROPE_BASE ** (jnp.arange(n, dtype=F32) / n))
    ang_r = (t // GRID_W).astype(F32)[:, None] * inv
    ang_c = (t % GRID_W).astype(F32)[:, None] * inv
    cos32 = jnp.concatenate([jnp.cos(ang_r)] * 2 + [jnp.cos(ang_c)] * 2, -1)
    sin32 = jnp.concatenate([jnp.sin(ang_r)] * 2 + [jnp.sin(ang_c)] * 2, -1)
    cos32 = jnp.concatenate([cos32, jnp.ones((n_tail, ROPE), F32)], 0)
    sin32 = jnp.concatenate([sin32, jnp.zeros((n_tail, ROPE), F32)], 0)
    pad = ((0, 0), (NOPE, QK_PAD - NOPE - ROPE))

    sgu_bias = jnp.broadcast_to(jnp.swapaxes(sgu_b, 1, 2)[:, :, :, None],
                                (n_layers, CHUNK, SGU_HEADS, SGU_HEAD_DIM)).reshape(n_layers, CHUNK, SGU_CH)

    def row(a):
        return a.reshape(n_layers, 1, a.shape[-1])

    return dict(
        norm1_g=row(norm1_g), norm2_g=row(norm2_g), w_in=w_in_x, q_norm_g=row(q_norm_g), wqT=wqT, wqrT=wqrT,
        kv_norm_g=row(kv_norm_g), wk=wk, wvT=wvT,
        cosk=jnp.pad(cos32, pad), sink=jnp.pad(sin32, pad), cosq=cos32.T, sinq=sin32.T,
        conv_w=conv_w, conv_b=row(conv_b), conv_ln_g=row(conv_ln_g), conv_ln_b=row(conv_ln_b),
        sgu_ln_g=row(sgu_ln_g), sgu_ln_b=row(sgu_ln_b), sgu_w=sgu_w.astype(BF16), sgu_bias=sgu_bias,
        w_out=w_out.astype(BF16), w_ff1=w_ff1.astype(BF16), w_ff2=w_ff2.astype(BF16),
        final_g=final_g.reshape(1, -1))


def kernel(x, c, ctx, c_ctx, ada_w, ada_b, norm1_g, norm2_g, w_in, q_norm_g, w_uq, kv_norm_g, w_ukv,
           conv_w, conv_b, conv_ln_g, conv_ln_b, sgu_ln_g, sgu_ln_b, sgu_w, sgu_b, w_out, w_ff1, w_ff2,
           final_g):
    bsz, n_lat, d = x.shape
    n_ctx = ctx.shape[1]
    n_layers = w_in.shape[0]
    assert n_lat % TOK_TILE == 0 and n_lat % GRID_W == 0
    assert n_ctx <= TOK_TILE and n_ctx % SUB_TILE == 0
    assert n_lat % n_ctx == 0
    n_lat_tiles = n_lat // TOK_TILE
    n_tiles = n_lat_tiles + 1

    p = _prepare(n_lat, TOK_TILE, norm1_g, norm2_g, w_in, q_norm_g, w_uq, kv_norm_g, w_ukv, conv_w, conv_b,
                 conv_ln_g, conv_ln_b, sgu_ln_g, sgu_ln_b, sgu_w, sgu_b, w_out, w_ff1, w_ff2, final_g)

    rows = -(-(bsz + 1) // 8) * 8
    cvec = jnp.concatenate([c, c_ctx[None, :], jnp.zeros((rows - bsz - 1, d), F32)], 0)
    mods = _ada_call(cvec, ada_w, ada_b).reshape(n_layers, rows, 6, d)
    m_lat = mods[:, :bsz]
    m_ctx = jnp.broadcast_to(mods[:, bsz][:, None], m_lat.shape)
    modsel = jnp.pad(jnp.stack([m_lat, m_ctx], 2), ((0, 0), (0, 0), (0, 0), (0, 2), (0, 0)))

    xc = jnp.concatenate([x, ctx, jnp.zeros((bsz, TOK_TILE - n_ctx, d), x.dtype)], 1)
    for layer in range(n_layers):
        last = layer == n_layers - 1
        n_act = n_lat_tiles if last else n_tiles
        qT, k, vT, yc, ys = _pre_call(layer, xc, modsel[layer], p, n_lat_tiles, n_ctx)
        aT = _attn_call(qT, k, vT, n_lat, n_ctx, not last)
        xc = _post_call(layer, xc, aT, yc, ys, modsel[layer], p, n_lat_tiles, n_act, last)
    return xc
```

```python
import functools
import math

import jax
import jax.numpy as jnp
from jax import lax
from jax.experimental import pallas as pl
from jax.experimental.pallas import tpu as pltpu

F32 = jnp.float32
BF16 = jnp.bfloat16

GRID_W = 64
HEADS = 8
NOPE = 64
ROPE = 32
VDIM = 64
QK_PAD = 128
REF_LANE = NOPE + ROPE
REF_ROWS = 16
Q_RANK = 384
KV_RANK = 256
CONV_CH = 256
CONV_WIDTH = 31
CONV_HALO = 16
SGU_HEADS = 4
SGU_HEAD_DIM = 64
SGU_CH = 256
CHUNK = 128
EPS = 1e-6
ROPE_BASE = 10000.0
ATTN_SCALE = (NOPE + ROPE) ** -0.5
LOG2E = math.log2(math.e)

TOK_TILE = 512
SUB_TILE = 256
Q_TILE = 4096
Q_BLOCK = 512
KV_STEP = 256
KV_UNROLL = 16
R0_KEYS = 16
STALE_MAX_HEADROOM = 64.0
V_EXT = VDIM + 16
FF_CHUNK = 1024
ADA_COLS = 1536
VMEM_LIMIT_BYTES = 56 * 1024 * 1024

ZC_Q = 0
ZC_KV = ZC_Q + Q_RANK
ZC_CONV = ZC_KV + KV_RANK
ZC_SGU = ZC_CONV + 2 * CONV_CH
ZC_KR = ZC_SGU + 2 * SGU_CH
Z_COLS = ZC_KR + QK_PAD


def _cparams(sem):
    return pltpu.CompilerParams(dimension_semantics=sem, vmem_limit_bytes=VMEM_LIMIT_BYTES)


def _sigmoid(v):
    return 1.0 / (1.0 + jnp.exp(-v))


def _layernorm(v, g, b):
    mu = jnp.mean(v, -1, keepdims=True)
    d = v - mu
    var = jnp.mean(d * d, -1, keepdims=True)
    return d * lax.rsqrt(var + EPS) * g + b


def _rms(v, g):
    return v * lax.rsqrt(jnp.mean(v * v, -1, keepdims=True) + EPS) * g


def _dot(a, b):
    return jnp.dot(a, b, preferred_element_type=F32)


def _dot_nt(a, b):
    return lax.dot_general(a, b, (((1,), (1,)), ((), ())), preferred_element_type=F32)


def _dot_tn(a, b):
    return lax.dot_general(a, b, (((0,), (0,)), ((), ())), preferred_element_type=F32)


def _ada_kernel(c_ref, w_ref, b_ref, o_ref):
    c = c_ref[...]
    s = (c * _sigmoid(c)).astype(BF16)
    o_ref[0] = _dot(s, w_ref[0].astype(BF16)) + b_ref[0]


def _ada_call(cvec, ada_w, ada_b):
    n_layers, d, n6 = ada_w.shape
    rows = cvec.shape[0]
    return pl.pallas_call(
        _ada_kernel,
        grid=(n_layers, n6 // ADA_COLS),
        in_specs=[
            pl.BlockSpec((rows, d), lambda l, j: (0, 0)),
            pl.BlockSpec((1, d, ADA_COLS), lambda l, j: (l, 0, j)),
            pl.BlockSpec((1, 1, ADA_COLS), lambda l, j: (l, 0, j)),
        ],
        out_specs=pl.BlockSpec((1, rows, ADA_COLS), lambda l, j: (l, 0, j)),
        out_shape=jax.ShapeDtypeStruct((n_layers, rows, n6), F32),
        compiler_params=_cparams(("arbitrary", "arbitrary")),
        name="adaln",
    )(cvec, ada_w, ada_b.reshape(n_layers, 1, n6))


def _pre_kernel(x_ref, xp_ref, xn_ref, mod_ref, n1g_ref, win_ref, qg_ref, wqT_ref, wqrT_ref,
                kvg_ref, wk_ref, wvT_ref, cosk_ref, sink_ref, cosq_ref, sinq_ref,
                convw_ref, convb_ref, clng_ref, clnb_ref, slng_ref, slnb_ref, sguw_ref, sgub_ref,
                qT_out, k_out, vT_out, yc_out, ys_out, ybuf, yshift, *, n_lat_tiles, n_ctx):
    i = pl.program_id(1)
    tt = x_ref.shape[1]
    subs = [(r, SUB_TILE) for r in range(0, tt, SUB_TILE)]
    sh1 = mod_ref[0, 0, 0:1, :]
    gain1 = n1g_ref[0] * (1.0 + mod_ref[0, 0, 1:2, :])

    def normmod(xt):
        return (_rms(xt, gain1) + sh1).astype(BF16)

    zs = [_dot(normmod(x_ref[0, r:r + n, :]), win_ref[0]) for r, n in subs]

    def glu(zc):
        return zc[:, :CONV_CH] * _sigmoid(zc[:, CONV_CH:])

    wconv_in = win_ref[0, :, ZC_CONV:ZC_CONV + 2 * CONV_CH]
    is_ctx = i == n_lat_tiles
    first = jnp.logical_or(i == 0, is_ctx)
    last = jnp.logical_or(i == n_lat_tiles - 1, is_ctx)
    ybuf[0:CONV_HALO, :] = jnp.where(first, 0.0, glu(_dot(normmod(xp_ref[0]), wconv_in)))
    ybuf[CONV_HALO + tt:, :] = jnp.where(last, 0.0, glu(_dot(normmod(xn_ref[0]), wconv_in)))
    for (r, n), z in zip(subs, zs):
        y = glu(z[:, ZC_CONV:ZC_CONV + 2 * CONV_CH])
        if r >= n_ctx:
            y = jnp.where(is_ctx, 0.0, y)
        ybuf[CONV_HALO + r:CONV_HALO + r + n, :] = y

    qscale = ATTN_SCALE * LOG2E
    for (r, n), z in zip(subs, zs):
        qn = _rms(z[:, ZC_Q:ZC_Q + Q_RANK], qg_ref[0]).astype(BF16)
        qT = _dot_nt(wqT_ref[0], qn)
        qrT = _dot_nt(wqrT_ref[0], qn)
        cq = cosq_ref[:, r:r + n]
        sq = sinq_ref[:, r:r + n]
        for h in range(HEADS):
            r0 = h * QK_PAD
            qT_out[0, r0:r0 + NOPE, r:r + n] = (qT[r0:r0 + NOPE, :] * qscale).astype(BF16)
            rot = qT[r0 + NOPE:r0 + NOPE + ROPE, :] * cq + qrT[h * ROPE:(h + 1) * ROPE, :] * sq
            qT_out[0, r0 + NOPE:r0 + NOPE + ROPE, r:r + n] = (rot * qscale).astype(BF16)
            qT_out[0, r0 + NOPE + ROPE:r0 + QK_PAD, r:r + n] = jnp.zeros((QK_PAD - NOPE - ROPE, n), BF16)

        kvn = _rms(z[:, ZC_KV:ZC_KV + KV_RANK], kvg_ref[0]).astype(BF16)
        kext = _dot(kvn, wk_ref[0])
        kr = z[:, ZC_KR:ZC_KR + QK_PAD]
        krope = kr * cosk_ref[r:r + n, :] + pltpu.roll(kr, QK_PAD // 2, 1) * sink_ref[r:r + n, :]
        krope = krope + (lax.broadcasted_iota(jnp.int32, (1, QK_PAD), 1) == REF_LANE).astype(F32)
        for h in range(HEADS):
            c0 = h * QK_PAD
            k_out[0, r:r + n, c0:c0 + QK_PAD] = (kext[:, c0:c0 + QK_PAD] + krope).astype(BF16)
        vT = _dot_nt(wvT_ref[0], kvn)
        for h in range(HEADS):
            vT_out[0, h * V_EXT:h * V_EXT + VDIM, r:r + n] = vT[h * VDIM:(h + 1) * VDIM, :].astype(BF16)
            vT_out[0, h * V_EXT + VDIM:(h + 1) * V_EXT, r:r + n] = jnp.ones((V_EXT - VDIM, n), BF16)

    lane = lax.broadcasted_iota(jnp.int32, (CHUNK, SGU_CH), 1)
    bias = sgub_ref[0]
    for (r, n), z in zip(subs, zs):
        zsg = z[:, ZC_SGU:ZC_SGU + 2 * SGU_CH]
        zg = 0.5 * zsg * (1.0 + jnp.tanh(math.sqrt(2.0 / math.pi) * (zsg + 0.044715 * (zsg * zsg * zsg))))
        u = zg[:, :SGU_CH]
        v = _layernorm(zg[:, SGU_CH:], slng_ref[0], slnb_ref[0]).astype(BF16)
        for c in range(n // CHUNK):
            vc = v[c * CHUNK:(c + 1) * CHUNK, :]
            mixed = _dot(sguw_ref[0, 0], vc)
            for h in range(1, SGU_HEADS):
                mixed = jnp.where(lane >= h * SGU_HEAD_DIM, _dot(sguw_ref[0, h], vc), mixed)
            ys_out[0, r + c * CHUNK:r + (c + 1) * CHUNK, :] = (
                u[c * CHUNK:(c + 1) * CHUNK, :] * (mixed + bias)).astype(BF16)

    cw = convw_ref[0]
    accs = [jnp.zeros((n, CONV_CH), F32) for _, n in subs]
    n_a = (CONV_WIDTH + 8) // 8
    for b in range(8):
        src = ybuf
        if b:
            yshift[...] = ybuf[b:b + yshift.shape[0], :]
            src = yshift
        for a in range(n_a):
            k = 8 * a + b - 1
            if 0 <= k < CONV_WIDTH:
                for j, (r, n) in enumerate(subs):
                    accs[j] = accs[j] + src[r + 8 * a:r + 8 * a + n, :] * cw[k:k + 1, :]
    for j, (r, n) in enumerate(subs):
        yc = _layernorm(accs[j] + convb_ref[0], clng_ref[0], clnb_ref[0])
        yc_out[0, r:r + n, :] = (yc * _sigmoid(yc)).astype(BF16)


def _pre_call(layer, xc, modsel, p, n_lat_tiles, n_ctx):
    bsz, ltot, d = xc.shape
    tt = TOK_TILE
    n_tiles = ltot // tt
    hb = tt // CONV_HALO
    n_hblocks = ltot // CONV_HALO

    def lay(shape):
        nd = len(shape)
        return pl.BlockSpec((1,) + shape, lambda b, i: (layer,) + (0,) * nd)

    in_specs = [
        pl.BlockSpec((1, tt, d), lambda b, i: (b, i, 0)),
        pl.BlockSpec((1, CONV_HALO, d), lambda b, i: (b, jnp.maximum(i * hb - 1, 0), 0)),
        pl.BlockSpec((1, CONV_HALO, d), lambda b, i: (b, jnp.minimum((i + 1) * hb, n_hblocks - 1), 0)),
        pl.BlockSpec((1, 1, 8, d), lambda b, i: (b, i // n_lat_tiles, 0, 0)),
        lay((1, d)),
        lay((d, Z_COLS)),
        lay((1, Q_RANK)),
        lay((HEADS * QK_PAD, Q_RANK)),
        lay((HEADS * ROPE, Q_RANK)),
        lay((1, KV_RANK)),
        lay((KV_RANK, HEADS * QK_PAD)),
        lay((HEADS * VDIM, KV_RANK)),
        pl.BlockSpec((tt, QK_PAD), lambda b, i: (i, 0)),
        pl.BlockSpec((tt, QK_PAD), lambda b, i: (i, 0)),
        pl.BlockSpec((ROPE, tt), lambda b, i: (0, i)),
        pl.BlockSpec((ROPE, tt), lambda b, i: (0, i)),
        lay((CONV_WIDTH, CONV_CH)),
        lay((1, CONV_CH)),
        lay((1, CONV_CH)),
        lay((1, CONV_CH)),
        lay((1, SGU_CH)),
        lay((1, SGU_CH)),
        lay((SGU_HEADS, CHUNK, CHUNK)),
        lay((CHUNK, SGU_CH)),
    ]
    out_specs = [
        pl.BlockSpec((1, HEADS * QK_PAD, tt), lambda b, i: (b, 0, i)),
        pl.BlockSpec((1, tt, HEADS * QK_PAD), lambda b, i: (b, i, 0)),
        pl.BlockSpec((1, HEADS * V_EXT, tt), lambda b, i: (b, 0, i)),
        pl.BlockSpec((1, tt, CONV_CH), lambda b, i: (b, i, 0)),
        pl.BlockSpec((1, tt, SGU_CH), lambda b, i: (b, i, 0)),
    ]
    out_shape = [
        jax.ShapeDtypeStruct((bsz, HEADS * QK_PAD, ltot), BF16),
        jax.ShapeDtypeStruct((bsz, ltot, HEADS * QK_PAD), BF16),
        jax.ShapeDtypeStruct((bsz, HEADS * V_EXT, ltot), BF16),
        jax.ShapeDtypeStruct((bsz, ltot, CONV_CH), BF16),
        jax.ShapeDtypeStruct((bsz, ltot, SGU_CH), BF16),
    ]
    return pl.pallas_call(
        functools.partial(_pre_kernel, n_lat_tiles=n_lat_tiles, n_ctx=n_ctx),
        grid=(bsz, n_tiles),
        in_specs=in_specs,
        out_specs=out_specs,
        out_shape=out_shape,
        scratch_shapes=[pltpu.VMEM((tt + 2 * CONV_HALO, CONV_CH), F32),
                        pltpu.VMEM((tt + 2 * CONV_HALO - 8, CONV_CH), F32)],
        compiler_params=_cparams(("arbitrary", "arbitrary")),
        name="pre",
    )(xc, xc, xc, modsel, p["norm1_g"], p["w_in"], p["q_norm_g"], p["wqT"], p["wqrT"],
      p["kv_norm_g"], p["wk"], p["wvT"], p["cosk"], p["sink"], p["cosq"], p["sinq"],
      p["conv_w"], p["conv_b"], p["conv_ln_g"], p["conv_ln_b"], p["sgu_ln_g"], p["sgu_ln_b"],
      p["sgu_w"], p["sgu_bias"])


def _softmax_update(s, vt, m, acc):
    m_new = jnp.maximum(m, jnp.max(s, axis=0, keepdims=True))
    alpha = jnp.exp2(m - m_new)
    pr = jnp.exp2(s - m_new).astype(BF16)
    return m_new, alpha * acc + _dot(vt, pr)


def _bf16_exact(v):
    return v.astype(BF16).astype(F32)


def _softmax_finish(acc):
    return (acc[0:VDIM, :] / acc[VDIM:VDIM + 1, :]).astype(BF16)


def _attn_lat_kernel(qT_ref, k_ref, vT_ref, o_ref, acc_scr, *, n_lat, n_ctx):
    tq = qT_ref.shape[2]
    n_cb = tq // Q_BLOCK
    n_steps = n_lat // KV_STEP

    def cols(c):
        return slice(c * Q_BLOCK, (c + 1) * Q_BLOCK)

    def scores(o, size, c, r=None):
        q = qT_ref[0, :, cols(c)]
        if r is not None:
            first_row = lax.broadcasted_iota(jnp.int32, (REF_ROWS, Q_BLOCK), 0) == 0
            q = jnp.concatenate([q[0:REF_LANE], jnp.where(first_row, -r, 0.0).astype(BF16),
                                 q[REF_LANE + REF_ROWS:]], axis=0)
        return _dot(k_ref[0, pl.ds(o, size), :], q)

    def run(blocks, carry):
        rs, excess = list(carry[0]), list(carry[1])
        n = len(blocks)
        s, pa = {}, {}

        def soft(i):
            c = blocks[i][2]
            si = s.pop(i)
            mx = jnp.max(si, axis=0, keepdims=True)
            r_new = _bf16_exact(rs[c] + jnp.maximum(mx, 0.0))
            pa[i] = (jnp.exp2(si).astype(BF16), jnp.exp2(rs[c] - r_new))
            excess[c] = jnp.maximum(excess[c], mx)
            rs[c] = r_new

        for i in range(min(2, n)):
            s[i] = scores(*blocks[i], rs[blocks[i][2]])
        soft(0)
        for i, (o, size, c) in enumerate(blocks):
            if i + 2 < n:
                s[i + 2] = scores(*blocks[i + 2], rs[blocks[i + 2][2]])
            if i + 1 < n:
                soft(i + 1)
            p, alpha = pa.pop(i)
            acc_scr[:, cols(c)] = (acc_scr[:, cols(c)] + _dot(vT_ref[0, :, pl.ds(o, size)], p)) * alpha
        return tuple(rs), tuple(excess)

    def latent_steps(j, carry):
        offs = [pl.multiple_of((j * KV_UNROLL + u) * KV_STEP, KV_STEP) for u in range(KV_UNROLL)]
        return run([(o, KV_STEP, c) for o in offs for c in range(n_cb)], carry)

    acc_scr[...] = jnp.zeros(acc_scr.shape, F32)
    carry = (tuple(_bf16_exact(jnp.max(scores(n_lat, R0_KEYS, c), axis=0, keepdims=True)) for c in range(n_cb)),
             tuple(jnp.zeros((1, Q_BLOCK), F32) for _ in range(n_cb)))
    carry = run([(n_lat, n_ctx, c) for c in range(n_cb)], carry)
    _, excess = lax.fori_loop(0, n_steps // KV_UNROLL, latent_steps, carry)
    o_ref[0] = _softmax_finish(acc_scr[...])

    @pl.when(jnp.max(jnp.concatenate(excess, axis=1)) > STALE_MAX_HEADROOM)
    def _():
        def exact_step(o, size, carry):
            s = _dot(k_ref[0, pl.ds(o, size), :], qT_ref[0])
            return _softmax_update(s, vT_ref[0, :, pl.ds(o, size)], *carry)

        carry = lax.fori_loop(
            0, n_steps, lambda t, cr: exact_step(pl.multiple_of(t * KV_STEP, KV_STEP), KV_STEP, cr),
            (jnp.full((1, tq), -jnp.inf, F32), jnp.zeros((V_EXT, tq), F32)))
        _, acc = exact_step(n_lat, n_ctx, carry)
        o_ref[0] = _softmax_finish(acc)


def _attn_ctx_kernel(qT_ref, k_ref, vT_ref, a_hbm_ref, o_ref):
    del a_hbm_ref
    s = _dot(k_ref[0], qT_ref[0])
    tq = s.shape[1]
    _, acc = _softmax_update(s, vT_ref[0], jnp.full((1, tq), -jnp.inf, F32), jnp.zeros((V_EXT, tq), F32))
    o_ref[0, :, 0:tq] = _softmax_finish(acc)
    if o_ref.shape[2] > tq:
        o_ref[0, :, tq:] = jnp.zeros((o_ref.shape[1], o_ref.shape[2] - tq), BF16)


def _attn_call(qT, k, vT, n_lat, n_ctx, with_ctx):
    bsz, _, ltot = qT.shape
    tq = Q_TILE
    assert n_lat % (KV_STEP * KV_UNROLL) == 0 and n_lat % tq == 0 and tq % Q_BLOCK == 0
    aT = pl.pallas_call(
        functools.partial(_attn_lat_kernel, n_lat=n_lat, n_ctx=n_ctx),
        grid=(bsz, HEADS, n_lat // tq),
        in_specs=[
            pl.BlockSpec((1, QK_PAD, tq), lambda b, h, q: (b, h, q)),
            pl.BlockSpec((1, ltot, QK_PAD), lambda b, h, q: (b, 0, h)),
            pl.BlockSpec((1, V_EXT, ltot), lambda b, h, q: (b, h, 0)),
        ],
        out_specs=pl.BlockSpec((1, VDIM, tq), lambda b, h, q: (b, h, q)),
        out_shape=jax.ShapeDtypeStruct((bsz, HEADS * VDIM, ltot), BF16),
        scratch_shapes=[pltpu.VMEM((V_EXT, tq), F32)],
        compiler_params=_cparams(("arbitrary", "arbitrary", "arbitrary")),
        name="attn_lat",
    )(qT, k, vT)
    if not with_ctx:
        return aT
    cb = n_lat // n_ctx
    return pl.pallas_call(
        _attn_ctx_kernel,
        grid=(bsz, HEADS),
        in_specs=[
            pl.BlockSpec((1, QK_PAD, n_ctx), lambda b, h: (b, h, cb)),
            pl.BlockSpec((1, n_ctx, QK_PAD), lambda b, h: (b, cb, h)),
            pl.BlockSpec((1, V_EXT, n_ctx), lambda b, h: (b, h, cb)),
            pl.BlockSpec(memory_space=pl.ANY),
        ],
        out_specs=pl.BlockSpec((1, VDIM, TOK_TILE), lambda b, h: (b, h, n_lat // TOK_TILE)),
        out_shape=jax.ShapeDtypeStruct(aT.shape, aT.dtype),
        input_output_aliases={3: 0},
        compiler_params=_cparams(("arbitrary", "arbitrary")),
        name="attn_ctx",
    )(qT, k, vT, aT)


def _post_kernel(x_ref, aT_ref, yc_ref, ys_ref, mod_ref, n2g_ref, wout_ref, w1_ref, w2_ref, fg_ref,
                 o_ref, *, final):
    tt, d = x_ref.shape[1], x_ref.shape[2]
    d_ff = w1_ref.shape[2]
    n_attn = aT_ref.shape[1]
    subs = [(r, SUB_TILE) for r in range(0, tt, SUB_TILE)]
    g1 = mod_ref[0, 0, 2:3, :]
    sh2 = mod_ref[0, 0, 3:4, :]
    gain2 = n2g_ref[0] * (1.0 + mod_ref[0, 0, 4:5, :])
    g2 = mod_ref[0, 0, 5:6, :]
    x1s, hs = [], []
    for r, n in subs:
        y = _dot_tn(aT_ref[0, :, r:r + n], wout_ref[0, 0:n_attn, :])
        y = y + _dot(yc_ref[0, r:r + n, :], wout_ref[0, n_attn:n_attn + CONV_CH, :])
        y = y + _dot(ys_ref[0, r:r + n, :], wout_ref[0, n_attn + CONV_CH:d, :])
        x1 = x_ref[0, r:r + n, :] + g1 * y
        x1s.append(x1)
        hs.append((_rms(x1, gain2) + sh2).astype(BF16))
    for (r, n), x1, h in zip(subs, x1s, hs):
        ff = jnp.zeros(x1.shape, F32)
        for c in range(d_ff // FF_CHUNK):
            f = jnp.maximum(_dot(h, w1_ref[0, :, c * FF_CHUNK:(c + 1) * FF_CHUNK]), 0.0)
            ff = ff + _dot((f * f).astype(BF16), w2_ref[0, c * FF_CHUNK:(c + 1) * FF_CHUNK, :])
        x2 = x1 + g2 * ff
        if final:
            x2 = _rms(x2, fg_ref[...])
        o_ref[0, r:r + n, :] = x2


def _post_call(layer, xc, aT, yc, ys, modsel, p, n_lat_tiles, n_out_tiles, final):
    bsz, _, d = xc.shape
    tt = TOK_TILE
    d_ff = p["w_ff1"].shape[2]

    def lay(shape):
        nd = len(shape)
        return pl.BlockSpec((1,) + shape, lambda b, i: (layer,) + (0,) * nd,
                            pipeline_mode=pl.Buffered(1))

    return pl.pallas_call(
        functools.partial(_post_kernel, final=final),
        grid=(bsz, n_out_tiles),
        in_specs=[
            pl.BlockSpec((1, tt, d), lambda b, i: (b, i, 0)),
            pl.BlockSpec((1, HEADS * VDIM, tt), lambda b, i: (b, 0, i)),
            pl.BlockSpec((1, tt, CONV_CH), lambda b, i: (b, i, 0)),
            pl.BlockSpec((1, tt, SGU_CH), lambda b, i: (b, i, 0)),
            pl.BlockSpec((1, 1, 8, d), lambda b, i: (b, i // n_lat_tiles, 0, 0)),
            pl.BlockSpec((1, 1, d), lambda b, i: (layer, 0, 0)),
            lay((d, d)),
            lay((d, d_ff)),
            lay((d_ff, d)),
            pl.BlockSpec((1, d), lambda b, i: (0, 0)),
        ],
        out_specs=pl.BlockSpec((1, tt, d), lambda b, i: (b, i, 0)),
        out_shape=jax.ShapeDtypeStruct((bsz, n_out_tiles * tt, d), F32),
        compiler_params=_cparams(("arbitrary", "arbitrary")),
        name="post",
    )(xc, aT, yc, ys, modsel, p["norm2_g"], p["w_out"], p["w_ff1"], p["w_ff2"], p["final_g"])


def _rotate_half_cols(w):
    n = ROPE // 4
    return jnp.concatenate([-w[..., n:2 * n], w[..., 0:n], -w[..., 3 * n:4 * n], w[..., 2 * n:3 * n]], -1)


def _prepare(n_lat, n_tail, norm1_g, norm2_g, w_in, q_norm_g, w_uq, kv_norm_g, w_ukv, conv_w, conv_b,
             conv_ln_g, conv_ln_b, sgu_ln_g, sgu_ln_b, sgu_w, sgu_b, w_out, w_ff1, w_ff2, final_g):
    n_layers, d, _ = w_in.shape
    o_q, o_kv, o_kr = 0, Q_RANK, Q_RANK + KV_RANK
    o_conv = o_kr + ROPE
    o_sgu = o_conv + 2 * CONV_CH
    w_kr = w_in[:, :, o_kr:o_kr + ROPE]

    gap = jnp.zeros((n_layers, d, QK_PAD // 2 - ROPE), w_in.dtype)
    kr_slab = jnp.concatenate([_rotate_half_cols(w_kr), gap, w_kr, gap], -1)
    w_in_x = jnp.concatenate([
        w_in[:, :, o_q:o_q + Q_RANK], w_in[:, :, o_kv:o_kv + KV_RANK],
        w_in[:, :, o_conv:o_conv + 2 * CONV_CH], w_in[:, :, o_sgu:o_sgu + 2 * SGU_CH],
        kr_slab], -1).astype(BF16)

    wq = w_uq.reshape(n_layers, Q_RANK, HEADS, NOPE + ROPE)
    wq_pad = jnp.pad(wq, ((0, 0), (0, 0), (0, 0), (0, QK_PAD - NOPE - ROPE)))
    wqT = wq_pad.reshape(n_layers, Q_RANK, HEADS * QK_PAD).transpose(0, 2, 1).astype(BF16)
    wqr = _rotate_half_cols(wq[..., NOPE:])
    wqrT = wqr.reshape(n_layers, Q_RANK, HEADS * ROPE).transpose(0, 2, 1).astype(BF16)

    wkv = w_ukv.reshape(n_layers, KV_RANK, HEADS, NOPE + VDIM)
    wk = jnp.pad(wkv[..., :NOPE], ((0, 0), (0, 0), (0, 0), (0, QK_PAD - NOPE)))
    wk = wk.reshape(n_layers, KV_RANK, HEADS * QK_PAD).astype(BF16)
    wvT = wkv[..., NOPE:].reshape(n_layers, KV_RANK, HEADS * VDIM).transpose(0, 2, 1).astype(BF16)

    t = jnp.arange(n_lat, dtype=jnp.int32)
    n = ROPE // 4
    inv = 1.0 / (ROPE_BASE ** (jnp.arange(n, dtype=F32) / n))
    ang_r = (t // GRID_W).astype(F32)[:, None] * inv
    ang_c = (t % GRID_W).astype(F32)[:, None] * inv
    cos32 = jnp.concatenate([jnp.cos(ang_r)] * 2 + [jnp.cos(ang_c)] * 2, -1)
    sin32 = jnp.concatenate([jnp.sin(ang_r)] * 2 + [jnp.sin(ang_c)] * 2, -1)
    cos32 = jnp.concatenate([cos32, jnp.ones((n_tail, ROPE), F32)], 0)
    sin32 = jnp.concatenate([sin32, jnp.zeros((n_tail, ROPE), F32)], 0)
    pad = ((0, 0), (NOPE, QK_PAD - NOPE - ROPE))

    sgu_bias = jnp.broadcast_to(jnp.swapaxes(sgu_b, 1, 2)[:, :, :, None],
                                (n_layers, CHUNK, SGU_HEADS, SGU_HEAD_DIM)).reshape(n_layers, CHUNK, SGU_CH)

    def row(a):
        return a.reshape(n_layers, 1, a.shape[-1])

    return dict(
        norm1_g=row(norm1_g), norm2_g=row(norm2_g), w_in=w_in_x, q_norm_g=row(q_norm_g), wqT=wqT, wqrT=wqrT,
        kv_norm_g=row(kv_norm_g), wk=wk, wvT=wvT,
        cosk=jnp.pad(cos32, pad), sink=jnp.pad(sin32, pad), cosq=cos32.T, sinq=sin32.T,
        conv_w=conv_w, conv_b=row(conv_b), conv_ln_g=row(conv_ln_g), conv_ln_b=row(conv_ln_b),
        sgu_ln_g=row(sgu_ln_g), sgu_ln_b=row(sgu_ln_b), sgu_w=sgu_w.astype(BF16), sgu_bias=sgu_bias,
        w_out=w_out.astype(BF16), w_ff1=w_ff1.astype(BF16), w_ff2=w_ff2.astype(BF16),
        final_g=final_g.reshape(1, -1))


def kernel(x, c, ctx, c_ctx, ada_w, ada_b, norm1_g, norm2_g, w_in, q_norm_g, w_uq, kv_norm_g, w_ukv,
           conv_w, conv_b, conv_ln_g, conv_ln_b, sgu_ln_g, sgu_ln_b, sgu_w, sgu_b, w_out, w_ff1, w_ff2,
           final_g):
    bsz, n_lat, d = x.shape
    n_ctx = ctx.shape[1]
    n_layers = w_in.shape[0]
    assert n_lat % TOK_TILE == 0 and n_lat % GRID_W == 0
    assert n_ctx <= TOK_TILE and n_ctx % SUB_TILE == 0
    assert n_lat % n_ctx == 0
    n_lat_tiles = n_lat // TOK_TILE
    n_tiles = n_lat_tiles + 1

    p = _prepare(n_lat, TOK_TILE, norm1_g, norm2_g, w_in, q_norm_g, w_uq, kv_norm_g, w_ukv, conv_w, conv_b,
                 conv_ln_g, conv_ln_b, sgu_ln_g, sgu_ln_b, sgu_w, sgu_b, w_out, w_ff1, w_ff2, final_g)

    rows = -(-(bsz + 1) // 8) * 8
    cvec = jnp.concatenate([c, c_ctx[None, :], jnp.zeros((rows - bsz - 1, d), F32)], 0)
    mods = _ada_call(cvec, ada_w, ada_b).reshape(n_layers, rows, 6, d)
    m_lat = mods[:, :bsz]
    m_ctx = jnp.broadcast_to(mods[:, bsz][:, None], m_lat.shape)
    modsel = jnp.pad(jnp.stack([m_lat, m_ctx], 2), ((0, 0), (0, 0), (0, 0), (0, 2), (0, 0)))

    xc = jnp.concatenate([x, ctx, jnp.zeros((bsz, TOK_TILE - n_ctx, d), x.dtype)], 1)
    for layer in range(n_layers):
        last = layer == n_layers - 1
        n_act = n_lat_tiles if last else n_tiles
        qT, k, vT, yc, ys = _pre_call(layer, xc, modsel[layer], p, n_lat_tiles, n_ctx)
        aT = _attn_call(qT, k, vT, n_lat, n_ctx, not last)
        xc = _post_call(layer, xc, aT, yc, ys, modsel[layer], p, n_lat_tiles, n_act, last)
    return xc
```

```python
import functools
import math

import jax
import jax.numpy as jnp
from jax import lax
from jax.experimental import pallas as pl
from jax.experimental.pallas import tpu as pltpu

F32 = jnp.float32
BF16 = jnp.bfloat16

GRID_W = 64
HEADS = 8
NOPE = 64
ROPE = 32
VDIM = 64
QK_PAD = 128
REF_LANE = NOPE + ROPE
REF_ROWS = 16
Q_RANK = 384
KV_RANK = 256
CONV_CH = 256
CONV_WIDTH = 31
CONV_HALO = 16
SGU_HEADS = 4
SGU_HEAD_DIM = 64
SGU_CH = 256
CHUNK = 128
EPS = 1e-6
ROPE_BASE = 10000.0
ATTN_SCALE = (NOPE + ROPE) ** -0.5
LOG2E = math.log2(math.e)

TOK_TILE = 512
SUB_TILE = 256
Q_TILE = 4096
Q_BLOCK = 512
KV_STEP = 256
KV_UNROLL = 16
R0_KEYS = 16
STALE_MAX_HEADROOM = 64.0
V_EXT = VDIM + 16
FF_CHUNK = 1024
ADA_COLS = 1536
VMEM_LIMIT_BYTES = 56 * 1024 * 1024

ZC_Q = 0
ZC_KV = ZC_Q + Q_RANK
ZC_CONV = ZC_KV + KV_RANK
ZC_SGU = ZC_CONV + 2 * CONV_CH
ZC_KR = ZC_SGU + 2 * SGU_CH
Z_COLS = ZC_KR + QK_PAD


def _cparams(sem):
    return pltpu.CompilerParams(dimension_semantics=sem, vmem_limit_bytes=VMEM_LIMIT_BYTES)


def _sigmoid(v):
    return 1.0 / (1.0 + jnp.exp(-v))


def _layernorm(v, g, b):
    mu = jnp.mean(v, -1, keepdims=True)
    d = v - mu
    var = jnp.mean(d * d, -1, keepdims=True)
    return d * lax.rsqrt(var + EPS) * g + b


def _rms(v, g):
    return v * lax.rsqrt(jnp.mean(v * v, -1, keepdims=True) + EPS) * g


def _dot(a, b):
    return jnp.dot(a, b, preferred_element_type=F32)


def _dot_nt(a, b):
    return lax.dot_general(a, b, (((1,), (1,)), ((), ())), preferred_element_type=F32)


def _dot_tn(a, b):
    return lax.dot_general(a, b, (((0,), (0,)), ((), ())), preferred_element_type=F32)


def _ada_kernel(c_ref, w_ref, b_ref, o_ref):
    c = c_ref[...]
    s = (c * _sigmoid(c)).astype(BF16)
    o_ref[0] = _dot(s, w_ref[0].astype(BF16)) + b_ref[0]


def _ada_call(cvec, ada_w, ada_b):
    n_layers, d, n6 = ada_w.shape
    rows = cvec.shape[0]
    return pl.pallas_call(
        _ada_kernel,
        grid=(n_layers, n6 // ADA_COLS),
        in_specs=[
            pl.BlockSpec((rows, d), lambda l, j: (0, 0)),
            pl.BlockSpec((1, d, ADA_COLS), lambda l, j: (l, 0, j)),
            pl.BlockSpec((1, 1, ADA_COLS), lambda l, j: (l, 0, j)),
        ],
        out_specs=pl.BlockSpec((1, rows, ADA_COLS), lambda l, j: (l, 0, j)),
        out_shape=jax.ShapeDtypeStruct((n_layers, rows, n6), F32),
        compiler_params=_cparams(("arbitrary", "arbitrary")),
        name="adaln",
    )(cvec, ada_w, ada_b.reshape(n_layers, 1, n6))


def _pre_kernel(x_ref, xp_ref, xn_ref, mod_ref, n1g_ref, win_ref, qg_ref, wqT_ref, wqrT_ref,
                kvg_ref, wk_ref, wvT_ref, cosk_ref, sink_ref, cosq_ref, sinq_ref,
                convw_ref, convb_ref, clng_ref, clnb_ref, slng_ref, slnb_ref, sguw_ref, sgub_ref,
                qT_out, k_out, vT_out, yc_out, ys_out, ybuf, yshift, *, n_lat_tiles, n_ctx):
    i = pl.program_id(1)
    tt = x_ref.shape[1]
    subs = [(r, SUB_TILE) for r in range(0, tt, SUB_TILE)]
    sh1 = mod_ref[0, 0, 0:1, :]
    gain1 = n1g_ref[0] * (1.0 + mod_ref[0, 0, 1:2, :])

    def normmod(xt):
        return (_rms(xt, gain1) + sh1).astype(BF16)

    zs = [_dot(normmod(x_ref[0, r:r + n, :]), win_ref[0]) for r, n in subs]

    def glu(zc):
        return zc[:, :CONV_CH] * _sigmoid(zc[:, CONV_CH:])

    wconv_in = win_ref[0, :, ZC_CONV:ZC_CONV + 2 * CONV_CH]
    is_ctx = i == n_lat_tiles
    first = jnp.logical_or(i == 0, is_ctx)
    last = jnp.logical_or(i == n_lat_tiles - 1, is_ctx)
    ybuf[0:CONV_HALO, :] = jnp.where(first, 0.0, glu(_dot(normmod(xp_ref[0]), wconv_in)))
    ybuf[CONV_HALO + tt:, :] = jnp.where(last, 0.0, glu(_dot(normmod(xn_ref[0]), wconv_in)))
    for (r, n), z in zip(subs, zs):
        y = glu(z[:, ZC_CONV:ZC_CONV + 2 * CONV_CH])
        if r >= n_ctx:
            y = jnp.where(is_ctx, 0.0, y)
        ybuf[CONV_HALO + r:CONV_HALO + r + n, :] = y

    qscale = ATTN_SCALE * LOG2E
    for (r, n), z in zip(subs, zs):
        qn = _rms(z[:, ZC_Q:ZC_Q + Q_RANK], qg_ref[0]).astype(BF16)
        qT = _dot_nt(wqT_ref[0], qn)
        qrT = _dot_nt(wqrT_ref[0], qn)
        cq = cosq_ref[:, r:r + n]
        sq = sinq_ref[:, r:r + n]
        for h in range(HEADS):
            r0 = h * QK_PAD
            qT_out[0, r0:r0 + NOPE, r:r + n] = (qT[r0:r0 + NOPE, :] * qscale).astype(BF16)
            rot = qT[r0 + NOPE:r0 + NOPE + ROPE, :] * cq + qrT[h * ROPE:(h + 1) * ROPE, :] * sq
            qT_out[0, r0 + NOPE:r0 + NOPE + ROPE, r:r + n] = (rot * qscale).astype(BF16)
            qT_out[0, r0 + NOPE + ROPE:r0 + QK_PAD, r:r + n] = jnp.zeros((QK_PAD - NOPE - ROPE, n), BF16)

        kvn = _rms(z[:, ZC_KV:ZC_KV + KV_RANK], kvg_ref[0]).astype(BF16)
        kext = _dot(kvn, wk_ref[0])
        kr = z[:, ZC_KR:ZC_KR + QK_PAD]
        krope = kr * cosk_ref[r:r + n, :] + pltpu.roll(kr, QK_PAD // 2, 1) * sink_ref[r:r + n, :]
        krope = krope + (lax.broadcasted_iota(jnp.int32, (1, QK_PAD), 1) == REF_LANE).astype(F32)
        for h in range(HEADS):
            c0 = h * QK_PAD
            k_out[0, r:r + n, c0:c0 + QK_PAD] = (kext[:, c0:c0 + QK_PAD] + krope).astype(BF16)
        vT = _dot_nt(wvT_ref[0], kvn)
        for h in range(HEADS):
            vT_out[0, h * V_EXT:h * V_EXT + VDIM, r:r + n] = vT[h * VDIM:(h + 1) * VDIM, :].astype(BF16)
            vT_out[0, h * V_EXT + VDIM:(h + 1) * V_EXT, r:r + n] = jnp.ones((V_EXT - VDIM, n), BF16)

    lane = lax.broadcasted_iota(jnp.int32, (CHUNK, SGU_CH), 1)
    bias = sgub_ref[0]
    for (r, n), z in zip(subs, zs):
        zsg = z[:, ZC_SGU:ZC_SGU + 2 * SGU_CH]
        zg = 0.5 * zsg * (1.0 + jnp.tanh(math.sqrt(2.0 / math.pi) * (zsg + 0.044715 * (zsg * zsg * zsg))))
        u = zg[:, :SGU_CH]
        v = _layernorm(zg[:, SGU_CH:], slng_ref[0], slnb_ref[0]).astype(BF16)
        for c in range(n // CHUNK):
            vc = v[c * CHUNK:(c + 1) * CHUNK, :]
            mixed = _dot(sguw_ref[0, 0], vc)
            for h in range(1, SGU_HEADS):
                mixed = jnp.where(lane >= h * SGU_HEAD_DIM, _dot(sguw_ref[0, h], vc), mixed)
            ys_out[0, r + c * CHUNK:r + (c + 1) * CHUNK, :] = (
                u[c * CHUNK:(c + 1) * CHUNK, :] * (mixed + bias)).astype(BF16)

    cw = convw_ref[0]
    accs = [jnp.zeros((n, CONV_CH), F32) for _, n in subs]
    n_a = (CONV_WIDTH + 8) // 8
    for b in range(8):
        src = ybuf
        if b:
            yshift[...] = ybuf[b:b + yshift.shape[0], :]
            src = yshift
        for a in range(n_a):
            k = 8 * a + b - 1
            if 0 <= k < CONV_WIDTH:
                for j, (r, n) in enumerate(subs):
                    accs[j] = accs[j] + src[r + 8 * a:r + 8 * a + n, :] * cw[k:k + 1, :]
    for j, (r, n) in enumerate(subs):
        yc = _layernorm(accs[j] + convb_ref[0], clng_ref[0], clnb_ref[0])
        yc_out[0, r:r + n, :] = (yc * _sigmoid(yc)).astype(BF16)


def _pre_call(layer, xc, modsel, p, n_lat_tiles, n_ctx):
    bsz, ltot, d = xc.shape
    tt = TOK_TILE
    n_tiles = ltot // tt
    hb = tt // CONV_HALO
    n_hblocks = ltot // CONV_HALO

    def lay(shape):
        nd = len(shape)
        return pl.BlockSpec((1,) + shape, lambda b, i: (layer,) + (0,) * nd)

    in_specs = [
        pl.BlockSpec((1, tt, d), lambda b, i: (b, i, 0)),
        pl.BlockSpec((1, CONV_HALO, d), lambda b, i: (b, jnp.maximum(i * hb - 1, 0), 0)),
        pl.BlockSpec((1, CONV_HALO, d), lambda b, i: (b, jnp.minimum((i + 1) * hb, n_hblocks - 1), 0)),
        pl.BlockSpec((1, 1, 8, d), lambda b, i: (b, i // n_lat_tiles, 0, 0)),
        lay((1, d)),
        lay((d, Z_COLS)),
        lay((1, Q_RANK)),
        lay((HEADS * QK_PAD, Q_RANK)),
        lay((HEADS * ROPE, Q_RANK)),
        lay((1, KV_RANK)),
        lay((KV_RANK, HEADS * QK_PAD)),
        lay((HEADS * VDIM, KV_RANK)),
        pl.BlockSpec((tt, QK_PAD), lambda b, i: (i, 0)),
        pl.BlockSpec((tt, QK_PAD), lambda b, i: (i, 0)),
        pl.BlockSpec((ROPE, tt), lambda b, i: (0, i)),
        pl.BlockSpec((ROPE, tt), lambda b, i: (0, i)),
        lay((CONV_WIDTH, CONV_CH)),
        lay((1, CONV_CH)),
        lay((1, CONV_CH)),
        lay((1, CONV_CH)),
        lay((1, SGU_CH)),
        lay((1, SGU_CH)),
        lay((SGU_HEADS, CHUNK, CHUNK)),
        lay((CHUNK, SGU_CH)),
    ]
    out_specs = [
        pl.BlockSpec((1, HEADS * QK_PAD, tt), lambda b, i: (b, 0, i)),
        pl.BlockSpec((1, tt, HEADS * QK_PAD), lambda b, i: (b, i, 0)),
        pl.BlockSpec((1, HEADS * V_EXT, tt), lambda b, i: (b, 0, i)),
        pl.BlockSpec((1, tt, CONV_CH), lambda b, i: (b, i, 0)),
        pl.BlockSpec((1, tt, SGU_CH), lambda b, i: (b, i, 0)),
    ]
    out_shape = [
        jax.ShapeDtypeStruct((bsz, HEADS * QK_PAD, ltot), BF16),
        jax.ShapeDtypeStruct((bsz, ltot, HEADS * QK_PAD), BF16),
        jax.ShapeDtypeStruct((bsz, HEADS * V_EXT, ltot), BF16),
        jax.ShapeDtypeStruct((bsz, ltot, CONV_CH), BF16),
        jax.ShapeDtypeStruct((bsz, ltot, SGU_CH), BF16),
    ]
    return pl.pallas_call(
        functools.partial(_pre_kernel, n_lat_tiles=n_lat_tiles, n_ctx=n_ctx),
        grid=(bsz, n_tiles),
        in_specs=in_specs,
        out_specs=out_specs,
        out_shape=out_shape,
        scratch_shapes=[pltpu.VMEM((tt + 2 * CONV_HALO, CONV_CH), F32),
                        pltpu.VMEM((tt + 2 * CONV_HALO - 8, CONV_CH), F32)],
        compiler_params=_cparams(("arbitrary", "arbitrary")),
        name="pre",
    )(xc, xc, xc, modsel, p["norm1_g"], p["w_in"], p["q_norm_g"], p["wqT"], p["wqrT"],
      p["kv_norm_g"], p["wk"], p["wvT"], p["cosk"], p["sink"], p["cosq"], p["sinq"],
      p["conv_w"], p["conv_b"], p["conv_ln_g"], p["conv_ln_b"], p["sgu_ln_g"], p["sgu_ln_b"],
      p["sgu_w"], p["sgu_bias"])


def _softmax_update(s, vt, m, acc):
    m_new = jnp.maximum(m, jnp.max(s, axis=0, keepdims=True))
    alpha = jnp.exp2(m - m_new)
    pr = jnp.exp2(s - m_new).astype(BF16)
    return m_new, alpha * acc + _dot(vt, pr)


def _bf16_exact(v):
    return v.astype(BF16).astype(F32)


def _softmax_finish(acc):
    return (acc[0:VDIM, :] / acc[VDIM:VDIM + 1, :]).astype(BF16)


def _attn_lat_kernel(qT_ref, k_ref, vT_ref, o_ref, acc_scr, *, n_lat, n_ctx):
    tq = qT_ref.shape[2]
    n_cb = tq // Q_BLOCK
    n_steps = n_lat // KV_STEP

    def cols(c):
        return slice(c * Q_BLOCK, (c + 1) * Q_BLOCK)

    def scores(o, size, c, r=None):
        q = qT_ref[0, :, cols(c)]
        if r is not None:
            first_row = lax.broadcasted_iota(jnp.int32, (REF_ROWS, Q_BLOCK), 0) == 0
            q = jnp.concatenate([q[0:REF_LANE], jnp.where(first_row, -r, 0.0).astype(BF16),
                                 q[REF_LANE + REF_ROWS:]], axis=0)
        return _dot(k_ref[0, pl.ds(o, size), :], q)

    def run(blocks, carry):
        rs, excess = list(carry[0]), list(carry[1])
        n = len(blocks)
        s, pa = {}, {}

        def soft(i):
            c = blocks[i][2]
            si = s.pop(i)
            mx = jnp.max(si, axis=0, keepdims=True)
            r_new = _bf16_exact(rs[c] + jnp.maximum(mx, 0.0))
            pa[i] = (jnp.exp2(si).astype(BF16), jnp.exp2(rs[c] - r_new))
            excess[c] = jnp.maximum(excess[c], mx)
            rs[c] = r_new

        for i in range(min(2, n)):
            s[i] = scores(*blocks[i], rs[blocks[i][2]])
        soft(0)
        for i, (o, size, c) in enumerate(blocks):
            if i + 2 < n:
                s[i + 2] = scores(*blocks[i + 2], rs[blocks[i + 2][2]])
            if i + 1 < n:
                soft(i + 1)
            p, alpha = pa.pop(i)
            acc_scr[:, cols(c)] = (acc_scr[:, cols(c)] + _dot(vT_ref[0, :, pl.ds(o, size)], p)) * alpha
        return tuple(rs), tuple(excess)

    def latent_steps(j, carry):
        offs = [pl.multiple_of((j * KV_UNROLL + u) * KV_STEP, KV_STEP) for u in range(KV_UNROLL)]
        return run([(o, KV_STEP, c) for o in offs for c in range(n_cb)], carry)

    acc_scr[...] = jnp.zeros(acc_scr.shape, F32)
    carry = (tuple(_bf16_exact(jnp.max(scores(n_lat, R0_KEYS, c), axis=0, keepdims=True)) for c in range(n_cb)),
             tuple(jnp.zeros((1, Q_BLOCK), F32) for _ in range(n_cb)))
    carry = run([(n_lat, n_ctx, c) for c in range(n_cb)], carry)
    _, excess = lax.fori_loop(0, n_steps // KV_UNROLL, latent_steps, carry)
    o_ref[0] = _softmax_finish(acc_scr[...])

    @pl.when(jnp.max(jnp.concatenate(excess, axis=1)) > STALE_MAX_HEADROOM)
    def _():
        def exact_step(o, size, carry):
            s = _dot(k_ref[0, pl.ds(o, size), :], qT_ref[0])
            return _softmax_update(s, vT_ref[0, :, pl.ds(o, size)], *carry)

        carry = lax.fori_loop(
            0, n_steps, lambda t, cr: exact_step(pl.multiple_of(t * KV_STEP, KV_STEP), KV_STEP, cr),
            (jnp.full((1, tq), -jnp.inf, F32), jnp.zeros((V_EXT, tq), F32)))
        _, acc = exact_step(n_lat, n_ctx, carry)
        o_ref[0] = _softmax_finish(acc)


def _attn_ctx_kernel(qT_ref, k_ref, vT_ref, a_hbm_ref, o_ref):
    del a_hbm_ref
    tq = qT_ref.shape[2]
    for h in range(HEADS):
        s = _dot(k_ref[0, :, h * QK_PAD:(h + 1) * QK_PAD], qT_ref[0, h * QK_PAD:(h + 1) * QK_PAD, :])
        _, acc = _softmax_update(s, vT_ref[0, h * V_EXT:(h + 1) * V_EXT, :],
                                 jnp.full((1, tq), -jnp.inf, F32), jnp.zeros((V_EXT, tq), F32))
        o_ref[0, h * VDIM:(h + 1) * VDIM, 0:tq] = _softmax_finish(acc)
    if o_ref.shape[2] > tq:
        o_ref[0, :, tq:] = jnp.zeros((o_ref.shape[1], o_ref.shape[2] - tq), BF16)


def _attn_call(qT, k, vT, n_lat, n_ctx, with_ctx):
    bsz, _, ltot = qT.shape
    tq = Q_TILE
    assert n_lat % (KV_STEP * KV_UNROLL) == 0 and n_lat % tq == 0 and tq % Q_BLOCK == 0
    aT = pl.pallas_call(
        functools.partial(_attn_lat_kernel, n_lat=n_lat, n_ctx=n_ctx),
        grid=(bsz, HEADS, n_lat // tq),
        in_specs=[
            pl.BlockSpec((1, QK_PAD, tq), lambda b, h, q: (b, h, q)),
            pl.BlockSpec((1, ltot, QK_PAD), lambda b, h, q: (b, 0, h)),
            pl.BlockSpec((1, V_EXT, ltot), lambda b, h, q: (b, h, 0)),
        ],
        out_specs=pl.BlockSpec((1, VDIM, tq), lambda b, h, q: (b, h, q)),
        out_shape=jax.ShapeDtypeStruct((bsz, HEADS * VDIM, ltot), BF16),
        scratch_shapes=[pltpu.VMEM((V_EXT, tq), F32)],
        compiler_params=_cparams(("arbitrary", "arbitrary", "arbitrary")),
        name="attn_lat",
    )(qT, k, vT)
    if not with_ctx:
        return aT
    cb = n_lat // n_ctx
    return pl.pallas_call(
        _attn_ctx_kernel,
        grid=(bsz,),
        in_specs=[
            pl.BlockSpec((1, HEADS * QK_PAD, n_ctx), lambda b: (b, 0, cb)),
            pl.BlockSpec((1, n_ctx, HEADS * QK_PAD), lambda b: (b, cb, 0)),
            pl.BlockSpec((1, HEADS * V_EXT, n_ctx), lambda b: (b, 0, cb)),
            pl.BlockSpec(memory_space=pl.ANY),
        ],
        out_specs=pl.BlockSpec((1, HEADS * VDIM, TOK_TILE), lambda b: (b, 0, n_lat // TOK_TILE)),
        out_shape=jax.ShapeDtypeStruct(aT.shape, aT.dtype),
        input_output_aliases={3: 0},
        compiler_params=_cparams(("arbitrary",)),
        name="attn_ctx",
    )(qT, k, vT, aT)


def _post_kernel(x_ref, aT_ref, yc_ref, ys_ref, mod_ref, n2g_ref, wout_ref, w1_ref, w2_ref, fg_ref,
                 o_ref, *, final):
    tt, d = x_ref.shape[1], x_ref.shape[2]
    d_ff = w1_ref.shape[2]
    n_attn = aT_ref.shape[1]
    subs = [(r, SUB_TILE) for r in range(0, tt, SUB_TILE)]
    g1 = mod_ref[0, 0, 2:3, :]
    sh2 = mod_ref[0, 0, 3:4, :]
    gain2 = n2g_ref[0] * (1.0 + mod_ref[0, 0, 4:5, :])
    g2 = mod_ref[0, 0, 5:6, :]
    x1s, hs = [], []
    for r, n in subs:
        y = _dot_tn(aT_ref[0, :, r:r + n], wout_ref[0, 0:n_attn, :])
        y = y + _dot(yc_ref[0, r:r + n, :], wout_ref[0, n_attn:n_attn + CONV_CH, :])
        y = y + _dot(ys_ref[0, r:r + n, :], wout_ref[0, n_attn + CONV_CH:d, :])
        x1 = x_ref[0, r:r + n, :] + g1 * y
        x1s.append(x1)
        hs.append((_rms(x1, gain2) + sh2).astype(BF16))
    for (r, n), x1, h in zip(subs, x1s, hs):
        ff = jnp.zeros(x1.shape, F32)
        for c in range(d_ff // FF_CHUNK):
            f = jnp.maximum(_dot(h, w1_ref[0, :, c * FF_CHUNK:(c + 1) * FF_CHUNK]), 0.0)
            ff = ff + _dot((f * f).astype(BF16), w2_ref[0, c * FF_CHUNK:(c + 1) * FF_CHUNK, :])
        x2 = x1 + g2 * ff
        if final:
            x2 = _rms(x2, fg_ref[...])
        o_ref[0, r:r + n, :] = x2


def _post_call(layer, xc, aT, yc, ys, modsel, p, n_lat_tiles, n_out_tiles, final):
    bsz, _, d = xc.shape
    tt = TOK_TILE
    d_ff = p["w_ff1"].shape[2]

    def lay(shape):
        nd = len(shape)
        return pl.BlockSpec((1,) + shape, lambda b, i: (layer,) + (0,) * nd,
                            pipeline_mode=pl.Buffered(1))

    return pl.pallas_call(
        functools.partial(_post_kernel, final=final),
        grid=(bsz, n_out_tiles),
        in_specs=[
            pl.BlockSpec((1, tt, d), lambda b, i: (b, i, 0)),
            pl.BlockSpec((1, HEADS * VDIM, tt), lambda b, i: (b, 0, i)),
            pl.BlockSpec((1, tt, CONV_CH), lambda b, i: (b, i, 0)),
            pl.BlockSpec((1, tt, SGU_CH), lambda b, i: (b, i, 0)),
            pl.BlockSpec((1, 1, 8, d), lambda b, i: (b, i // n_lat_tiles, 0, 0)),
            pl.BlockSpec((1, 1, d), lambda b, i: (layer, 0, 0)),
            lay((d, d)),
            lay((d, d_ff)),
            lay((d_ff, d)),
            pl.BlockSpec((1, d), lambda b, i: (0, 0)),
        ],
        out_specs=pl.BlockSpec((1, tt, d), lambda b, i: (b, i, 0)),
        out_shape=jax.ShapeDtypeStruct((bsz, n_out_tiles * tt, d), F32),
        compiler_params=_cparams(("arbitrary", "arbitrary")),
        name="post",
    )(xc, aT, yc, ys, modsel, p["norm2_g"], p["w_out"], p["w_ff1"], p["w_ff2"], p["final_g"])


def _rotate_half_cols(w):
    n = ROPE // 4
    return jnp.concatenate([-w[..., n:2 * n], w[..., 0:n], -w[..., 3 * n:4 * n], w[..., 2 * n:3 * n]], -1)


def _prepare(n_lat, n_tail, norm1_g, norm2_g, w_in, q_norm_g, w_uq, kv_norm_g, w_ukv, conv_w, conv_b,
             conv_ln_g, conv_ln_b, sgu_ln_g, sgu_ln_b, sgu_w, sgu_b, w_out, w_ff1, w_ff2, final_g):
    n_layers, d, _ = w_in.shape
    o_q, o_kv, o_kr = 0, Q_RANK, Q_RANK + KV_RANK
    o_conv = o_kr + ROPE
    o_sgu = o_conv + 2 * CONV_CH
    w_kr = w_in[:, :, o_kr:o_kr + ROPE]

    gap = jnp.zeros((n_layers, d, QK_PAD // 2 - ROPE), w_in.dtype)
    kr_slab = jnp.concatenate([_rotate_half_cols(w_kr), gap, w_kr, gap], -1)
    w_in_x = jnp.concatenate([
        w_in[:, :, o_q:o_q + Q_RANK], w_in[:, :, o_kv:o_kv + KV_RANK],
        w_in[:, :, o_conv:o_conv + 2 * CONV_CH], w_in[:, :, o_sgu:o_sgu + 2 * SGU_CH],
        kr_slab], -1).astype(BF16)

    wq = w_uq.reshape(n_layers, Q_RANK, HEADS, NOPE + ROPE)
    wq_pad = jnp.pad(wq, ((0, 0), (0, 0), (0, 0), (0, QK_PAD - NOPE - ROPE)))
    wqT = wq_pad.reshape(n_layers, Q_RANK, HEADS * QK_PAD).transpose(0, 2, 1).astype(BF16)
    wqr = _rotate_half_cols(wq[..., NOPE:])
    wqrT = wqr.reshape(n_layers, Q_RANK, HEADS * ROPE).transpose(0, 2, 1).astype(BF16)

    wkv = w_ukv.reshape(n_layers, KV_RANK, HEADS, NOPE + VDIM)
    wk = jnp.pad(wkv[..., :NOPE], ((0, 0), (0, 0), (0, 0), (0, QK_PAD - NOPE)))
    wk = wk.reshape(n_layers, KV_RANK, HEADS * QK_PAD).astype(BF16)
    wvT = wkv[..., NOPE:].reshape(n_layers, KV_RANK, HEADS * VDIM).transpose(0, 2, 1).astype(BF16)

    t = jnp.arange(n_lat, dtype=jnp.int32)
    n = ROPE // 4
    inv = 1.0 / (ROPE_BASE ** (jnp.arange(n, dtype=F32) / n))
    ang_r = (t // GRID_W).astype(F32)[:, None] * inv
    ang_c = (t % GRID_W).astype(F32)[:, None] * inv
    cos32 = jnp.concatenate([jnp.cos(ang_r)] * 2 + [jnp.cos(ang_c)] * 2, -1)
    sin32 = jnp.concatenate([jnp.sin(ang_r)] * 2 + [jnp.sin(ang_c)] * 2, -1)
    cos32 = jnp.concatenate([cos32, jnp.ones((n_tail, ROPE), F32)], 0)
    sin32 = jnp.concatenate([sin32, jnp.zeros((n_tail, ROPE), F32)], 0)
    pad = ((0, 0), (NOPE, QK_PAD - NOPE - ROPE))

    sgu_bias = jnp.broadcast_to(jnp.swapaxes(sgu_b, 1, 2)[:, :, :, None],
                                (n_layers, CHUNK, SGU_HEADS, SGU_HEAD_DIM)).reshape(n_layers, CHUNK, SGU_CH)

    def row(a):
        return a.reshape(n_layers, 1, a.shape[-1])

    return dict(
        norm1_g=row(norm1_g), norm2_g=row(norm2_g), w_in=w_in_x, q_norm_g=row(q_norm_g), wqT=wqT, wqrT=wqrT,
        kv_norm_g=row(kv_norm_g), wk=wk, wvT=wvT,
        cosk=jnp.pad(cos32, pad), sink=jnp.pad(sin32, pad), cosq=cos32.T, sinq=sin32.T,
        conv_w=conv_w, conv_b=row(conv_b), conv_ln_g=row(conv_ln_g), conv_ln_b=row(conv_ln_b),
        sgu_ln_g=row(sgu_ln_g), sgu_ln_b=row(sgu_ln_b), sgu_w=sgu_w.astype(BF16), sgu_bias=sgu_bias,
        w_out=w_out.astype(BF16), w_ff1=w_ff1.astype(BF16), w_ff2=w_ff2.astype(BF16),
        final_g=final_g.reshape(1, -1))


def kernel(x, c, ctx, c_ctx, ada_w, ada_b, norm1_g, norm2_g, w_in, q_norm_g, w_uq, kv_norm_g, w_ukv,
           conv_w, conv_b, conv_ln_g, conv_ln_b, sgu_ln_g, sgu_ln_b, sgu_w, sgu_b, w_out, w_ff1, w_ff2,
           final_g):
    bsz, n_lat, d = x.shape
    n_ctx = ctx.shape[1]
    n_layers = w_in.shape[0]
    assert n_lat % TOK_TILE == 0 and n_lat % GRID_W == 0
    assert n_ctx <= TOK_TILE and n_ctx % SUB_TILE == 0
    assert n_lat % n_ctx == 0
    n_lat_tiles = n_lat // TOK_TILE
    n_tiles = n_lat_tiles + 1

    p = _prepare(n_lat, TOK_TILE, norm1_g, norm2_g, w_in, q_norm_g, w_uq, kv_norm_g, w_ukv, conv_w, conv_b,
                 conv_ln_g, conv_ln_b, sgu_ln_g, sgu_ln_b, sgu_w, sgu_b, w_out, w_ff1, w_ff2, final_g)

    rows = -(-(bsz + 1) // 8) * 8
    cvec = jnp.concatenate([c, c_ctx[None, :], jnp.zeros((rows - bsz - 1, d), F32)], 0)
    mods = _ada_call(cvec, ada_w, ada_b).reshape(n_layers, rows, 6, d)
    m_lat = mods[:, :bsz]
    m_ctx = jnp.broadcast_to(mods[:, bsz][:, None], m_lat.shape)
    modsel = jnp.pad(jnp.stack([m_lat, m_ctx], 2), ((0, 0), (0, 0), (0, 0), (0, 2), (0, 0)))

    xc = jnp.concatenate([x, ctx, jnp.zeros((bsz, TOK_TILE - n_ctx, d), x.dtype)], 1)
    for layer in range(n_layers):
        last = layer == n_layers - 1
        n_act = n_lat_tiles if last else n_tiles
        qT, k, vT, yc, ys = _pre_call(layer, xc, modsel[layer], p, n_lat_tiles, n_ctx)
        aT = _attn_call(qT, k, vT, n_lat, n_ctx, not last)
        xc = _post_call(layer, xc, aT, yc, ys, modsel[layer], p, n_lat_tiles, n_act, last)
    return xc
```

```python
import functools
import math

import jax
import jax.numpy as jnp
from jax import lax
from jax.experimental import pallas as pl
from jax.experimental.pallas import tpu as pltpu

F32 = jnp.float32
BF16 = jnp.bfloat16

GRID_W = 64
HEADS = 8
NOPE = 64
ROPE = 32
VDIM = 64
QK_PAD = 128
REF_LANE = NOPE + ROPE
REF_ROWS = 16
Q_RANK = 384
KV_RANK = 256
CONV_CH = 256
CONV_WIDTH = 31
CONV_HALO = 16
SGU_HEADS = 4
SGU_HEAD_DIM = 64
SGU_CH = 256
CHUNK = 128
EPS = 1e-6
ROPE_BASE = 10000.0
ATTN_SCALE = (NOPE + ROPE) ** -0.5
LOG2E = math.log2(math.e)

TOK_TILE = 512
SUB_TILE = 256
Q_TILE = 8192
Q_BLOCK = 512
KV_STEP = 256
KV_UNROLL = 8
R0_KEYS = 16
STALE_MAX_HEADROOM = 64.0
V_EXT = VDIM + 16
FF_CHUNK = 1024
ADA_COLS = 1536
VMEM_LIMIT_BYTES = 56 * 1024 * 1024

ZC_Q = 0
ZC_KV = ZC_Q + Q_RANK
ZC_CONV = ZC_KV + KV_RANK
ZC_SGU = ZC_CONV + 2 * CONV_CH
ZC_KR = ZC_SGU + 2 * SGU_CH
Z_COLS = ZC_KR + QK_PAD


def _cparams(sem):
    return pltpu.CompilerParams(dimension_semantics=sem, vmem_limit_bytes=VMEM_LIMIT_BYTES)


def _sigmoid(v):
    return 1.0 / (1.0 + jnp.exp(-v))


def _layernorm(v, g, b):
    mu = jnp.mean(v, -1, keepdims=True)
    d = v - mu
    var = jnp.mean(d * d, -1, keepdims=True)
    return d * lax.rsqrt(var + EPS) * g + b


def _rms(v, g):
    return v * lax.rsqrt(jnp.mean(v * v, -1, keepdims=True) + EPS) * g


def _dot(a, b):
    return jnp.dot(a, b, preferred_element_type=F32)


def _dot_nt(a, b):
    return lax.dot_general(a, b, (((1,), (1,)), ((), ())), preferred_element_type=F32)


def _dot_tn(a, b):
    return lax.dot_general(a, b, (((0,), (0,)), ((), ())), preferred_element_type=F32)


def _ada_kernel(c_ref, w_ref, b_ref, o_ref):
    c = c_ref[...]
    s = (c * _sigmoid(c)).astype(BF16)
    o_ref[0] = _dot(s, w_ref[0].astype(BF16)) + b_ref[0]


def _ada_call(cvec, ada_w, ada_b):
    n_layers, d, n6 = ada_w.shape
    rows = cvec.shape[0]
    return pl.pallas_call(
        _ada_kernel,
        grid=(n_layers, n6 // ADA_COLS),
        in_specs=[
            pl.BlockSpec((rows, d), lambda l, j: (0, 0)),
            pl.BlockSpec((1, d, ADA_COLS), lambda l, j: (l, 0, j)),
            pl.BlockSpec((1, 1, ADA_COLS), lambda l, j: (l, 0, j)),
        ],
        out_specs=pl.BlockSpec((1, rows, ADA_COLS), lambda l, j: (l, 0, j)),
        out_shape=jax.ShapeDtypeStruct((n_layers, rows, n6), F32),
        compiler_params=_cparams(("arbitrary", "arbitrary")),
        name="adaln",
    )(cvec, ada_w, ada_b.reshape(n_layers, 1, n6))


def _pre_kernel(x_ref, xp_ref, xn_ref, mod_ref, n1g_ref, win_ref, qg_ref, wqT_ref, wqrT_ref,
                kvg_ref, wk_ref, wvT_ref, cosk_ref, sink_ref, cosq_ref, sinq_ref,
                convw_ref, convb_ref, clng_ref, clnb_ref, slng_ref, slnb_ref, sguw_ref, sgub_ref,
                qT_out, k_out, vT_out, yc_out, ys_out, ybuf, yshift, *, n_lat_tiles, n_ctx):
    i = pl.program_id(1)
    tt = x_ref.shape[1]
    subs = [(r, SUB_TILE) for r in range(0, tt, SUB_TILE)]
    sh1 = mod_ref[0, 0, 0:1, :]
    gain1 = n1g_ref[0] * (1.0 + mod_ref[0, 0, 1:2, :])

    def normmod(xt):
        return (_rms(xt, gain1) + sh1).astype(BF16)

    zs = [_dot(normmod(x_ref[0, r:r + n, :]), win_ref[0]) for r, n in subs]

    def glu(zc):
        return zc[:, :CONV_CH] * _sigmoid(zc[:, CONV_CH:])

    wconv_in = win_ref[0, :, ZC_CONV:ZC_CONV + 2 * CONV_CH]
    is_ctx = i == n_lat_tiles
    first = jnp.logical_or(i == 0, is_ctx)
    last = jnp.logical_or(i == n_lat_tiles - 1, is_ctx)
    ybuf[0:CONV_HALO, :] = jnp.where(first, 0.0, glu(_dot(normmod(xp_ref[0]), wconv_in)))
    ybuf[CONV_HALO + tt:, :] = jnp.where(last, 0.0, glu(_dot(normmod(xn_ref[0]), wconv_in)))
    for (r, n), z in zip(subs, zs):
        y = glu(z[:, ZC_CONV:ZC_CONV + 2 * CONV_CH])
        if r >= n_ctx:
            y = jnp.where(is_ctx, 0.0, y)
        ybuf[CONV_HALO + r:CONV_HALO + r + n, :] = y

    qscale = ATTN_SCALE * LOG2E
    for (r, n), z in zip(subs, zs):
        qn = _rms(z[:, ZC_Q:ZC_Q + Q_RANK], qg_ref[0]).astype(BF16)
        qT = _dot_nt(wqT_ref[0], qn)
        qrT = _dot_nt(wqrT_ref[0], qn)
        cq = cosq_ref[:, r:r + n]
        sq = sinq_ref[:, r:r + n]
        for h in range(HEADS):
            r0 = h * QK_PAD
            qT_out[0, r0:r0 + NOPE, r:r + n] = (qT[r0:r0 + NOPE, :] * qscale).astype(BF16)
            rot = qT[r0 + NOPE:r0 + NOPE + ROPE, :] * cq + qrT[h * ROPE:(h + 1) * ROPE, :] * sq
            qT_out[0, r0 + NOPE:r0 + NOPE + ROPE, r:r + n] = (rot * qscale).astype(BF16)
            qT_out[0, r0 + NOPE + ROPE:r0 + QK_PAD, r:r + n] = jnp.zeros((QK_PAD - NOPE - ROPE, n), BF16)

        kvn = _rms(z[:, ZC_KV:ZC_KV + KV_RANK], kvg_ref[0]).astype(BF16)
        kext = _dot(kvn, wk_ref[0])
        kr = z[:, ZC_KR:ZC_KR + QK_PAD]
        krope = kr * cosk_ref[r:r + n, :] + pltpu.roll(kr, QK_PAD // 2, 1) * sink_ref[r:r + n, :]
        krope = krope + (lax.broadcasted_iota(jnp.int32, (1, QK_PAD), 1) == REF_LANE).astype(F32)
        for h in range(HEADS):
            c0 = h * QK_PAD
            k_out[0, r:r + n, c0:c0 + QK_PAD] = (kext[:, c0:c0 + QK_PAD] + krope).astype(BF16)
        vT = _dot_nt(wvT_ref[0], kvn)
        for h in range(HEADS):
            vT_out[0, h * V_EXT:h * V_EXT + VDIM, r:r + n] = vT[h * VDIM:(h + 1) * VDIM, :].astype(BF16)
            vT_out[0, h * V_EXT + VDIM:(h + 1) * V_EXT, r:r + n] = jnp.ones((V_EXT - VDIM, n), BF16)

    lane = lax.broadcasted_iota(jnp.int32, (CHUNK, SGU_CH), 1)
    bias = sgub_ref[0]
    for (r, n), z in zip(subs, zs):
        zsg = z[:, ZC_SGU:ZC_SGU + 2 * SGU_CH]
        zg = 0.5 * zsg * (1.0 + jnp.tanh(math.sqrt(2.0 / math.pi) * (zsg + 0.044715 * (zsg * zsg * zsg))))
        u = zg[:, :SGU_CH]
        v = _layernorm(zg[:, SGU_CH:], slng_ref[0], slnb_ref[0]).astype(BF16)
        for c in range(n // CHUNK):
            vc = v[c * CHUNK:(c + 1) * CHUNK, :]
            mixed = _dot(sguw_ref[0, 0], vc)
            for h in range(1, SGU_HEADS):
                mixed = jnp.where(lane >= h * SGU_HEAD_DIM, _dot(sguw_ref[0, h], vc), mixed)
            ys_out[0, r + c * CHUNK:r + (c + 1) * CHUNK, :] = (
                u[c * CHUNK:(c + 1) * CHUNK, :] * (mixed + bias)).astype(BF16)

    cw = convw_ref[0]
    accs = [jnp.zeros((n, CONV_CH), F32) for _, n in subs]
    n_a = (CONV_WIDTH + 8) // 8
    for b in range(8):
        src = ybuf
        if b:
            yshift[...] = ybuf[b:b + yshift.shape[0], :]
            src = yshift
        for a in range(n_a):
            k = 8 * a + b - 1
            if 0 <= k < CONV_WIDTH:
                for j, (r, n) in enumerate(subs):
                    accs[j] = accs[j] + src[r + 8 * a:r + 8 * a + n, :] * cw[k:k + 1, :]
    for j, (r, n) in enumerate(subs):
        yc = _layernorm(accs[j] + convb_ref[0], clng_ref[0], clnb_ref[0])
        yc_out[0, r:r + n, :] = (yc * _sigmoid(yc)).astype(BF16)


def _pre_call(layer, xc, modsel, p, n_lat_tiles, n_ctx):
    bsz, ltot, d = xc.shape
    tt = TOK_TILE
    n_tiles = ltot // tt
    hb = tt // CONV_HALO
    n_hblocks = ltot // CONV_HALO

    def lay(shape):
        nd = len(shape)
        return pl.BlockSpec((1,) + shape, lambda b, i: (layer,) + (0,) * nd)

    in_specs = [
        pl.BlockSpec((1, tt, d), lambda b, i: (b, i, 0)),
        pl.BlockSpec((1, CONV_HALO, d), lambda b, i: (b, jnp.maximum(i * hb - 1, 0), 0)),
        pl.BlockSpec((1, CONV_HALO, d), lambda b, i: (b, jnp.minimum((i + 1) * hb, n_hblocks - 1), 0)),
        pl.BlockSpec((1, 1, 8, d), lambda b, i: (b, i // n_lat_tiles, 0, 0)),
        lay((1, d)),
        lay((d, Z_COLS)),
        lay((1, Q_RANK)),
        lay((HEADS * QK_PAD, Q_RANK)),
        lay((HEADS * ROPE, Q_RANK)),
        lay((1, KV_RANK)),
        lay((KV_RANK, HEADS * QK_PAD)),
        lay((HEADS * VDIM, KV_RANK)),
        pl.BlockSpec((tt, QK_PAD), lambda b, i: (i, 0)),
        pl.BlockSpec((tt, QK_PAD), lambda b, i: (i, 0)),
        pl.BlockSpec((ROPE, tt), lambda b, i: (0, i)),
        pl.BlockSpec((ROPE, tt), lambda b, i: (0, i)),
        lay((CONV_WIDTH, CONV_CH)),
        lay((1, CONV_CH)),
        lay((1, CONV_CH)),
        lay((1, CONV_CH)),
        lay((1, SGU_CH)),
        lay((1, SGU_CH)),
        lay((SGU_HEADS, CHUNK, CHUNK)),
        lay((CHUNK, SGU_CH)),
    ]
    out_specs = [
        pl.BlockSpec((1, HEADS * QK_PAD, tt), lambda b, i: (b, 0, i)),
        pl.BlockSpec((1, tt, HEADS * QK_PAD), lambda b, i: (b, i, 0)),
        pl.BlockSpec((1, HEADS * V_EXT, tt), lambda b, i: (b, 0, i)),
        pl.BlockSpec((1, tt, CONV_CH), lambda b, i: (b, i, 0)),
        pl.BlockSpec((1, tt, SGU_CH), lambda b, i: (b, i, 0)),
    ]
    out_shape = [
        jax.ShapeDtypeStruct((bsz, HEADS * QK_PAD, ltot), BF16),
        jax.ShapeDtypeStruct((bsz, ltot, HEADS * QK_PAD), BF16),
        jax.ShapeDtypeStruct((bsz, HEADS * V_EXT, ltot), BF16),
        jax.ShapeDtypeStruct((bsz, ltot, CONV_CH), BF16),
        jax.ShapeDtypeStruct((bsz, ltot, SGU_CH), BF16),
    ]
    return pl.pallas_call(
        functools.partial(_pre_kernel, n_lat_tiles=n_lat_tiles, n_ctx=n_ctx),
        grid=(bsz, n_tiles),
        in_specs=in_specs,
        out_specs=out_specs,
        out_shape=out_shape,
        scratch_shapes=[pltpu.VMEM((tt + 2 * CONV_HALO, CONV_CH), F32),
                        pltpu.VMEM((tt + 2 * CONV_HALO - 8, CONV_CH), F32)],
        compiler_params=_cparams(("arbitrary", "arbitrary")),
        name="pre",
    )(xc, xc, xc, modsel, p["norm1_g"], p["w_in"], p["q_norm_g"], p["wqT"], p["wqrT"],
      p["kv_norm_g"], p["wk"], p["wvT"], p["cosk"], p["sink"], p["cosq"], p["sinq"],
      p["conv_w"], p["conv_b"], p["conv_ln_g"], p["conv_ln_b"], p["sgu_ln_g"], p["sgu_ln_b"],
      p["sgu_w"], p["sgu_bias"])


def _softmax_update(s, vt, m, acc):
    m_new = jnp.maximum(m, jnp.max(s, axis=0, keepdims=True))
    alpha = jnp.exp2(m - m_new)
    pr = jnp.exp2(s - m_new).astype(BF16)
    return m_new, alpha * acc + _dot(vt, pr)


def _bf16_exact(v):
    return v.astype(BF16).astype(F32)


def _softmax_finish(acc):
    return (acc[0:VDIM, :] / acc[VDIM:VDIM + 1, :]).astype(BF16)


def _attn_lat_kernel(qT_ref, k_ref, vT_ref, o_ref, acc_scr, *, n_lat, n_ctx):
    tq = qT_ref.shape[2]
    n_cb = tq // Q_BLOCK
    n_steps = n_lat // KV_STEP

    def cols(c):
        return slice(c * Q_BLOCK, (c + 1) * Q_BLOCK)

    def scores(o, size, c, r=None):
        q = qT_ref[0, :, cols(c)]
        if r is not None:
            first_row = lax.broadcasted_iota(jnp.int32, (REF_ROWS, Q_BLOCK), 0) == 0
            q = jnp.concatenate([q[0:REF_LANE], jnp.where(first_row, -r, 0.0).astype(BF16),
                                 q[REF_LANE + REF_ROWS:]], axis=0)
        return _dot(k_ref[0, pl.ds(o, size), :], q)

    def run(blocks, carry):
        rs, excess = list(carry[0]), list(carry[1])
        n = len(blocks)
        s, pa = {}, {}

        def soft(i):
            c = blocks[i][2]
            si = s.pop(i)
            mx = jnp.max(si, axis=0, keepdims=True)
            r_new = _bf16_exact(rs[c] + jnp.maximum(mx, 0.0))
            pa[i] = (jnp.exp2(si).astype(BF16), jnp.exp2(rs[c] - r_new))
            excess[c] = jnp.maximum(excess[c], mx)
            rs[c] = r_new

        for i in range(min(2, n)):
            s[i] = scores(*blocks[i], rs[blocks[i][2]])
        soft(0)
        for i, (o, size, c) in enumerate(blocks):
            if i + 2 < n:
                s[i + 2] = scores(*blocks[i + 2], rs[blocks[i + 2][2]])
            if i + 1 < n:
                soft(i + 1)
            p, alpha = pa.pop(i)
            acc_scr[:, cols(c)] = (acc_scr[:, cols(c)] + _dot(vT_ref[0, :, pl.ds(o, size)], p)) * alpha
        return tuple(rs), tuple(excess)

    def latent_steps(j, carry):
        offs = [pl.multiple_of((j * KV_UNROLL + u) * KV_STEP, KV_STEP) for u in range(KV_UNROLL)]
        return run([(o, KV_STEP, c) for o in offs for c in range(n_cb)], carry)

    acc_scr[...] = jnp.zeros(acc_scr.shape, F32)
    carry = (tuple(_bf16_exact(jnp.max(scores(n_lat, R0_KEYS, c), axis=0, keepdims=True)) for c in range(n_cb)),
             tuple(jnp.zeros((1, Q_BLOCK), F32) for _ in range(n_cb)))
    carry = run([(n_lat, n_ctx, c) for c in range(n_cb)], carry)
    _, excess = lax.fori_loop(0, n_steps // KV_UNROLL, latent_steps, carry)
    o_ref[0] = _softmax_finish(acc_scr[...])

    @pl.when(jnp.max(jnp.concatenate(excess, axis=1)) > STALE_MAX_HEADROOM)
    def _():
        def exact_step(o, size, carry):
            s = _dot(k_ref[0, pl.ds(o, size), :], qT_ref[0])
            return _softmax_update(s, vT_ref[0, :, pl.ds(o, size)], *carry)

        carry = lax.fori_loop(
            0, n_steps, lambda t, cr: exact_step(pl.multiple_of(t * KV_STEP, KV_STEP), KV_STEP, cr),
            (jnp.full((1, tq), -jnp.inf, F32), jnp.zeros((V_EXT, tq), F32)))
        _, acc = exact_step(n_lat, n_ctx, carry)
        o_ref[0] = _softmax_finish(acc)


def _attn_ctx_kernel(qT_ref, k_ref, vT_ref, a_hbm_ref, o_ref):
    del a_hbm_ref
    tq = qT_ref.shape[2]
    for h in range(HEADS):
        s = _dot(k_ref[0, :, h * QK_PAD:(h + 1) * QK_PAD], qT_ref[0, h * QK_PAD:(h + 1) * QK_PAD, :])
        _, acc = _softmax_update(s, vT_ref[0, h * V_EXT:(h + 1) * V_EXT, :],
                                 jnp.full((1, tq), -jnp.inf, F32), jnp.zeros((V_EXT, tq), F32))
        o_ref[0, h * VDIM:(h + 1) * VDIM, 0:tq] = _softmax_finish(acc)
    if o_ref.shape[2] > tq:
        o_ref[0, :, tq:] = jnp.zeros((o_ref.shape[1], o_ref.shape[2] - tq), BF16)


def _attn_call(qT, k, vT, n_lat, n_ctx, with_ctx):
    bsz, _, ltot = qT.shape
    tq = Q_TILE
    assert n_lat % (KV_STEP * KV_UNROLL) == 0 and n_lat % tq == 0 and tq % Q_BLOCK == 0
    aT = pl.pallas_call(
        functools.partial(_attn_lat_kernel, n_lat=n_lat, n_ctx=n_ctx),
        grid=(bsz, HEADS, n_lat // tq),
        in_specs=[
            pl.BlockSpec((1, QK_PAD, tq), lambda b, h, q: (b, h, q)),
            pl.BlockSpec((1, ltot, QK_PAD), lambda b, h, q: (b, 0, h)),
            pl.BlockSpec((1, V_EXT, ltot), lambda b, h, q: (b, h, 0)),
        ],
        out_specs=pl.BlockSpec((1, VDIM, tq), lambda b, h, q: (b, h, q)),
        out_shape=jax.ShapeDtypeStruct((bsz, HEADS * VDIM, ltot), BF16),
        scratch_shapes=[pltpu.VMEM((V_EXT, tq), F32)],
        compiler_params=_cparams(("arbitrary", "arbitrary", "arbitrary")),
        name="attn_lat",
    )(qT, k, vT)
    if not with_ctx:
        return aT
    cb = n_lat // n_ctx
    return pl.pallas_call(
        _attn_ctx_kernel,
        grid=(bsz,),
        in_specs=[
            pl.BlockSpec((1, HEADS * QK_PAD, n_ctx), lambda b: (b, 0, cb)),
            pl.BlockSpec((1, n_ctx, HEADS * QK_PAD), lambda b: (b, cb, 0)),
            pl.BlockSpec((1, HEADS * V_EXT, n_ctx), lambda b: (b, 0, cb)),
            pl.BlockSpec(memory_space=pl.ANY),
        ],
        out_specs=pl.BlockSpec((1, HEADS * VDIM, TOK_TILE), lambda b: (b, 0, n_lat // TOK_TILE)),
        out_shape=jax.ShapeDtypeStruct(aT.shape, aT.dtype),
        input_output_aliases={3: 0},
        compiler_params=_cparams(("arbitrary",)),
        name="attn_ctx",
    )(qT, k, vT, aT)


def _post_kernel(x_ref, aT_ref, yc_ref, ys_ref, mod_ref, n2g_ref, wout_ref, w1_ref, w2_ref, fg_ref,
                 o_ref, *, final):
    tt, d = x_ref.shape[1], x_ref.shape[2]
    d_ff = w1_ref.shape[2]
    n_attn = aT_ref.shape[1]
    subs = [(r, SUB_TILE) for r in range(0, tt, SUB_TILE)]
    g1 = mod_ref[0, 0, 2:3, :]
    sh2 = mod_ref[0, 0, 3:4, :]
    gain2 = n2g_ref[0] * (1.0 + mod_ref[0, 0, 4:5, :])
    g2 = mod_ref[0, 0, 5:6, :]
    x1s, hs = [], []
    for r, n in subs:
        y = _dot_tn(aT_ref[0, :, r:r + n], wout_ref[0, 0:n_attn, :])
        y = y + _dot(yc_ref[0, r:r + n, :], wout_ref[0, n_attn:n_attn + CONV_CH, :])
        y = y + _dot(ys_ref[0, r:r + n, :], wout_ref[0, n_attn + CONV_CH:d, :])
        x1 = x_ref[0, r:r + n, :] + g1 * y
        x1s.append(x1)
        hs.append((_rms(x1, gain2) + sh2).astype(BF16))
    for (r, n), x1, h in zip(subs, x1s, hs):
        ff = jnp.zeros(x1.shape, F32)
        for c in range(d_ff // FF_CHUNK):
            f = jnp.maximum(_dot(h, w1_ref[0, :, c * FF_CHUNK:(c + 1) * FF_CHUNK]), 0.0)
            ff = ff + _dot((f * f).astype(BF16), w2_ref[0, c * FF_CHUNK:(c + 1) * FF_CHUNK, :])
        x2 = x1 + g2 * ff
        if final:
            x2 = _rms(x2, fg_ref[...])
        o_ref[0, r:r + n, :] = x2


def _post_call(layer, xc, aT, yc, ys, modsel, p, n_lat_tiles, n_out_tiles, final):
    bsz, _, d = xc.shape
    tt = TOK_TILE
    d_ff = p["w_ff1"].shape[2]

    def lay(shape):
        nd = len(shape)
        return pl.BlockSpec((1,) + shape, lambda b, i: (layer,) + (0,) * nd,
                            pipeline_mode=pl.Buffered(1))

    return pl.pallas_call(
        functools.partial(_post_kernel, final=final),
        grid=(bsz, n_out_tiles),
        in_specs=[
            pl.BlockSpec((1, tt, d), lambda b, i: (b, i, 0)),
            pl.BlockSpec((1, HEADS * VDIM, tt), lambda b, i: (b, 0, i)),
            pl.BlockSpec((1, tt, CONV_CH), lambda b, i: (b, i, 0)),
            pl.BlockSpec((1, tt, SGU_CH), lambda b, i: (b, i, 0)),
            pl.BlockSpec((1, 1, 8, d), lambda b, i: (b, i // n_lat_tiles, 0, 0)),
            pl.BlockSpec((1, 1, d), lambda b, i: (layer, 0, 0)),
            lay((d, d)),
            lay((d, d_ff)),
            lay((d_ff, d)),
            pl.BlockSpec((1, d), lambda b, i: (0, 0)),
        ],
        out_specs=pl.BlockSpec((1, tt, d), lambda b, i: (b, i, 0)),
        out_shape=jax.ShapeDtypeStruct((bsz, n_out_tiles * tt, d), F32),
        compiler_params=_cparams(("arbitrary", "arbitrary")),
        name="post",
    )(xc, aT, yc, ys, modsel, p["norm2_g"], p["w_out"], p["w_ff1"], p["w_ff2"], p["final_g"])


def _rotate_half_cols(w):
    n = ROPE // 4
    return jnp.concatenate([-w[..., n:2 * n], w[..., 0:n], -w[..., 3 * n:4 * n], w[..., 2 * n:3 * n]], -1)


def _prepare(n_lat, n_tail, norm1_g, norm2_g, w_in, q_norm_g, w_uq, kv_norm_g, w_ukv, conv_w, conv_b,
             conv_ln_g, conv_ln_b, sgu_ln_g, sgu_ln_b, sgu_w, sgu_b, w_out, w_ff1, w_ff2, final_g):
    n_layers, d, _ = w_in.shape
    o_q, o_kv, o_kr = 0, Q_RANK, Q_RANK + KV_RANK
    o_conv = o_kr + ROPE
    o_sgu = o_conv + 2 * CONV_CH
    w_kr = w_in[:, :, o_kr:o_kr + ROPE]

    gap = jnp.zeros((n_layers, d, QK_PAD // 2 - ROPE), w_in.dtype)
    kr_slab = jnp.concatenate([_rotate_half_cols(w_kr), gap, w_kr, gap], -1)
    w_in_x = jnp.concatenate([
        w_in[:, :, o_q:o_q + Q_RANK], w_in[:, :, o_kv:o_kv + KV_RANK],
        w_in[:, :, o_conv:o_conv + 2 * CONV_CH], w_in[:, :, o_sgu:o_sgu + 2 * SGU_CH],
        kr_slab], -1).astype(BF16)

    wq = w_uq.reshape(n_layers, Q_RANK, HEADS, NOPE + ROPE)
    wq_pad = jnp.pad(wq, ((0, 0), (0, 0), (0, 0), (0, QK_PAD - NOPE - ROPE)))
    wqT = wq_pad.reshape(n_layers, Q_RANK, HEADS * QK_PAD).transpose(0, 2, 1).astype(BF16)
    wqr = _rotate_half_cols(wq[..., NOPE:])
    wqrT = wqr.reshape(n_layers, Q_RANK, HEADS * ROPE).transpose(0, 2, 1).astype(BF16)

    wkv = w_ukv.reshape(n_layers, KV_RANK, HEADS, NOPE + VDIM)
    wk = jnp.pad(wkv[..., :NOPE], ((0, 0), (0, 0), (0, 0), (0, QK_PAD - NOPE)))
    wk = wk.reshape(n_layers, KV_RANK, HEADS * QK_PAD).astype(BF16)
    wvT = wkv[..., NOPE:].reshape(n_layers, KV_RANK, HEADS * VDIM).transpose(0, 2, 1).astype(BF16)

    t = jnp.arange(n_lat, dtype=jnp.int32)
    n = ROPE // 4
    inv = 1.0 / (ROPE_BASE ** (jnp.arange(n, dtype=F32) / n))
    ang_r = (t // GRID_W).astype(F32)[:, None] * inv
    ang_c = (t % GRID_W).astype(F32)[:, None] * inv
    cos32 = jnp.concatenate([jnp.cos(ang_r)] * 2 + [jnp.cos(ang_c)] * 2, -1)
    sin32 = jnp.concatenate([jnp.sin(ang_r)] * 2 + [jnp.sin(ang_c)] * 2, -1)
    cos32 = jnp.concatenate([cos32, jnp.ones((n_tail, ROPE), F32)], 0)
    sin32 = jnp.concatenate([sin32, jnp.zeros((n_tail, ROPE), F32)], 0)
    pad = ((0, 0), (NOPE, QK_PAD - NOPE - ROPE))

    sgu_bias = jnp.broadcast_to(jnp.swapaxes(sgu_b, 1, 2)[:, :, :, None],
                                (n_layers, CHUNK, SGU_HEADS, SGU_HEAD_DIM)).reshape(n_layers, CHUNK, SGU_CH)

    def row(a):
        return a.reshape(n_layers, 1, a.shape[-1])

    return dict(
        norm1_g=row(norm1_g), norm2_g=row(norm2_g), w_in=w_in_x, q_norm_g=row(q_norm_g), wqT=wqT, wqrT=wqrT,
        kv_norm_g=row(kv_norm_g), wk=wk, wvT=wvT,
        cosk=jnp.pad(cos32, pad), sink=jnp.pad(sin32, pad), cosq=cos32.T, sinq=sin32.T,
        conv_w=conv_w, conv_b=row(conv_b), conv_ln_g=row(conv_ln_g), conv_ln_b=row(conv_ln_b),
        sgu_ln_g=row(sgu_ln_g), sgu_ln_b=row(sgu_ln_b), sgu_w=sgu_w.astype(BF16), sgu_bias=sgu_bias,
        w_out=w_out.astype(BF16), w_ff1=w_ff1.astype(BF16), w_ff2=w_ff2.astype(BF16),
        final_g=final_g.reshape(1, -1))


def kernel(x, c, ctx, c_ctx, ada_w, ada_b, norm1_g, norm2_g, w_in, q_norm_g, w_uq, kv_norm_g, w_ukv,
           conv_w, conv_b, conv_ln_g, conv_ln_b, sgu_ln_g, sgu_ln_b, sgu_w, sgu_b, w_out, w_ff1, w_ff2,
           final_g):
    bsz, n_lat, d = x.shape
    n_ctx = ctx.shape[1]
    n_layers = w_in.shape[0]
    assert n_lat % TOK_TILE == 0 and n_lat % GRID_W == 0
    assert n_ctx <= TOK_TILE and n_ctx % SUB_TILE == 0
    assert n_lat % n_ctx == 0
    n_lat_tiles = n_lat // TOK_TILE
    n_tiles = n_lat_tiles + 1

    p = _prepare(n_lat, TOK_TILE, norm1_g, norm2_g, w_in, q_norm_g, w_uq, kv_norm_g, w_ukv, conv_w, conv_b,
                 conv_ln_g, conv_ln_b, sgu_ln_g, sgu_ln_b, sgu_w, sgu_b, w_out, w_ff1, w_ff2, final_g)

    rows = -(-(bsz + 1) // 8) * 8
    cvec = jnp.concatenate([c, c_ctx[None, :], jnp.zeros((rows - bsz - 1, d), F32)], 0)
    mods = _ada_call(cvec, ada_w, ada_b).reshape(n_layers, rows, 6, d)
    m_lat = mods[:, :bsz]
    m_ctx = jnp.broadcast_to(mods[:, bsz][:, None], m_lat.shape)
    modsel = jnp.pad(jnp.stack([m_lat, m_ctx], 2), ((0, 0), (0, 0), (0, 0), (0, 2), (0, 0)))

    xc = jnp.concatenate([x, ctx, jnp.zeros((bsz, TOK_TILE - n_ctx, d), x.dtype)], 1)
    for layer in range(n_layers):
        last = layer == n_layers - 1
        n_act = n_lat_tiles if last else n_tiles
        qT, k, vT, yc, ys = _pre_call(layer, xc, modsel[layer], p, n_lat_tiles, n_ctx)
        aT = _attn_call(qT, k, vT, n_lat, n_ctx, not last)
        xc = _post_call(layer, xc, aT, yc, ys, modsel[layer], p, n_lat_tiles, n_act, last)
    return xc
```

```python
import functools
import math

import jax
import jax.numpy as jnp
from jax import lax
from jax.experimental import pallas as pl
from jax.experimental.pallas import tpu as pltpu

F32 = jnp.float32
BF16 = jnp.bfloat16

GRID_W = 64
HEADS = 8
NOPE = 64
ROPE = 32
VDIM = 64
QK_PAD = 128
REF_LANE = NOPE + ROPE
REF_ROWS = 16
Q_RANK = 384
KV_RANK = 256
CONV_CH = 256
CONV_WIDTH = 31
CONV_HALO = 16
SGU_HEADS = 4
SGU_HEAD_DIM = 64
SGU_CH = 256
CHUNK = 128
EPS = 1e-6
ROPE_BASE = 10000.0
ATTN_SCALE = (NOPE + ROPE) ** -0.5
LOG2E = math.log2(math.e)

TOK_TILE = 512
SUB_TILE = 256
Q_TILE = 8192
Q_BLOCK = 512
KV_STEP = 256
KV_UNROLL = 8
R0_KEYS = 16
STALE_MAX_HEADROOM = 64.0
V_EXT = VDIM + 16
FF_CHUNK = 1024
ADA_COLS = 1536
VMEM_LIMIT_BYTES = 56 * 1024 * 1024

ZC_Q = 0
ZC_KV = ZC_Q + Q_RANK
ZC_CONV = ZC_KV + KV_RANK
ZC_SGU = ZC_CONV + 2 * CONV_CH
ZC_KR = ZC_SGU + 2 * SGU_CH
Z_COLS = ZC_KR + QK_PAD


def _cparams(sem):
    return pltpu.CompilerParams(dimension_semantics=sem, vmem_limit_bytes=VMEM_LIMIT_BYTES)


def _sigmoid(v):
    return 1.0 / (1.0 + jnp.exp(-v))


def _layernorm(v, g, b):
    mu = jnp.mean(v, -1, keepdims=True)
    d = v - mu
    var = jnp.mean(d * d, -1, keepdims=True)
    return d * lax.rsqrt(var + EPS) * g + b


def _rms(v, g):
    return v * lax.rsqrt(jnp.mean(v * v, -1, keepdims=True) + EPS) * g


def _dot(a, b):
    return jnp.dot(a, b, preferred_element_type=F32)


def _dot_nt(a, b):
    return lax.dot_general(a, b, (((1,), (1,)), ((), ())), preferred_element_type=F32)


def _dot_tn(a, b):
    return lax.dot_general(a, b, (((0,), (0,)), ((), ())), preferred_element_type=F32)


def _ada_kernel(c_ref, w_ref, b_ref, o_ref):
    c = c_ref[...]
    s = (c * _sigmoid(c)).astype(BF16)
    o_ref[0] = _dot(s, w_ref[0].astype(BF16)) + b_ref[0]


def _ada_call(cvec, ada_w, ada_b):
    n_layers, d, n6 = ada_w.shape
    rows = cvec.shape[0]
    return pl.pallas_call(
        _ada_kernel,
        grid=(n_layers, n6 // ADA_COLS),
        in_specs=[
            pl.BlockSpec((rows, d), lambda l, j: (0, 0)),
            pl.BlockSpec((1, d, ADA_COLS), lambda l, j: (l, 0, j)),
            pl.BlockSpec((1, 1, ADA_COLS), lambda l, j: (l, 0, j)),
        ],
        out_specs=pl.BlockSpec((1, rows, ADA_COLS), lambda l, j: (l, 0, j)),
        out_shape=jax.ShapeDtypeStruct((n_layers, rows, n6), F32),
        compiler_params=_cparams(("arbitrary", "arbitrary")),
        name="adaln",
    )(cvec, ada_w, ada_b.reshape(n_layers, 1, n6))


def _pre_kernel(x_ref, tail_ref, xp_ref, xn_ref, mod_ref, n1g_ref, win_ref, qg_ref, wqT_ref, wqrT_ref,
                kvg_ref, wk_ref, wvT_ref, cosk_ref, sink_ref, cosq_ref, sinq_ref,
                convw_ref, convb_ref, clng_ref, clnb_ref, slng_ref, slnb_ref, sguw_ref, sgub_ref,
                qT_out, k_out, vT_out, yc_out, ys_out, ybuf, yshift, *, n_lat_tiles, n_ctx, split):
    i = pl.program_id(1)
    tt = x_ref.shape[1]
    subs = [(r, SUB_TILE) for r in range(0, tt, SUB_TILE)]
    sh1 = mod_ref[0, 0, 0:1, :]
    gain1 = n1g_ref[0] * (1.0 + mod_ref[0, 0, 1:2, :])

    def normmod(xt):
        return (_rms(xt, gain1) + sh1).astype(BF16)

    is_ctx = i == n_lat_tiles

    def rows(r, n):
        xt = x_ref[0, r:r + n, :]
        return jnp.where(is_ctx, tail_ref[0, r:r + n, :], xt) if split else xt

    zs = [_dot(normmod(rows(r, n)), win_ref[0]) for r, n in subs]

    def glu(zc):
        return zc[:, :CONV_CH] * _sigmoid(zc[:, CONV_CH:])

    wconv_in = win_ref[0, :, ZC_CONV:ZC_CONV + 2 * CONV_CH]
    first = jnp.logical_or(i == 0, is_ctx)
    last = jnp.logical_or(i == n_lat_tiles - 1, is_ctx)
    ybuf[0:CONV_HALO, :] = jnp.where(first, 0.0, glu(_dot(normmod(xp_ref[0]), wconv_in)))
    ybuf[CONV_HALO + tt:, :] = jnp.where(last, 0.0, glu(_dot(normmod(xn_ref[0]), wconv_in)))
    for (r, n), z in zip(subs, zs):
        y = glu(z[:, ZC_CONV:ZC_CONV + 2 * CONV_CH])
        if r >= n_ctx:
            y = jnp.where(is_ctx, 0.0, y)
        ybuf[CONV_HALO + r:CONV_HALO + r + n, :] = y

    qscale = ATTN_SCALE * LOG2E
    for (r, n), z in zip(subs, zs):
        qn = _rms(z[:, ZC_Q:ZC_Q + Q_RANK], qg_ref[0]).astype(BF16)
        qT = _dot_nt(wqT_ref[0], qn)
        qrT = _dot_nt(wqrT_ref[0], qn)
        cq = cosq_ref[:, r:r + n]
        sq = sinq_ref[:, r:r + n]
        for h in range(HEADS):
            r0 = h * QK_PAD
            qT_out[0, r0:r0 + NOPE, r:r + n] = (qT[r0:r0 + NOPE, :] * qscale).astype(BF16)
            rot = qT[r0 + NOPE:r0 + NOPE + ROPE, :] * cq + qrT[h * ROPE:(h + 1) * ROPE, :] * sq
            qT_out[0, r0 + NOPE:r0 + NOPE + ROPE, r:r + n] = (rot * qscale).astype(BF16)
            qT_out[0, r0 + NOPE + ROPE:r0 + QK_PAD, r:r + n] = jnp.zeros((QK_PAD - NOPE - ROPE, n), BF16)

        kvn = _rms(z[:, ZC_KV:ZC_KV + KV_RANK], kvg_ref[0]).astype(BF16)
        kext = _dot(kvn, wk_ref[0])
        kr = z[:, ZC_KR:ZC_KR + QK_PAD]
        krope = kr * cosk_ref[r:r + n, :] + pltpu.roll(kr, QK_PAD // 2, 1) * sink_ref[r:r + n, :]
        krope = krope + (lax.broadcasted_iota(jnp.int32, (1, QK_PAD), 1) == REF_LANE).astype(F32)
        for h in range(HEADS):
            c0 = h * QK_PAD
            k_out[0, r:r + n, c0:c0 + QK_PAD] = (kext[:, c0:c0 + QK_PAD] + krope).astype(BF16)
        vT = _dot_nt(wvT_ref[0], kvn)
        for h in range(HEADS):
            vT_out[0, h * V_EXT:h * V_EXT + VDIM, r:r + n] = vT[h * VDIM:(h + 1) * VDIM, :].astype(BF16)
            vT_out[0, h * V_EXT + VDIM:(h + 1) * V_EXT, r:r + n] = jnp.ones((V_EXT - VDIM, n), BF16)

    lane = lax.broadcasted_iota(jnp.int32, (CHUNK, SGU_CH), 1)
    bias = sgub_ref[0]
    for (r, n), z in zip(subs, zs):
        zsg = z[:, ZC_SGU:ZC_SGU + 2 * SGU_CH]
        zg = 0.5 * zsg * (1.0 + jnp.tanh(math.sqrt(2.0 / math.pi) * (zsg + 0.044715 * (zsg * zsg * zsg))))
        u = zg[:, :SGU_CH]
        v = _layernorm(zg[:, SGU_CH:], slng_ref[0], slnb_ref[0]).astype(BF16)
        for c in range(n // CHUNK):
            vc = v[c * CHUNK:(c + 1) * CHUNK, :]
            mixed = _dot(sguw_ref[0, 0], vc)
            for h in range(1, SGU_HEADS):
                mixed = jnp.where(lane >= h * SGU_HEAD_DIM, _dot(sguw_ref[0, h], vc), mixed)
            ys_out[0, r + c * CHUNK:r + (c + 1) * CHUNK, :] = (
                u[c * CHUNK:(c + 1) * CHUNK, :] * (mixed + bias)).astype(BF16)

    cw = convw_ref[0]
    accs = [jnp.zeros((n, CONV_CH), F32) for _, n in subs]
    n_a = (CONV_WIDTH + 8) // 8
    for b in range(8):
        src = ybuf
        if b:
            yshift[...] = ybuf[b:b + yshift.shape[0], :]
            src = yshift
        for a in range(n_a):
            k = 8 * a + b - 1
            if 0 <= k < CONV_WIDTH:
                for j, (r, n) in enumerate(subs):
                    accs[j] = accs[j] + src[r + 8 * a:r + 8 * a + n, :] * cw[k:k + 1, :]
    for j, (r, n) in enumerate(subs):
        yc = _layernorm(accs[j] + convb_ref[0], clng_ref[0], clnb_ref[0])
        yc_out[0, r:r + n, :] = (yc * _sigmoid(yc)).astype(BF16)


def _pre_call(layer, xc, tail, modsel, p, n_lat_tiles, n_ctx):
    bsz, _, d = xc.shape
    split = xc.shape[1] == n_lat_tiles * TOK_TILE
    ltot = (n_lat_tiles + 1) * TOK_TILE
    n_xblocks = xc.shape[1] // CONV_HALO
    tt = TOK_TILE
    n_tiles = ltot // tt
    hb = tt // CONV_HALO

    def lay(shape):
        nd = len(shape)
        return pl.BlockSpec((1,) + shape, lambda b, i: (layer,) + (0,) * nd)

    in_specs = [
        pl.BlockSpec((1, tt, d), lambda b, i: (b, jnp.minimum(i, xc.shape[1] // tt - 1), 0)),
        pl.BlockSpec((1, tt, d), lambda b, i: (b, 0, 0)),
        pl.BlockSpec((1, CONV_HALO, d), lambda b, i: (b, jnp.minimum(jnp.maximum(i * hb - 1, 0), n_xblocks - 1), 0)),
        pl.BlockSpec((1, CONV_HALO, d), lambda b, i: (b, jnp.minimum((i + 1) * hb, n_xblocks - 1), 0)),
        pl.BlockSpec((1, 1, 8, d), lambda b, i: (b, i // n_lat_tiles, 0, 0)),
        lay((1, d)),
        lay((d, Z_COLS)),
        lay((1, Q_RANK)),
        lay((HEADS * QK_PAD, Q_RANK)),
        lay((HEADS * ROPE, Q_RANK)),
        lay((1, KV_RANK)),
        lay((KV_RANK, HEADS * QK_PAD)),
        lay((HEADS * VDIM, KV_RANK)),
        pl.BlockSpec((tt, QK_PAD), lambda b, i: (i, 0)),
        pl.BlockSpec((tt, QK_PAD), lambda b, i: (i, 0)),
        pl.BlockSpec((ROPE, tt), lambda b, i: (0, i)),
        pl.BlockSpec((ROPE, tt), lambda b, i: (0, i)),
        lay((CONV_WIDTH, CONV_CH)),
        lay((1, CONV_CH)),
        lay((1, CONV_CH)),
        lay((1, CONV_CH)),
        lay((1, SGU_CH)),
        lay((1, SGU_CH)),
        lay((SGU_HEADS, CHUNK, CHUNK)),
        lay((CHUNK, SGU_CH)),
    ]
    out_specs = [
        pl.BlockSpec((1, HEADS * QK_PAD, tt), lambda b, i: (b, 0, i)),
        pl.BlockSpec((1, tt, HEADS * QK_PAD), lambda b, i: (b, i, 0)),
        pl.BlockSpec((1, HEADS * V_EXT, tt), lambda b, i: (b, 0, i)),
        pl.BlockSpec((1, tt, CONV_CH), lambda b, i: (b, i, 0)),
        pl.BlockSpec((1, tt, SGU_CH), lambda b, i: (b, i, 0)),
    ]
    out_shape = [
        jax.ShapeDtypeStruct((bsz, HEADS * QK_PAD, ltot), BF16),
        jax.ShapeDtypeStruct((bsz, ltot, HEADS * QK_PAD), BF16),
        jax.ShapeDtypeStruct((bsz, HEADS * V_EXT, ltot), BF16),
        jax.ShapeDtypeStruct((bsz, ltot, CONV_CH), BF16),
        jax.ShapeDtypeStruct((bsz, ltot, SGU_CH), BF16),
    ]
    return pl.pallas_call(
        functools.partial(_pre_kernel, n_lat_tiles=n_lat_tiles, n_ctx=n_ctx, split=split),
        grid=(bsz, n_tiles),
        in_specs=in_specs,
        out_specs=out_specs,
        out_shape=out_shape,
        scratch_shapes=[pltpu.VMEM((tt + 2 * CONV_HALO, CONV_CH), F32),
                        pltpu.VMEM((tt + 2 * CONV_HALO - 8, CONV_CH), F32)],
        compiler_params=_cparams(("arbitrary", "arbitrary")),
        name="pre",
    )(xc, tail, xc, xc, modsel, p["norm1_g"], p["w_in"], p["q_norm_g"], p["wqT"], p["wqrT"],
      p["kv_norm_g"], p["wk"], p["wvT"], p["cosk"], p["sink"], p["cosq"], p["sinq"],
      p["conv_w"], p["conv_b"], p["conv_ln_g"], p["conv_ln_b"], p["sgu_ln_g"], p["sgu_ln_b"],
      p["sgu_w"], p["sgu_bias"])


def _softmax_update(s, vt, m, acc):
    m_new = jnp.maximum(m, jnp.max(s, axis=0, keepdims=True))
    alpha = jnp.exp2(m - m_new)
    pr = jnp.exp2(s - m_new).astype(BF16)
    return m_new, alpha * acc + _dot(vt, pr)


def _bf16_exact(v):
    return v.astype(BF16).astype(F32)


def _softmax_finish(acc):
    return (acc[0:VDIM, :] / acc[VDIM:VDIM + 1, :]).astype(BF16)


def _attn_lat_kernel(qT_ref, k_ref, vT_ref, o_ref, acc_scr, *, n_lat, n_ctx):
    tq = qT_ref.shape[2]
    n_cb = tq // Q_BLOCK
    n_steps = n_lat // KV_STEP

    def cols(c):
        return slice(c * Q_BLOCK, (c + 1) * Q_BLOCK)

    def scores(o, size, c, r=None):
        q = qT_ref[0, :, cols(c)]
        if r is not None:
            first_row = lax.broadcasted_iota(jnp.int32, (REF_ROWS, Q_BLOCK), 0) == 0
            q = jnp.concatenate([q[0:REF_LANE], jnp.where(first_row, -r, 0.0).astype(BF16),
                                 q[REF_LANE + REF_ROWS:]], axis=0)
        return _dot(k_ref[0, pl.ds(o, size), :], q)

    def run(blocks, carry):
        rs, excess = list(carry[0]), list(carry[1])
        n = len(blocks)
        s, pa = {}, {}

        def soft(i):
            c = blocks[i][2]
            si = s.pop(i)
            mx = jnp.max(si, axis=0, keepdims=True)
            r_new = _bf16_exact(rs[c] + jnp.maximum(mx, 0.0))
            pa[i] = (jnp.exp2(si).astype(BF16), jnp.exp2(rs[c] - r_new))
            excess[c] = jnp.maximum(excess[c], mx)
            rs[c] = r_new

        for i in range(min(2, n)):
            s[i] = scores(*blocks[i], rs[blocks[i][2]])
        soft(0)
        for i, (o, size, c) in enumerate(blocks):
            if i + 2 < n:
                s[i + 2] = scores(*blocks[i + 2], rs[blocks[i + 2][2]])
            if i + 1 < n:
                soft(i + 1)
            p, alpha = pa.pop(i)
            acc_scr[:, cols(c)] = (acc_scr[:, cols(c)] + _dot(vT_ref[0, :, pl.ds(o, size)], p)) * alpha
        return tuple(rs), tuple(excess)

    def latent_steps(j, carry):
        offs = [pl.multiple_of((j * KV_UNROLL + u) * KV_STEP, KV_STEP) for u in range(KV_UNROLL)]
        return run([(o, KV_STEP, c) for o in offs for c in range(n_cb)], carry)

    acc_scr[...] = jnp.zeros(acc_scr.shape, F32)
    carry = (tuple(_bf16_exact(jnp.max(scores(n_lat, R0_KEYS, c), axis=0, keepdims=True)) for c in range(n_cb)),
             tuple(jnp.zeros((1, Q_BLOCK), F32) for _ in range(n_cb)))
    carry = run([(n_lat, n_ctx, c) for c in range(n_cb)], carry)
    _, excess = lax.fori_loop(0, n_steps // KV_UNROLL, latent_steps, carry)
    o_ref[0] = _softmax_finish(acc_scr[...])

    @pl.when(jnp.max(jnp.concatenate(excess, axis=1)) > STALE_MAX_HEADROOM)
    def _():
        def exact_step(o, size, carry):
            s = _dot(k_ref[0, pl.ds(o, size), :], qT_ref[0])
            return _softmax_update(s, vT_ref[0, :, pl.ds(o, size)], *carry)

        carry = lax.fori_loop(
            0, n_steps, lambda t, cr: exact_step(pl.multiple_of(t * KV_STEP, KV_STEP), KV_STEP, cr),
            (jnp.full((1, tq), -jnp.inf, F32), jnp.zeros((V_EXT, tq), F32)))
        _, acc = exact_step(n_lat, n_ctx, carry)
        o_ref[0] = _softmax_finish(acc)


def _attn_ctx_kernel(qT_ref, k_ref, vT_ref, a_hbm_ref, o_ref):
    del a_hbm_ref
    tq = qT_ref.shape[2]
    for h in range(HEADS):
        s = _dot(k_ref[0, :, h * QK_PAD:(h + 1) * QK_PAD], qT_ref[0, h * QK_PAD:(h + 1) * QK_PAD, :])
        _, acc = _softmax_update(s, vT_ref[0, h * V_EXT:(h + 1) * V_EXT, :],
                                 jnp.full((1, tq), -jnp.inf, F32), jnp.zeros((V_EXT, tq), F32))
        o_ref[0, h * VDIM:(h + 1) * VDIM, 0:tq] = _softmax_finish(acc)
    if o_ref.shape[2] > tq:
        o_ref[0, :, tq:] = jnp.zeros((o_ref.shape[1], o_ref.shape[2] - tq), BF16)


def _attn_call(qT, k, vT, n_lat, n_ctx, with_ctx):
    bsz, _, ltot = qT.shape
    tq = Q_TILE
    assert n_lat % (KV_STEP * KV_UNROLL) == 0 and n_lat % tq == 0 and tq % Q_BLOCK == 0
    aT = pl.pallas_call(
        functools.partial(_attn_lat_kernel, n_lat=n_lat, n_ctx=n_ctx),
        grid=(bsz, HEADS, n_lat // tq),
        in_specs=[
            pl.BlockSpec((1, QK_PAD, tq), lambda b, h, q: (b, h, q)),
            pl.BlockSpec((1, ltot, QK_PAD), lambda b, h, q: (b, 0, h)),
            pl.BlockSpec((1, V_EXT, ltot), lambda b, h, q: (b, h, 0)),
        ],
        out_specs=pl.BlockSpec((1, VDIM, tq), lambda b, h, q: (b, h, q)),
        out_shape=jax.ShapeDtypeStruct((bsz, HEADS * VDIM, ltot), BF16),
        scratch_shapes=[pltpu.VMEM((V_EXT, tq), F32)],
        compiler_params=_cparams(("arbitrary", "arbitrary", "arbitrary")),
        name="attn_lat",
    )(qT, k, vT)
    if not with_ctx:
        return aT
    cb = n_lat // n_ctx
    return pl.pallas_call(
        _attn_ctx_kernel,
        grid=(bsz,),
        in_specs=[
            pl.BlockSpec((1, HEADS * QK_PAD, n_ctx), lambda b: (b, 0, cb)),
            pl.BlockSpec((1, n_ctx, HEADS * QK_PAD), lambda b: (b, cb, 0)),
            pl.BlockSpec((1, HEADS * V_EXT, n_ctx), lambda b: (b, 0, cb)),
            pl.BlockSpec(memory_space=pl.ANY),
        ],
        out_specs=pl.BlockSpec((1, HEADS * VDIM, TOK_TILE), lambda b: (b, 0, n_lat // TOK_TILE)),
        out_shape=jax.ShapeDtypeStruct(aT.shape, aT.dtype),
        input_output_aliases={3: 0},
        compiler_params=_cparams(("arbitrary",)),
        name="attn_ctx",
    )(qT, k, vT, aT)


def _post_kernel(x_ref, tail_ref, aT_ref, yc_ref, ys_ref, mod_ref, n2g_ref, wout_ref, w1_ref, w2_ref, fg_ref,
                 o_ref, *, final, n_lat_tiles, split):
    tt, d = x_ref.shape[1], x_ref.shape[2]
    is_ctx = pl.program_id(1) == n_lat_tiles
    d_ff = w1_ref.shape[2]
    n_attn = aT_ref.shape[1]
    subs = [(r, SUB_TILE) for r in range(0, tt, SUB_TILE)]
    g1 = mod_ref[0, 0, 2:3, :]
    sh2 = mod_ref[0, 0, 3:4, :]
    gain2 = n2g_ref[0] * (1.0 + mod_ref[0, 0, 4:5, :])
    g2 = mod_ref[0, 0, 5:6, :]
    x1s, hs = [], []
    for r, n in subs:
        y = _dot_tn(aT_ref[0, :, r:r + n], wout_ref[0, 0:n_attn, :])
        y = y + _dot(yc_ref[0, r:r + n, :], wout_ref[0, n_attn:n_attn + CONV_CH, :])
        y = y + _dot(ys_ref[0, r:r + n, :], wout_ref[0, n_attn + CONV_CH:d, :])
        xt = x_ref[0, r:r + n, :]
        if split:
            xt = jnp.where(is_ctx, tail_ref[0, r:r + n, :], xt)
        x1 = xt + g1 * y
        x1s.append(x1)
        hs.append((_rms(x1, gain2) + sh2).astype(BF16))
    for (r, n), x1, h in zip(subs, x1s, hs):
        ff = jnp.zeros(x1.shape, F32)
        for c in range(d_ff // FF_CHUNK):
            f = jnp.maximum(_dot(h, w1_ref[0, :, c * FF_CHUNK:(c + 1) * FF_CHUNK]), 0.0)
            ff = ff + _dot((f * f).astype(BF16), w2_ref[0, c * FF_CHUNK:(c + 1) * FF_CHUNK, :])
        x2 = x1 + g2 * ff
        if final:
            x2 = _rms(x2, fg_ref[...])
        o_ref[0, r:r + n, :] = x2


def _post_call(layer, xc, tail, aT, yc, ys, modsel, p, n_lat_tiles, n_out_tiles, final):
    bsz, _, d = xc.shape
    split = xc.shape[1] == n_lat_tiles * TOK_TILE
    tt = TOK_TILE
    d_ff = p["w_ff1"].shape[2]

    def lay(shape):
        nd = len(shape)
        return pl.BlockSpec((1,) + shape, lambda b, i: (layer,) + (0,) * nd,
                            pipeline_mode=pl.Buffered(1))

    return pl.pallas_call(
        functools.partial(_post_kernel, final=final, n_lat_tiles=n_lat_tiles, split=split),
        grid=(bsz, n_out_tiles),
        in_specs=[
            pl.BlockSpec((1, tt, d), lambda b, i: (b, jnp.minimum(i, xc.shape[1] // tt - 1), 0)),
            pl.BlockSpec((1, tt, d), lambda b, i: (b, 0, 0)),
            pl.BlockSpec((1, HEADS * VDIM, tt), lambda b, i: (b, 0, i)),
            pl.BlockSpec((1, tt, CONV_CH), lambda b, i: (b, i, 0)),
            pl.BlockSpec((1, tt, SGU_CH), lambda b, i: (b, i, 0)),
            pl.BlockSpec((1, 1, 8, d), lambda b, i: (b, i // n_lat_tiles, 0, 0)),
            pl.BlockSpec((1, 1, d), lambda b, i: (layer, 0, 0)),
            lay((d, d)),
            lay((d, d_ff)),
            lay((d_ff, d)),
            pl.BlockSpec((1, d), lambda b, i: (0, 0)),
        ],
        out_specs=pl.BlockSpec((1, tt, d), lambda b, i: (b, i, 0)),
        out_shape=jax.ShapeDtypeStruct((bsz, n_out_tiles * tt, d), F32),
        compiler_params=_cparams(("arbitrary", "arbitrary")),
        name="post",
    )(xc, tail, aT, yc, ys, modsel, p["norm2_g"], p["w_out"], p["w_ff1"], p["w_ff2"], p["final_g"])


def _rotate_half_cols(w):
    n = ROPE // 4
    return jnp.concatenate([-w[..., n:2 * n], w[..., 0:n], -w[..., 3 * n:4 * n], w[..., 2 * n:3 * n]], -1)


def _prepare(n_lat, n_tail, norm1_g, norm2_g, w_in, q_norm_g, w_uq, kv_norm_g, w_ukv, conv_w, conv_b,
             conv_ln_g, conv_ln_b, sgu_ln_g, sgu_ln_b, sgu_w, sgu_b, w_out, w_ff1, w_ff2, final_g):
    n_layers, d, _ = w_in.shape
    o_q, o_kv, o_kr = 0, Q_RANK, Q_RANK + KV_RANK
    o_conv = o_kr + ROPE
    o_sgu = o_conv + 2 * CONV_CH
    w_kr = w_in[:, :, o_kr:o_kr + ROPE]

    gap = jnp.zeros((n_layers, d, QK_PAD // 2 - ROPE), w_in.dtype)
    kr_slab = jnp.concatenate([_rotate_half_cols(w_kr), gap, w_kr, gap], -1)
    w_in_x = jnp.concatenate([
        w_in[:, :, o_q:o_q + Q_RANK], w_in[:, :, o_kv:o_kv + KV_RANK],
        w_in[:, :, o_conv:o_conv + 2 * CONV_CH], w_in[:, :, o_sgu:o_sgu + 2 * SGU_CH],
        kr_slab], -1).astype(BF16)

    wq = w_uq.reshape(n_layers, Q_RANK, HEADS, NOPE + ROPE)
    wq_pad = jnp.pad(wq, ((0, 0), (0, 0), (0, 0), (0, QK_PAD - NOPE - ROPE)))
    wqT = wq_pad.reshape(n_layers, Q_RANK, HEADS * QK_PAD).transpose(0, 2, 1).astype(BF16)
    wqr = _rotate_half_cols(wq[..., NOPE:])
    wqrT = wqr.reshape(n_layers, Q_RANK, HEADS * ROPE).transpose(0, 2, 1).astype(BF16)

    wkv = w_ukv.reshape(n_layers, KV_RANK, HEADS, NOPE + VDIM)
    wk = jnp.pad(wkv[..., :NOPE], ((0, 0), (0, 0), (0, 0), (0, QK_PAD - NOPE)))
    wk = wk.reshape(n_layers, KV_RANK, HEADS * QK_PAD).astype(BF16)
    wvT = wkv[..., NOPE:].reshape(n_layers, KV_RANK, HEADS * VDIM).transpose(0, 2, 1).astype(BF16)

    t = jnp.arange(n_lat, dtype=jnp.int32)
    n = ROPE // 4
    inv = 1.0 / (ROPE_BASE ** (jnp.arange(n, dtype=F32) / n))
    ang_r = (t // GRID_W).astype(F32)[:, None] * inv
    ang_c = (t % GRID_W).astype(F32)[:, None] * inv
    cos32 = jnp.concatenate([jnp.cos(ang_r)] * 2 + [jnp.cos(ang_c)] * 2, -1)
    sin32 = jnp.concatenate([jnp.sin(ang_r)] * 2 + [jnp.sin(ang_c)] * 2, -1)
    cos32 = jnp.concatenate([cos32, jnp.ones((n_tail, ROPE), F32)], 0)
    sin32 = jnp.concatenate([sin32, jnp.zeros((n_tail, ROPE), F32)], 0)
    pad = ((0, 0), (NOPE, QK_PAD - NOPE - ROPE))

    sgu_bias = jnp.broadcast_to(jnp.swapaxes(sgu_b, 1, 2)[:, :, :, None],
                                (n_layers, CHUNK, SGU_HEADS, SGU_HEAD_DIM)).reshape(n_layers, CHUNK, SGU_CH)

    def row(a):
        return a.reshape(n_layers, 1, a.shape[-1])

    return dict(
        norm1_g=row(norm1_g), norm2_g=row(norm2_g), w_in=w_in_x, q_norm_g=row(q_norm_g), wqT=wqT, wqrT=wqrT,
        kv_norm_g=row(kv_norm_g), wk=wk, wvT=wvT,
        cosk=jnp.pad(cos32, pad), sink=jnp.pad(sin32, pad), cosq=cos32.T, sinq=sin32.T,
        conv_w=conv_w, conv_b=row(conv_b), conv_ln_g=row(conv_ln_g), conv_ln_b=row(conv_ln_b),
        sgu_ln_g=row(sgu_ln_g), sgu_ln_b=row(sgu_ln_b), sgu_w=sgu_w.astype(BF16), sgu_bias=sgu_bias,
        w_out=w_out.astype(BF16), w_ff1=w_ff1.astype(BF16), w_ff2=w_ff2.astype(BF16),
        final_g=final_g.reshape(1, -1))


def kernel(x, c, ctx, c_ctx, ada_w, ada_b, norm1_g, norm2_g, w_in, q_norm_g, w_uq, kv_norm_g, w_ukv,
           conv_w, conv_b, conv_ln_g, conv_ln_b, sgu_ln_g, sgu_ln_b, sgu_w, sgu_b, w_out, w_ff1, w_ff2,
           final_g):
    bsz, n_lat, d = x.shape
    n_ctx = ctx.shape[1]
    n_layers = w_in.shape[0]
    assert n_lat % TOK_TILE == 0 and n_lat % GRID_W == 0
    assert n_ctx <= TOK_TILE and n_ctx % SUB_TILE == 0
    assert n_lat % n_ctx == 0
    n_lat_tiles = n_lat // TOK_TILE
    n_tiles = n_lat_tiles + 1

    p = _prepare(n_lat, TOK_TILE, norm1_g, norm2_g, w_in, q_norm_g, w_uq, kv_norm_g, w_ukv, conv_w, conv_b,
                 conv_ln_g, conv_ln_b, sgu_ln_g, sgu_ln_b, sgu_w, sgu_b, w_out, w_ff1, w_ff2, final_g)

    rows = -(-(bsz + 1) // 8) * 8
    cvec = jnp.concatenate([c, c_ctx[None, :], jnp.zeros((rows - bsz - 1, d), F32)], 0)
    mods = _ada_call(cvec, ada_w, ada_b).reshape(n_layers, rows, 6, d)
    m_lat = mods[:, :bsz]
    m_ctx = jnp.broadcast_to(mods[:, bsz][:, None], m_lat.shape)
    modsel = jnp.pad(jnp.stack([m_lat, m_ctx], 2), ((0, 0), (0, 0), (0, 0), (0, 2), (0, 0)))

    tail = jnp.concatenate([ctx, jnp.zeros((bsz, TOK_TILE - n_ctx, d), x.dtype)], 1)
    xc = x
    for layer in range(n_layers):
        last = layer == n_layers - 1
        n_act = n_lat_tiles if last else n_tiles
        qT, k, vT, yc, ys = _pre_call(layer, xc, tail, modsel[layer], p, n_lat_tiles, n_ctx)
        aT = _attn_call(qT, k, vT, n_lat, n_ctx, not last)
        xc = _post_call(layer, xc, tail, aT, yc, ys, modsel[layer], p, n_lat_tiles, n_act, last)
    return xc
```
